```python
import jax, jax.numpy as jnp
from jax import lax
import numpy as np

D_MODEL = 2048
BATCH = 16
SEQ = 256
DEPTH = 4
DEC_BATCH = 4
DEC_SEQ = 2048
PAST_LEN = 512

GRID_W = 64
N_MIXERS = 3
N_MOD = 9
NORM_EPS = 1e-6
D_FF = 5632
Q_BLOCK = 128
NEG_INF = -1e30

NA_HEADS = 16
NA_HEAD_DIM = D_MODEL // NA_HEADS
NA_WIN_ROWS = 8
NA_WIN_COLS = 16

MLA_HEADS = 16
MLA_Q_LORA = 768
MLA_KV_LORA = 512
MLA_NOPE = 128
MLA_ROPE = 64
MLA_V = 128
MLA_SCALE = (MLA_NOPE + MLA_ROPE) ** -0.5
ROPE_THETA = 10000.0

GLA_HEADS = 4
GLA_DK = D_MODEL // 2 // GLA_HEADS
GLA_DV = D_MODEL // GLA_HEADS
GLA_GATE_RANK = 16
GLA_TAU = 16.0
GLA_CHUNK = 64

kernel_name = 'hybrid_na_mla_gla_diffusion_step'


def _n_slots(kind):
    return len(range(kind, DEPTH, N_MIXERS))


def rmsnorm(x, g):
    x32 = x.astype(jnp.float32)
    y = x32 * lax.rsqrt(jnp.mean(x32 * x32, axis=-1, keepdims=True) + NORM_EPS)
    return y.astype(x.dtype) * g


def adaln(cond, w, b):
    m = jax.nn.silu(cond) @ w + b
    return m.reshape(cond.shape[:-1] + (1, N_MOD, D_MODEL))


def mod_parts(mod, j):
    return mod[..., 3 * j, :], mod[..., 3 * j + 1, :], mod[..., 3 * j + 2, :]


def modulate(x, g, shift, scale):
    return rmsnorm(x, g) * (1 + scale) + shift


def swiglu(h, w_in, w_out):
    gate, up = jnp.split(h @ w_in, 2, axis=-1)
    return (jax.nn.silu(gate) * up) @ w_out


def ffn_half(x, mod, j, g, w_in, w_out):
    shift, scale, gate = mod_parts(mod, j)
    return x + 0.5 * gate * swiglu(modulate(x, g, shift, scale), w_in, w_out)


def attend(q, k, v, scale):
    b, lq, h, dq = q.shape
    nb = lq // Q_BLOCK
    qb = jnp.moveaxis(q.reshape(b, nb, Q_BLOCK, h, dq), 1, 0)

    def block(qi):
        s = jnp.einsum('bqhd,bkhd->bhqk', qi, k).astype(jnp.float32) * scale
        p = jax.nn.softmax(s, axis=-1).astype(v.dtype)
        return jnp.einsum('bhqk,bkhd->bqhd', p, v)

    o = lax.map(block, qb)
    return jnp.moveaxis(o, 0, 1).reshape(b, lq, h, v.shape[-1])


def grid_positions(length):
    t = jnp.arange(length)
    return t // GRID_W, t % GRID_W


def rope_axis(x, pos):
    d = x.shape[-1]
    inv = ROPE_THETA ** (-jnp.arange(0, d, 2, dtype=jnp.float32) / d)
    ang = pos.astype(jnp.float32)[:, None] * inv[None]
    ang = ang.reshape((pos.shape[0],) + (1,) * (x.ndim - 3) + (d // 2,))
    cos = jnp.cos(ang).astype(x.dtype)
    sin = jnp.sin(ang).astype(x.dtype)
    x1, x2 = jnp.split(x, 2, axis=-1)
    return jnp.concatenate([x1 * cos - x2 * sin, x1 * sin + x2 * cos], axis=-1)


def rope2d(x, rows, cols):
    xr, xc = jnp.split(x, 2, axis=-1)
    return jnp.concatenate([rope_axis(xr, rows), rope_axis(xc, cols)], axis=-1)


def na_latent(q, k, v, k_ctx, v_ctx, rpb):
    b, l, h, dh = q.shape
    rows = l // GRID_W
    kh = min(NA_WIN_ROWS, rows)
    kw = NA_WIN_COLS
    scale = dh ** -0.5
    qr = jnp.moveaxis(q.reshape(b, rows, GRID_W, h, dh), 1, 0)
    kr = k.reshape(b, rows, GRID_W, h, dh)
    vr = v.reshape(b, rows, GRID_W, h, dh)
    r_idx = jnp.arange(rows)
    row_start = jnp.clip(r_idx - kh // 2, 0, rows - kh)
    col = jnp.arange(GRID_W)
    col_start = jnp.clip(col - kw // 2, 0, GRID_W - kw)
    col_ok = (col[None, :] >= col_start[:, None]) & (col[None, :] < col_start[:, None] + kw)
    col_off = jnp.clip(col[None, :] - col[:, None] + kw - 1, 0, 2 * kw - 2)
    rpb_cols = rpb[:, :, col_off]
    n_loc = kh * GRID_W

    def row_block(args):
        q_i, r, rs = args
        k_i = lax.dynamic_slice_in_dim(kr, rs, kh, axis=1).reshape(b, n_loc, h, dh)
        v_i = lax.dynamic_slice_in_dim(vr, rs, kh, axis=1).reshape(b, n_loc, h, dh)
        s_loc = jnp.einsum('bqhd,bkhd->bhqk', q_i, k_i).astype(jnp.float32) * scale
        s_loc = s_loc.reshape(b, h, GRID_W, kh, GRID_W)
        dr = rs + jnp.arange(kh) - r + (NA_WIN_ROWS - 1)
        bias = jnp.transpose(jnp.take(rpb_cols, dr, axis=1), (0, 2, 1, 3))
        s_loc = jnp.where(col_ok[None, None, :, None, :], s_loc + bias.astype(jnp.float32)[None], NEG_INF)
        s_ctx = jnp.einsum('bqhd,bkhd->bhqk', q_i, k_ctx).astype(jnp.float32) * scale
        s = jnp.concatenate([s_loc.reshape(b, h, GRID_W, n_loc), s_ctx], axis=-1)
        p = jax.nn.softmax(s, axis=-1).astype(v.dtype)
        return (jnp.einsum('bhqk,bkhd->bqhd', p[..., :n_loc], v_i)
                + jnp.einsum('bhqk,bkhd->bqhd', p[..., n_loc:], v_ctx))

    o = lax.map(row_block, (qr, r_idx, row_start))
    return jnp.moveaxis(o, 0, 1).reshape(b, l, h * dh)


def na_project(h, w_qkv):
    b, l, _ = h.shape
    qkv = (h @ w_qkv).reshape(b, l, 3, NA_HEADS, NA_HEAD_DIM)
    return qkv[:, :, 0], qkv[:, :, 1], qkv[:, :, 2]


def mla_project(h, w_down, g_q, w_uq, g_kv):
    b, l, _ = h.shape
    cq, ckv, kr = jnp.split(h @ w_down, [MLA_Q_LORA, MLA_Q_LORA + MLA_KV_LORA], axis=-1)
    q = (rmsnorm(cq, g_q) @ w_uq).reshape(b, l, MLA_HEADS, MLA_NOPE + MLA_ROPE)
    return q, rmsnorm(ckv, g_kv), kr


def mla_keys(ckv, kr, w_ukv):
    b, l, _ = ckv.shape
    kv = (ckv @ w_ukv).reshape(b, l, MLA_HEADS, MLA_NOPE + MLA_V)
    k_nope, v = jnp.split(kv, [MLA_NOPE], axis=-1)
    k = jnp.concatenate([k_nope, jnp.broadcast_to(kr[:, :, None, :], (b, l, MLA_HEADS, MLA_ROPE))], axis=-1)
    return k, v


def gla_project(h, w_in, w_gd, w_gu, b_g):
    b, l, _ = h.shape
    dk_t = GLA_HEADS * GLA_DK
    dv_t = GLA_HEADS * GLA_DV
    q, k, v, g = jnp.split(h @ w_in, [dk_t, 2 * dk_t, 2 * dk_t + dv_t], axis=-1)
    q = q.reshape(b, l, GLA_HEADS, GLA_DK)
    k = k.reshape(b, l, GLA_HEADS, GLA_DK)
    v = v.reshape(b, l, GLA_HEADS, GLA_DV)
    la_f = jax.nn.log_sigmoid(((h @ w_gd[0]) @ w_gu[0] + b_g[0]).astype(jnp.float32)) / GLA_TAU
    la_b = jax.nn.log_sigmoid(((h @ w_gd[1]) @ w_gu[1] + b_g[1]).astype(jnp.float32)) / GLA_TAU
    return q, k, v, g, la_f.reshape(b, l, GLA_HEADS, GLA_DK), la_b.reshape(b, l, GLA_HEADS, GLA_DK)


def gla_chunked(q, k, v, log_a, s0):
    b, l, h, dk = q.shape
    dv = v.shape[-1]
    n = l // GLA_CHUNK
    f32 = jnp.float32

    def chunks(t):
        return t.astype(f32).reshape(b, n, GLA_CHUNK, h, t.shape[-1])

    qc = chunks(q) * dk ** -0.5
    kc = chunks(k)
    vc = chunks(v)
    cum = jnp.cumsum(chunks(log_a), axis=2)
    last = cum[:, :, -1:]
    q_dec = qc * jnp.exp(cum)
    k_inv = kc * jnp.exp(-cum)
    k_end = kc * jnp.exp(last - cum)
    causal = jnp.tril(jnp.ones((GLA_CHUNK, GLA_CHUNK), dtype=bool))
    att = jnp.where(causal, jnp.einsum('bnchk,bnshk->bnhcs', q_dec, k_inv), 0.0)
    o_intra = jnp.einsum('bnhcs,bnshv->bnchv', att, vc)

    def step(state, inp):
        q_i, k_i, v_i, dec_i = inp
        o_i = jnp.einsum('bchk,bhkv->bchv', q_i, state)
        state = dec_i[..., None] * state + jnp.einsum('bchk,bchv->bhkv', k_i, v_i)
        return state, o_i

    xs = (jnp.moveaxis(q_dec, 1, 0), jnp.moveaxis(k_end, 1, 0), jnp.moveaxis(vc, 1, 0),
          jnp.moveaxis(jnp.exp(last[:, :, 0]), 1, 0))
    s_fin, o_inter = lax.scan(step, s0.astype(f32), xs)
    o = o_intra + jnp.moveaxis(o_inter, 0, 1)
    return o.reshape(b, l, h, dv).astype(v.dtype), s_fin


def gla_bidir(q, k, v, la_f, la_b, s_f, s_b):
    o_f, s_f_new = gla_chunked(q, k, v, la_f, s_f)
    o_b, s_b_new = gla_chunked(jnp.flip(q, 1), jnp.flip(k, 1), jnp.flip(v, 1), jnp.flip(la_b, 1), s_b)
    return o_f + jnp.flip(o_b, 1), s_f_new, s_b_new


def gla_out(o, g, g_norm, w_o):
    b, l = o.shape[:2]
    o = rmsnorm(o, g_norm).reshape(b, l, GLA_HEADS * GLA_DV) * jax.nn.silu(g)
    return o @ w_o


def setup_inputs(seed: int = 0) -> dict:
    key = jax.random.key(seed)
    ks = iter(jax.random.split(key, 40))

    def nrm(shape, scale=1.0):
        return jax.random.normal(next(ks), shape, jnp.float32) * scale

    def gain(shape):
        return 1.0 + nrm(shape, 0.02)

    n_a, n_b, n_c = _n_slots(0), _n_slots(1), _n_slots(2)
    D = D_MODEL
    return {
        'x_prompt': nrm((BATCH, SEQ, D)),
        'x_sample': nrm((DEC_BATCH, DEC_SEQ, D)),
        'cache_na_k': nrm((DEC_BATCH, n_a, PAST_LEN, NA_HEADS, NA_HEAD_DIM)),
        'cache_na_v': nrm((DEC_BATCH, n_a, PAST_LEN, NA_HEADS, NA_HEAD_DIM)),
        'cache_mla_ckv': nrm((DEC_BATCH, n_b, PAST_LEN, MLA_KV_LORA)),
        'cache_mla_krope': nrm((DEC_BATCH, n_b, PAST_LEN, MLA_ROPE)),
        'state_gla': nrm((DEC_BATCH, n_c, 2, GLA_HEADS, GLA_DK, GLA_DV)),
        'c': nrm((DEC_BATCH, D)),
        'c_ctx': nrm((D,)),
        'norm_g': gain((DEPTH, 3, D)),
        'w_ada': nrm((DEPTH, D, N_MOD * D), 0.5 * D ** -0.5),
        'b_ada': nrm((DEPTH, N_MOD * D), 0.02),
        'w_ffn_in': nrm((DEPTH, 2, D, 2 * D_FF), D ** -0.5),
        'w_ffn_out': nrm((DEPTH, 2, D_FF, D), D_FF ** -0.5),
        'w_na_qkv': nrm((n_a, D, 3 * D), D ** -0.5),
        'w_na_o': nrm((n_a, D, D), D ** -0.5),
        'na_rpb': nrm((n_a, NA_HEADS, 2 * NA_WIN_ROWS - 1, 2 * NA_WIN_COLS - 1), 0.1),
        'w_mla_down': nrm((n_b, D, MLA_Q_LORA + MLA_KV_LORA + MLA_ROPE), D ** -0.5),
        'g_mla_q': gain((n_b, MLA_Q_LORA)),
        'w_mla_uq': nrm((n_b, MLA_Q_LORA, MLA_HEADS * (MLA_NOPE + MLA_ROPE)), MLA_Q_LORA ** -0.5),
        'g_mla_kv': gain((n_b, MLA_KV_LORA)),
        'w_mla_ukv': nrm((n_b, MLA_KV_LORA, MLA_HEADS * (MLA_NOPE + MLA_V)), MLA_KV_LORA ** -0.5),
        'w_mla_o': nrm((n_b, MLA_HEADS * MLA_V, D), (MLA_HEADS * MLA_V) ** -0.5),
        'w_gla_in': nrm((n_c, D, 2 * GLA_HEADS * GLA_DK + 2 * GLA_HEADS * GLA_DV), D ** -0.5),
        'w_gla_gate_down': nrm((n_c, 2, D, GLA_GATE_RANK), D ** -0.5),
        'w_gla_gate_up': nrm((n_c, 2, GLA_GATE_RANK, GLA_HEADS * GLA_DK), GLA_GATE_RANK ** -0.5),
        'b_gla_gate': nrm((n_c, 2, GLA_HEADS * GLA_DK), 0.1),
        'g_gla_norm': gain((n_c, GLA_DV)),
        'w_gla_o': nrm((n_c, GLA_HEADS * GLA_DV, D), (GLA_HEADS * GLA_DV) ** -0.5),
        'final_norm_g': gain((D,)),
    }


def reference(x_prompt, x_sample, cache_na_k, cache_na_v, cache_mla_ckv, cache_mla_krope, state_gla, c, c_ctx,
              norm_g, w_ada, b_ada, w_ffn_in, w_ffn_out, w_na_qkv, w_na_o, na_rpb,
              w_mla_down, g_mla_q, w_mla_uq, g_mla_kv, w_mla_ukv, w_mla_o,
              w_gla_in, w_gla_gate_down, w_gla_gate_up, b_gla_gate, g_gla_norm, w_gla_o, final_norm_g):
    xc, xl = x_prompt, x_sample
    bc, lc, _ = x_prompt.shape
    bl, ll, _ = x_sample.shape
    rows, cols = grid_positions(ll)
    na_k_list, na_v_list, ckv_list, kr_list, gla_list = [], [], [], [], []
    for i in range(DEPTH):
        kind, slot = i % N_MIXERS, i // N_MIXERS
        mod_c = adaln(c_ctx, w_ada[i], b_ada[i])
        mod_l = adaln(c, w_ada[i], b_ada[i])
        xc = ffn_half(xc, mod_c, 0, norm_g[i, 0], w_ffn_in[i, 0], w_ffn_out[i, 0])
        xl = ffn_half(xl, mod_l, 0, norm_g[i, 0], w_ffn_in[i, 0], w_ffn_out[i, 0])
        sh_c, sc_c, gt_c = mod_parts(mod_c, 1)
        sh_l, sc_l, gt_l = mod_parts(mod_l, 1)
        hc = modulate(xc, norm_g[i, 1], sh_c, sc_c)
        hl = modulate(xl, norm_g[i, 1], sh_l, sc_l)
        if kind == 0:
            q_c, k_c, v_c = na_project(hc, w_na_qkv[slot])
            o_c = attend(q_c, k_c, v_c, NA_HEAD_DIM ** -0.5).reshape(bc, lc, D_MODEL) @ w_na_o[slot]
            na_k_list.append(k_c)
            na_v_list.append(v_c)
            q_l, k_l, v_l = na_project(hl, w_na_qkv[slot])
            o_l = na_latent(q_l, k_l, v_l, cache_na_k[:, slot], cache_na_v[:, slot], na_rpb[slot]) @ w_na_o[slot]
        elif kind == 1:
            q_c, ckv_c, kr_c = mla_project(hc, w_mla_down[slot], g_mla_q[slot], w_mla_uq[slot], g_mla_kv[slot])
            k_c, v_c = mla_keys(ckv_c, kr_c, w_mla_ukv[slot])
            o_c = attend(q_c, k_c, v_c, MLA_SCALE).reshape(bc, lc, MLA_HEADS * MLA_V) @ w_mla_o[slot]
            ckv_list.append(ckv_c)
            kr_list.append(kr_c)
            q_l, ckv_l, kr_l = mla_project(hl, w_mla_down[slot], g_mla_q[slot], w_mla_uq[slot], g_mla_kv[slot])
            q_l = jnp.concatenate([q_l[..., :MLA_NOPE], rope2d(q_l[..., MLA_NOPE:], rows, cols)], axis=-1)
            kr_l = rope2d(kr_l, rows, cols)
            k_l, v_l = mla_keys(jnp.concatenate([ckv_l, cache_mla_ckv[:, slot]], axis=1),
                                jnp.concatenate([kr_l, cache_mla_krope[:, slot]], axis=1), w_mla_ukv[slot])
            o_l = attend(q_l, k_l, v_l, MLA_SCALE).reshape(bl, ll, MLA_HEADS * MLA_V) @ w_mla_o[slot]
        else:
            q, k, v, g, la_f, la_b = gla_project(hc, w_gla_in[slot], w_gla_gate_down[slot], w_gla_gate_up[slot], b_gla_gate[slot])
            zeros = jnp.zeros((bc, GLA_HEADS, GLA_DK, GLA_DV), jnp.float32)
            o, s_f, s_b = gla_bidir(q, k, v, la_f, la_b, zeros, zeros)
            o_c = gla_out(o, g, g_gla_norm[slot], w_gla_o[slot])
            gla_list.append(jnp.stack([s_f, s_b], axis=1).astype(x_prompt.dtype))
            q, k, v, g, la_f, la_b = gla_project(hl, w_gla_in[slot], w_gla_gate_down[slot], w_gla_gate_up[slot], b_gla_gate[slot])
            o, _, _ = gla_bidir(q, k, v, la_f, la_b, state_gla[:, slot, 0], state_gla[:, slot, 1])
            o_l = gla_out(o, g, g_gla_norm[slot], w_gla_o[slot])
        xc = xc + gt_c * o_c
        xl = xl + gt_l * o_l
        xc = ffn_half(xc, mod_c, 2, norm_g[i, 2], w_ffn_in[i, 1], w_ffn_out[i, 1])
        xl = ffn_half(xl, mod_l, 2, norm_g[i, 2], w_ffn_in[i, 1], w_ffn_out[i, 1])
    y_prompt = rmsnorm(xc, final_norm_g)
    y_sample = rmsnorm(xl, final_norm_g)
    new_na_k = jnp.stack(na_k_list, axis=1)
    new_na_v = jnp.stack(na_v_list, axis=1)
    new_mla_ckv = jnp.stack(ckv_list, axis=1)
    new_mla_krope = jnp.stack(kr_list, axis=1)
    new_gla_state = jnp.stack(gla_list, axis=1)
    return (y_prompt, y_sample, new_na_k, new_na_v, new_mla_ckv, new_mla_krope, new_gla_state)
```

```python
import functools

import jax
import jax.numpy as jnp
from jax import lax
from jax.experimental import pallas as pl
from jax.experimental.pallas import tpu as pltpu

F32 = jnp.float32
BF16 = jnp.bfloat16

D = 2048
BATCH, SEQ = 16, 256
DEC_BATCH, DEC_SEQ = 4, 2048
PAST = 512
DEPTH = 4
N_MOD = 9
EPS = 1e-6
D_FF = 5632
GRID_W = 64
NEG_INF = -1e30

N_CTX = BATCH * SEQ
N_LAT = DEC_BATCH * DEC_SEQ
N_TOK = N_CTX + N_LAT
N_GROUPS = 8

NA_HEADS, NA_DH = 16, 128
NA_WIN_ROWS, NA_WIN_COLS = 8, 16
LAT_ROWS = DEC_SEQ // GRID_W

MLA_HEADS = 16
MLA_Q_LORA, MLA_KV_LORA = 768, 512
MLA_NOPE, MLA_ROPE, MLA_V = 128, 64, 128
MLA_SCALE = (MLA_NOPE + MLA_ROPE) ** -0.5
ROPE_THETA = 10000.0
MLA_DOWN_N = 1536

GLA_HEADS, GLA_DK, GLA_DV = 4, 256, 512
GLA_RANK = 16
GLA_TAU = 16.0
GLA_CHUNK = 64
GLA_PROJ_N = 2 * GLA_HEADS * GLA_DK + 2 * GLA_HEADS * GLA_DV + 128

VMEM_LIMIT = 56 * 1024 * 1024


def _cparams(sem):
    return pltpu.CompilerParams(dimension_semantics=sem, vmem_limit_bytes=VMEM_LIMIT)


def _group_of_row(r0):
    return jnp.where(r0 < N_CTX, 0, 1 + (r0 - N_CTX) // DEC_SEQ)


def _silu(x):
    return x / (1.0 + jnp.exp(-x))


def _rms(x):
    return x * lax.rsqrt(jnp.mean(x * x, axis=-1, keepdims=True) + EPS)


def _dot(a, b):
    return jnp.dot(a, b, preferred_element_type=F32)


def _dot_nt(a, b):
    return lax.dot_general(a, b, (((1,), (1,)), ((), ())), preferred_element_type=F32)


def _dot_tn(a, b):
    return lax.dot_general(a, b, (((0,), (0,)), ((), ())), preferred_element_type=F32)


def _ada_kernel(c_ref, w_ref, b_ref, o_ref):
    s = _silu(c_ref[...]).astype(BF16)
    o_ref[...] = _dot(s, w_ref[...].astype(BF16)) + b_ref[...]


def _ada_mod(cond, w_ada, b_ada):
    tn = 1024
    n = N_MOD * D
    out = pl.pallas_call(
        _ada_kernel,
        out_shape=jax.ShapeDtypeStruct((DEPTH, N_GROUPS, n), F32),
        grid=(DEPTH, n // tn),
        in_specs=[
            pl.BlockSpec((N_GROUPS, D), lambda l, j: (0, 0)),
            pl.BlockSpec((None, D, tn), lambda l, j: (l, 0, j)),
            pl.BlockSpec((None, 1, tn), lambda l, j: (l, 0, j)),
        ],
        out_specs=pl.BlockSpec((None, N_GROUPS, tn), lambda l, j: (l, 0, j)),
        compiler_params=_cparams(("parallel", "parallel")),
        name="ada_mod",
    )(cond, w_ada, b_ada.reshape(DEPTH, 1, n))
    return out.reshape(DEPTH, N_GROUPS, N_MOD, 1, D)


def _mod_spec(layer, j, tm, row_off):
    return pl.BlockSpec(
        (None, None, None, 1, D),
        lambda i, n: (layer, _group_of_row((i + row_off) * tm), j, 0, 0))


def _ffn_kernel(x_ref, g_ref, sh_ref, sc_ref, gt_ref, wg_ref, wu_ref, wo_ref, o_ref, h_ref):
    f = pl.program_id(1)

    @pl.when(f == 0)
    def _():
        h = _rms(x_ref[...]) * g_ref[...] * (1.0 + sc_ref[...]) + sh_ref[...]
        h_ref[...] = h.astype(BF16)
        o_ref[...] = jnp.zeros_like(o_ref)

    h = h_ref[...]
    a = _silu(_dot(h, wg_ref[...])) * _dot(h, wu_ref[...])
    o_ref[...] += _dot(a.astype(BF16), wo_ref[...])

    @pl.when(f == pl.num_programs(1) - 1)
    def _():
        o_ref[...] = x_ref[...] + 0.5 * gt_ref[...] * o_ref[...]


def _ffn(x, mod, layer, j, g, w_in, w_out, *, tm=512, tf=512):
    nf = D_FF // tf
    return pl.pallas_call(
        _ffn_kernel,
        out_shape=jax.ShapeDtypeStruct((N_TOK, D), F32),
        grid=(N_TOK // tm, nf),
        in_specs=[
            pl.BlockSpec((tm, D), lambda i, f: (i, 0)),
            pl.BlockSpec((1, D), lambda i, f: (0, 0)),
            _mod_spec(layer, 3 * j, tm, 0),
            _mod_spec(layer, 3 * j + 1, tm, 0),
            _mod_spec(layer, 3 * j + 2, tm, 0),
            pl.BlockSpec((D, tf), lambda i, f: (0, f)),
            pl.BlockSpec((D, tf), lambda i, f: (0, nf + f)),
            pl.BlockSpec((tf, D), lambda i, f: (f, 0)),
        ],
        out_specs=pl.BlockSpec((tm, D), lambda i, f: (i, 0)),
        scratch_shapes=[pltpu.VMEM((tm, D), BF16)],
        compiler_params=_cparams(("parallel", "arbitrary")),
        name="ffn_half",
    )(x, g.reshape(1, D), mod, mod, mod, w_in, w_in, w_out)


def _mm_kernel(*refs, pro, epi):
    it = iter(refs)
    x_ref = next(it)
    if pro == "gla":
        x2_ref, gin_ref = next(it), next(it)
    if pro in ("ada", "rms", "gla"):
        g_ref = next(it)
    if pro == "ada":
        sh_ref, sc_ref = next(it), next(it)
    w_ref = next(it)
    if epi == "resid":
        res_ref, gt_ref = next(it), next(it)
    o_ref = next(it)
    h_ref = next(it)

    @pl.when(pl.program_id(1) == 0)
    def _():
        if pro == "ada":
            h = _rms(x_ref[...]) * g_ref[...] * (1.0 + sc_ref[...]) + sh_ref[...]
        elif pro == "rms":
            h = _rms(x_ref[...].astype(F32)) * g_ref[...]
        elif pro == "cast":
            h = x_ref[...]
        else:
            o = x_ref[...] + x2_ref[...]
            parts = [_rms(o[:, k * GLA_DV:(k + 1) * GLA_DV]) * g_ref[...] for k in range(GLA_HEADS)]
            h = jnp.concatenate(parts, axis=-1) * _silu(gin_ref[...])
        h_ref[...] = h.astype(BF16)

    y = _dot(h_ref[...], w_ref[...])
    if epi == "resid":
        y = res_ref[...] + gt_ref[...] * y
    o_ref[...] = y.astype(o_ref.dtype)


def _mm(x, w, *, rows, row_off=0, xcol=0, pro, epi="plain", out_dtype=F32, tm=512, tn=512,
        g=None, mod=None, layer=None, jmod=None, x2=None, gin=None, gin_col=0, res=None):
    kdim, n = w.shape
    tn = min(tn, n)
    assert rows % tm == 0 and n % tn == 0
    xspec = pl.BlockSpec((tm, kdim), lambda i, j: (i + row_off, xcol))
    args, specs = [x], [xspec]
    if pro == "gla":
        args += [x2, gin]
        specs += [xspec, pl.BlockSpec((tm, kdim), lambda i, j: (i + row_off, gin_col))]
    if pro in ("ada", "rms", "gla"):
        args.append(g.reshape(1, -1))
        specs.append(pl.BlockSpec((1, g.shape[-1]), lambda i, j: (0, 0)))
    if pro == "ada":
        args += [mod, mod]
        specs += [_mod_spec(layer, 3 * jmod, tm, row_off), _mod_spec(layer, 3 * jmod + 1, tm, row_off)]
    args.append(w)
    specs.append(pl.BlockSpec((kdim, tn), lambda i, j: (0, j)))
    if epi == "resid":
        assert tn == D or D % tn == 0
        args += [res, mod]
        specs += [
            pl.BlockSpec((tm, tn), lambda i, j: (i + row_off, j)),
            pl.BlockSpec((None, None, None, 1, tn),
                         lambda i, j: (layer, _group_of_row((i + row_off) * tm), 3 * jmod + 2, 0, j)),
        ]
    return pl.pallas_call(
        functools.partial(_mm_kernel, pro=pro, epi=epi),
        out_shape=jax.ShapeDtypeStruct((rows, n), out_dtype),
        grid=(rows // tm, n // tn),
        in_specs=specs,
        out_specs=pl.BlockSpec((tm, tn), lambda i, j: (i, j)),
        scratch_shapes=[pltpu.VMEM((tm, kdim), BF16)],
        compiler_params=_cparams(("parallel", "arbitrary")),
        name="proj_" + pro + "_" + epi,
    )(*args)


def _rmsnorm_kernel(x_ref, g_ref, o_ref):
    o_ref[...] = _rms(x_ref[...]) * g_ref[...]


def _rmsnorm(x, g, *, rows, row_off=0, xcol=0, tm=512):
    width = g.shape[-1]
    return pl.pallas_call(
        _rmsnorm_kernel,
        out_shape=jax.ShapeDtypeStruct((rows, width), F32),
        grid=(rows // tm,),
        in_specs=[pl.BlockSpec((tm, width), lambda i: (i + row_off, xcol)),
                  pl.BlockSpec((1, width), lambda i: (0, 0))],
        out_specs=pl.BlockSpec((tm, width), lambda i: (i, 0)),
        compiler_params=_cparams(("parallel",)),
        name="rmsnorm",
    )(x, g.reshape(1, width))


def _rope(x, cos, sin):
    width = x.shape[-1]
    lane = lax.broadcasted_iota(jnp.int32, x.shape, 1)
    up = pltpu.roll(x, width - 16, 1)
    down = pltpu.roll(x, 16, 1)
    swapped = jnp.where((lane & 31) < 16, up, down)
    return x * cos + swapped * sin


def _attn_kernel(*refs, nseg, has_r, rope, scale, dh, dv):
    it = iter(refs)
    q_ref = next(it)
    qr_ref = next(it) if has_r else None
    segs = []
    for _ in range(nseg):
        k_ref = next(it)
        kr_ref = next(it) if has_r else None
        v_ref = next(it)
        segs.append((k_ref, kr_ref, v_ref))
    if rope:
        cq_ref, sq_ref, ck_ref, sk_ref = next(it), next(it), next(it), next(it)
    o_ref = next(it)

    if has_r:
        qr = qr_ref[...].astype(F32)
        if rope:
            qr = _rope(qr, cq_ref[...], sq_ref[...])
        qrs = [qr[:, hh * MLA_ROPE:(hh + 1) * MLA_ROPE].astype(BF16) for hh in range(2)]
        krs = []
        for si, (_, kr_ref, _) in enumerate(segs):
            kr = kr_ref[...].astype(F32)
            if rope and si == 0:
                kr = _rope(kr, ck_ref[...], sk_ref[...])
            krs.append(kr[:, :MLA_ROPE].astype(BF16))

    for hh in range(2):
        q = q_ref[:, hh * dh:(hh + 1) * dh].astype(BF16)
        ss = []
        for si, (k_ref, _, _) in enumerate(segs):
            s = _dot_nt(q, k_ref[:, hh * dh:(hh + 1) * dh].astype(BF16))
            if has_r:
                s = s + _dot_nt(qrs[hh], krs[si])
            ss.append(s * scale)
        m = ss[0].max(axis=-1, keepdims=True)
        for s in ss[1:]:
            m = jnp.maximum(m, s.max(axis=-1, keepdims=True))
        acc = None
        den = None
        for s, (_, _, v_ref) in zip(ss, segs):
            e = jnp.exp(s - m)
            d = e.sum(axis=-1, keepdims=True)
            pv = _dot(e.astype(BF16), v_ref[:, hh * dv:(hh + 1) * dv].astype(BF16))
            acc = pv if acc is None else acc + pv
            den = d if den is None else den + d
        o_ref[:, hh * dv:(hh + 1) * dv] = (acc / den).astype(o_ref.dtype)


def _attention(q, segs, *, nb, lq, tq, q_row0, q_col, scale, qr=None, rope_tabs=None, dh=128, dv=128):
    npairs = 8
    nq = lq // tq
    has_r = qr is not None
    args, specs = [q], [pl.BlockSpec((tq, 2 * dh), lambda b, p, t: (q_row0 // tq + b * nq + t, q_col + p))]
    if has_r:
        qr_arr, qr_col = qr
        args.append(qr_arr)
        specs.append(pl.BlockSpec((tq, 2 * MLA_ROPE), lambda b, p, t: (q_row0 // tq + b * nq + t, qr_col + p)))
    for sg in segs:
        lk = sg["lk"]
        k_arr, k_row0, k_col = sg["k"]
        args.append(k_arr)
        specs.append(pl.BlockSpec((lk, 2 * dh), functools.partial(
            lambda b, p, t, r0, c0: (r0 + b, c0 + p), r0=k_row0 // lk, c0=k_col)))
        if has_r:
            kr_arr, kr_row0, kr_col, kr_w = sg["kr"]
            args.append(kr_arr)
            specs.append(pl.BlockSpec((lk, kr_w), functools.partial(
                lambda b, p, t, r0, c0: (r0 + b, c0), r0=kr_row0 // lk, c0=kr_col)))
        v_arr, v_row0, v_col = sg["v"]
        args.append(v_arr)
        specs.append(pl.BlockSpec((lk, 2 * dv), functools.partial(
            lambda b, p, t, r0, c0: (r0 + b, c0 + p), r0=v_row0 // lk, c0=v_col)))
    if rope_tabs is not None:
        cos, sin = rope_tabs
        lk0 = segs[0]["lk"]
        args += [cos, sin, cos, sin]
        specs += [pl.BlockSpec((tq, 128), lambda b, p, t: (t, 0)),
                  pl.BlockSpec((tq, 128), lambda b, p, t: (t, 0)),
                  pl.BlockSpec((lk0, 128), lambda b, p, t: (0, 0)),
                  pl.BlockSpec((lk0, 128), lambda b, p, t: (0, 0))]
    return pl.pallas_call(
        functools.partial(_attn_kernel, nseg=len(segs), has_r=has_r, rope=rope_tabs is not None,
                          scale=scale, dh=dh, dv=dv),
        out_shape=jax.ShapeDtypeStruct((nb * lq, npairs * 2 * dv), BF16),
        grid=(nb, npairs, nq),
        in_specs=specs,
        out_specs=pl.BlockSpec((tq, 2 * dv), lambda b, p, t: (b * nq + t, p)),
        compiler_params=_cparams(("parallel", "parallel", "arbitrary")),
        name="attention",
    )(*args)


def _na_kernel(q_ref, k_ref, v_ref, kc_ref, vc_ref, tb_ref, o_ref):
    r = pl.program_id(2)
    rs = jnp.clip(r - NA_WIN_ROWS // 2, 0, LAT_ROWS - NA_WIN_ROWS)
    start = pl.multiple_of(rs * GRID_W, GRID_W)
    n_loc = NA_WIN_ROWS * GRID_W
    scale = NA_DH ** -0.5
    qc = lax.broadcasted_iota(jnp.int32, (GRID_W, n_loc), 0)
    kc = lax.broadcasted_iota(jnp.int32, (GRID_W, n_loc), 1) & (GRID_W - 1)
    cs = jnp.clip(qc - NA_WIN_COLS // 2, 0, GRID_W - NA_WIN_COLS)
    col_ok = (kc >= cs) & (kc < cs + NA_WIN_COLS)
    kw = k_ref[pl.ds(start, n_loc), :]
    vw = v_ref[pl.ds(start, n_loc), :]
    for hh in range(2):
        sl = slice(hh * NA_DH, (hh + 1) * NA_DH)
        q = q_ref[:, sl]
        s_loc = _dot_nt(q, kw[:, sl]) * scale
        s_loc = jnp.where(col_ok, s_loc + tb_ref[hh], NEG_INF)
        s_ctx = _dot_nt(q, kc_ref[:, sl]) * scale
        m = jnp.maximum(s_loc.max(axis=-1, keepdims=True), s_ctx.max(axis=-1, keepdims=True))
        e_loc = jnp.exp(s_loc - m)
        e_ctx = jnp.exp(s_ctx - m)
        den = e_loc.sum(axis=-1, keepdims=True) + e_ctx.sum(axis=-1, keepdims=True)
        acc = _dot(e_loc.astype(BF16), vw[:, sl]) + _dot(e_ctx.astype(BF16), vc_ref[:, sl])
        o_ref[:, sl] = (acc / den).astype(o_ref.dtype)


def _na_bias_table(rpb):
    col = jnp.arange(GRID_W)
    col_off = jnp.clip(col[None, :] - col[:, None] + NA_WIN_COLS - 1, 0, 2 * NA_WIN_COLS - 2)
    dr = jnp.arange(NA_WIN_ROWS)[:, None] + jnp.arange(NA_WIN_ROWS)[None, :]
    tb = rpb[:, dr][:, :, :, col_off]
    return jnp.transpose(tb, (0, 1, 3, 2, 4)).reshape(NA_HEADS, NA_WIN_ROWS, GRID_W, NA_WIN_ROWS * GRID_W)


def _na_latent(qkv_l, kc, vc, tb):
    npairs = NA_HEADS // 2
    w2 = 2 * NA_DH

    def variant(r):
        return jnp.clip(r - NA_WIN_ROWS // 2, 0, LAT_ROWS - NA_WIN_ROWS) - r + NA_WIN_ROWS - 1

    return pl.pallas_call(
        _na_kernel,
        out_shape=jax.ShapeDtypeStruct((N_LAT, D), BF16),
        grid=(DEC_BATCH, npairs, LAT_ROWS),
        in_specs=[
            pl.BlockSpec((GRID_W, w2), lambda b, p, r: (b * LAT_ROWS + r, p)),
            pl.BlockSpec((DEC_SEQ, w2), lambda b, p, r: (b, npairs + p)),
            pl.BlockSpec((DEC_SEQ, w2), lambda b, p, r: (b, 2 * npairs + p)),
            pl.BlockSpec((PAST, w2), lambda b, p, r: (b, p)),
            pl.BlockSpec((PAST, w2), lambda b, p, r: (b, p)),
            pl.BlockSpec((2, None, GRID_W, NA_WIN_ROWS * GRID_W), lambda b, p, r: (p, variant(r), 0, 0)),
        ],
        out_specs=pl.BlockSpec((GRID_W, w2), lambda b, p, r: (b * LAT_ROWS + r, p)),
        compiler_params=_cparams(("parallel", "parallel", "arbitrary")),
        name="na_latent",
    )(qkv_l, qkv_l, qkv_l, kc, vc, tb)


def _gla_kernel(*refs, has_s0, has_fin):
    it = iter(refs)
    q_ref, k_ref, v_ref, gd_ref, wgu_ref, bg_ref = (next(it) for _ in range(6))
    s0_ref = next(it) if has_s0 else None
    o_ref = next(it)
    fin_ref = next(it) if has_fin else None
    st_ref = next(it)
    dirn = pl.program_id(2)
    n = pl.program_id(3)

    @pl.when(n == 0)
    def _():
        if has_s0:
            st_ref[...] = s0_ref[...]
        else:
            st_ref[...] = jnp.zeros_like(st_ref)

    pre = _dot(gd_ref[...].astype(BF16), wgu_ref[...]) + bg_ref[...]
    la = (jnp.minimum(pre, 0.0) - jnp.log1p(jnp.exp(-jnp.abs(pre)))) / GLA_TAU
    ri = lax.broadcasted_iota(jnp.int32, (GLA_CHUNK, GLA_CHUNK), 0)
    ci = lax.broadcasted_iota(jnp.int32, (GLA_CHUNK, GLA_CHUNK), 1)
    keep = (ci - ri) * (1 - 2 * dirn) <= 0
    tri = jnp.where(keep, 1.0, 0.0).astype(BF16)
    hi = la.astype(BF16)
    r1 = la - hi.astype(F32)
    mid = r1.astype(BF16)
    lo = (r1 - mid.astype(F32)).astype(BF16)
    cum = _dot(tri, hi) + _dot(tri, mid) + _dot(tri, lo)
    tot = jnp.sum(la, axis=0, keepdims=True)
    q = q_ref[...]
    k = k_ref[...]
    v = v_ref[...].astype(BF16)
    q_dec = (q * (GLA_DK ** -0.5) * jnp.exp(cum)).astype(BF16)
    k_inv = (k * jnp.exp(-cum)).astype(BF16)
    k_end = (k * jnp.exp(tot - cum)).astype(BF16)
    att = jnp.where(keep, _dot_nt(q_dec, k_inv), 0.0)
    st = st_ref[...]
    o_ref[...] = _dot(att.astype(BF16), v) + _dot_nt(q_dec, st.astype(BF16))
    st_ref[...] = st * jnp.exp(tot) + _dot_tn(v, k_end)

    if has_fin:
        @pl.when(n == pl.num_programs(3) - 1)
        def _():
            fin_ref[...] = st_ref[...]


def _gla_scan(proj, row0, nb, length, wgu, bg, s0t, want_final):
    nchunk = length // GLA_CHUNK
    c0 = row0 // GLA_CHUNK

    def chunk(b, d, n):
        return c0 + b * nchunk + n + d * (nchunk - 1 - 2 * n)

    def ochunk(b, d, n):
        return b * nchunk + n + d * (nchunk - 1 - 2 * n)

    kq = GLA_DK
    specs = [
        pl.BlockSpec((GLA_CHUNK, kq), lambda b, h, d, n: (chunk(b, d, n), h)),
        pl.BlockSpec((GLA_CHUNK, kq), lambda b, h, d, n: (chunk(b, d, n), GLA_HEADS + h)),
        pl.BlockSpec((GLA_CHUNK, GLA_DV), lambda b, h, d, n: (chunk(b, d, n), GLA_HEADS + h)),
        pl.BlockSpec((GLA_CHUNK, 128), lambda b, h, d, n: (chunk(b, d, n), (GLA_PROJ_N - 128) // 128)),
        pl.BlockSpec((None, 128, kq), lambda b, h, d, n: (d, 0, h)),
        pl.BlockSpec((None, 1, kq), lambda b, h, d, n: (d, 0, h)),
    ]
    args = [proj, proj, proj, proj, wgu, bg]
    st_spec = pl.BlockSpec((None, None, None, GLA_DV, GLA_DK), lambda b, h, d, n: (b, d, h, 0, 0))
    if s0t is not None:
        specs.append(st_spec)
        args.append(s0t)
    o_shape = jax.ShapeDtypeStruct((2, nb * length, GLA_HEADS * GLA_DV), F32)
    o_spec = pl.BlockSpec((None, GLA_CHUNK, GLA_DV), lambda b, h, d, n: (d, ochunk(b, d, n), h))
    if want_final:
        out_shape = (o_shape, jax.ShapeDtypeStruct((nb, 2, GLA_HEADS, GLA_DV, GLA_DK), F32))
        out_specs = (o_spec, st_spec)
    else:
        out_shape, out_specs = o_shape, o_spec
    return pl.pallas_call(
        functools.partial(_gla_kernel, has_s0=s0t is not None, has_fin=want_final),
        out_shape=out_shape,
        grid=(nb, GLA_HEADS, 2, nchunk),
        in_specs=specs,
        out_specs=out_specs,
        scratch_shapes=[pltpu.VMEM((GLA_DV, GLA_DK), F32)],
        compiler_params=_cparams(("parallel", "parallel", "parallel", "arbitrary")),
        name="gla_scan",
    )(*args)


def _rope_tables():
    t = jnp.arange(DEC_SEQ)
    d = MLA_ROPE // 2
    inv = ROPE_THETA ** (-jnp.arange(0, d, 2, dtype=F32) / d)
    ang_r = (t // GRID_W).astype(F32)[:, None] * inv[None]
    ang_c = (t % GRID_W).astype(F32)[:, None] * inv[None]
    cos = jnp.concatenate([jnp.cos(ang_r)] * 2 + [jnp.cos(ang_c)] * 2, axis=-1)
    sin = jnp.concatenate([-jnp.sin(ang_r), jnp.sin(ang_r), -jnp.sin(ang_c), jnp.sin(ang_c)], axis=-1)
    return jnp.concatenate([cos, cos], axis=-1), jnp.concatenate([sin, sin], axis=-1)


def _mixer_na(x, mod, layer, g, w_qkv, w_o, rpb, cache_k, cache_v):
    w_qkv = w_qkv.astype(BF16)
    qkv_c = _mm(x, w_qkv, rows=N_CTX, pro="ada", g=g, mod=mod, layer=layer, jmod=1, out_dtype=F32)
    qkv_l = _mm(x, w_qkv, rows=N_LAT, row_off=N_CTX // 512, pro="ada", g=g, mod=mod, layer=layer, jmod=1,
                out_dtype=BF16)
    npairs = NA_HEADS // 2
    o_c = _attention(qkv_c, [dict(k=(qkv_c, 0, npairs), v=(qkv_c, 0, 2 * npairs), lk=SEQ)],
                     nb=BATCH, lq=SEQ, tq=SEQ, q_row0=0, q_col=0, scale=NA_DH ** -0.5)
    kc = cache_k.reshape(DEC_BATCH * PAST, D).astype(BF16)
    vc = cache_v.reshape(DEC_BATCH * PAST, D).astype(BF16)
    o_l = _na_latent(qkv_l, kc, vc, _na_bias_table(rpb))
    o = jnp.concatenate([o_c, o_l], axis=0)
    x = _mm(o, w_o.astype(BF16), rows=N_TOK, pro="cast", epi="resid", res=x, mod=mod, layer=layer, jmod=1)
    k_c = qkv_c[:, D:2 * D].reshape(BATCH, SEQ, NA_HEADS, NA_DH)
    v_c = qkv_c[:, 2 * D:].reshape(BATCH, SEQ, NA_HEADS, NA_DH)
    return x, k_c, v_c


def _mixer_mla(x, mod, layer, g, w_down, g_q, w_uq, g_kv, w_ukv, w_o, cache_ckv, cache_kr):
    hq = MLA_NOPE + MLA_ROPE
    wd = jnp.concatenate([
        w_down[:, :MLA_Q_LORA],
        w_down[:, MLA_Q_LORA + MLA_KV_LORA:],
        jnp.zeros((D, 256 - MLA_ROPE), F32),
        w_down[:, MLA_Q_LORA:MLA_Q_LORA + MLA_KV_LORA]], axis=1).astype(BF16)
    wq = w_uq.reshape(MLA_Q_LORA, MLA_HEADS, hq)
    wq = jnp.concatenate([wq[:, :, :MLA_NOPE].reshape(MLA_Q_LORA, -1),
                          wq[:, :, MLA_NOPE:].reshape(MLA_Q_LORA, -1)], axis=1).astype(BF16)
    wkv = w_ukv.reshape(MLA_KV_LORA, MLA_HEADS, MLA_NOPE + MLA_V)
    wkv = jnp.concatenate([wkv[:, :, :MLA_NOPE].reshape(MLA_KV_LORA, -1),
                           wkv[:, :, MLA_NOPE:].reshape(MLA_KV_LORA, -1)], axis=1).astype(BF16)

    down = _mm(x, wd, rows=N_TOK, pro="ada", g=g, mod=mod, layer=layer, jmod=1, out_dtype=F32, tn=MLA_DOWN_N)
    q = _mm(down, wq, rows=N_TOK, pro="rms", g=g_q, out_dtype=F32)
    kv = _mm(down, wkv, rows=N_TOK, xcol=2, pro="rms", g=g_kv, out_dtype=BF16)
    kv_cache = _mm(cache_ckv.reshape(DEC_BATCH * PAST, MLA_KV_LORA).astype(BF16), wkv,
                   rows=DEC_BATCH * PAST, pro="cast", out_dtype=BF16)
    ckv_c = _rmsnorm(down, g_kv, rows=N_CTX, xcol=2)
    kr_c = down[:N_CTX, MLA_Q_LORA:MLA_Q_LORA + MLA_ROPE]

    npairs = MLA_HEADS // 2
    kr_col = MLA_Q_LORA // 128
    o_c = _attention(q, [dict(k=(kv, 0, 0), v=(kv, 0, npairs), kr=(down, 0, kr_col, 128), lk=SEQ)],
                     nb=BATCH, lq=SEQ, tq=SEQ, q_row0=0, q_col=0, qr=(q, 2 * npairs), scale=MLA_SCALE)
    o_l = _attention(
        q,
        [dict(k=(kv, N_CTX, 0), v=(kv, N_CTX, npairs), kr=(down, N_CTX, kr_col, 128), lk=DEC_SEQ),
         dict(k=(kv_cache, 0, 0), v=(kv_cache, 0, npairs),
              kr=(cache_kr.reshape(DEC_BATCH * PAST, MLA_ROPE), 0, 0, MLA_ROPE), lk=PAST)],
        nb=DEC_BATCH, lq=DEC_SEQ, tq=256, q_row0=N_CTX, q_col=0, qr=(q, 2 * npairs), scale=MLA_SCALE,
        rope_tabs=_rope_tables())
    o = jnp.concatenate([o_c, o_l], axis=0)
    x = _mm(o, w_o.astype(BF16), rows=N_TOK, pro="cast", epi="resid", res=x, mod=mod, layer=layer, jmod=1)
    return x, ckv_c.reshape(BATCH, SEQ, MLA_KV_LORA), kr_c.reshape(BATCH, SEQ, MLA_ROPE)


def _mixer_gla(x, mod, layer, g, w_in, w_gd, w_gu, b_g, g_norm, w_o, state):
    w_cat = jnp.concatenate([w_in, w_gd[0], w_gd[1], jnp.zeros((D, 128 - 2 * GLA_RANK), F32)], axis=1).astype(BF16)
    proj = _mm(x, w_cat, rows=N_TOK, pro="ada", g=g, mod=mod, layer=layer, jmod=1, out_dtype=F32, tn=896)
    wgu = jnp.zeros((2, 128, GLA_HEADS * GLA_DK), F32)
    wgu = wgu.at[0, :GLA_RANK].set(w_gu[0]).at[1, GLA_RANK:2 * GLA_RANK].set(w_gu[1]).astype(BF16)
    bg = b_g.reshape(2, 1, GLA_HEADS * GLA_DK)
    o_c, st_c = _gla_scan(proj, 0, BATCH, SEQ, wgu, bg, None, True)
    o_l = _gla_scan(proj, N_CTX, DEC_BATCH, DEC_SEQ, wgu, bg, jnp.swapaxes(state, -1, -2), False)
    o = jnp.concatenate([o_c, o_l], axis=1)
    x = _mm(o[0], w_o.astype(BF16), rows=N_TOK, pro="gla", x2=o[1], gin=proj, gin_col=2, g=g_norm,
            epi="resid", res=x, mod=mod, layer=layer, jmod=1)
    return x, jnp.swapaxes(st_c, -1, -2)


def kernel(x_prompt, x_sample, cache_na_k, cache_na_v, cache_mla_ckv, cache_mla_krope, state_gla, c, c_ctx, norm_g, w_ada, b_ada, w_ffn_in, w_ffn_out, w_na_qkv, w_na_o, na_rpb, w_mla_down, g_mla_q, w_mla_uq, g_mla_kv, w_mla_ukv, w_mla_o, w_gla_in, w_gla_gate_down, w_gla_gate_up, b_gla_gate, g_gla_norm, w_gla_o, final_norm_g):
    x = jnp.concatenate([x_prompt.reshape(N_CTX, D), x_sample.reshape(N_LAT, D)], axis=0)
    cond = jnp.concatenate([c_ctx[None], c, jnp.zeros((N_GROUPS - 1 - DEC_BATCH, D), F32)], axis=0)
    mod = _ada_mod(cond, w_ada, b_ada)
    w_in = w_ffn_in.astype(BF16)
    w_out = w_ffn_out.astype(BF16)

    na_k, na_v, ckv, krope, gla_st = [], [], [], [], []
    for i in range(DEPTH):
        kind, slot = i % 3, i // 3
        x = _ffn(x, mod, i, 0, norm_g[i, 0], w_in[i, 0], w_out[i, 0])
        if kind == 0:
            x, k_c, v_c = _mixer_na(x, mod, i, norm_g[i, 1], w_na_qkv[slot], w_na_o[slot], na_rpb[slot],
                                    cache_na_k[:, slot], cache_na_v[:, slot])
            na_k.append(k_c)
            na_v.append(v_c)
        elif kind == 1:
            x, ckv_c, kr_c = _mixer_mla(x, mod, i, norm_g[i, 1], w_mla_down[slot], g_mla_q[slot], w_mla_uq[slot],
                                        g_mla_kv[slot], w_mla_ukv[slot], w_mla_o[slot],
                                        cache_mla_ckv[:, slot], cache_mla_krope[:, slot])
            ckv.append(ckv_c)
            krope.append(kr_c)
        else:
            x, st = _mixer_gla(x, mod, i, norm_g[i, 1], w_gla_in[slot], w_gla_gate_down[slot],
                               w_gla_gate_up[slot], b_gla_gate[slot], g_gla_norm[slot], w_gla_o[slot],
                               state_gla[:, slot])
            gla_st.append(st)
        x = _ffn(x, mod, i, 2, norm_g[i, 2], w_in[i, 1], w_out[i, 1])

    y_prompt = _rmsnorm(x, final_norm_g, rows=N_CTX).reshape(BATCH, SEQ, D)
    y_sample = _rmsnorm(x, final_norm_g, rows=N_LAT, row_off=N_CTX // 512).reshape(DEC_BATCH, DEC_SEQ, D)
    return (y_prompt, y_sample, jnp.stack(na_k, axis=1), jnp.stack(na_v, axis=1), jnp.stack(ckv, axis=1),
            jnp.stack(krope, axis=1), jnp.stack(gla_st, axis=1))
```

```python
import functools

import jax
import jax.numpy as jnp
from jax import lax
from jax.experimental import pallas as pl
from jax.experimental.pallas import tpu as pltpu

F32 = jnp.float32
BF16 = jnp.bfloat16

D = 2048
BATCH, SEQ = 16, 256
DEC_BATCH, DEC_SEQ = 4, 2048
PAST = 512
DEPTH = 4
N_MOD = 9
EPS = 1e-6
D_FF = 5632
GRID_W = 64
LOG2_GRID_W = 6
NEG_INF = -1e30

N_CTX = BATCH * SEQ
N_LAT = DEC_BATCH * DEC_SEQ
N_TOK = N_CTX + N_LAT
N_GROUPS = 8

NA_HEADS, NA_DH = 16, 128
NA_WIN_ROWS, NA_WIN_COLS = 8, 16
LAT_ROWS = DEC_SEQ // GRID_W
NA_QR = 8
NA_KR = NA_QR + NA_WIN_ROWS

MLA_HEADS = 16
MLA_Q_LORA, MLA_KV_LORA = 768, 512
MLA_NOPE, MLA_ROPE, MLA_V = 128, 64, 128
MLA_SCALE = (MLA_NOPE + MLA_ROPE) ** -0.5
ROPE_THETA = 10000.0
MLA_DOWN_N = 1536

GLA_HEADS, GLA_DK, GLA_DV = 4, 256, 512
GLA_RANK = 16
GLA_TAU = 16.0
GLA_CHUNK = 64
GLA_PROJ_N = 2 * GLA_HEADS * GLA_DK + 2 * GLA_HEADS * GLA_DV + 128
GLA_CTX_CHUNKS = SEQ // GLA_CHUNK
GLA_LAT_CHUNKS = DEC_SEQ // GLA_CHUNK
GLA_CTX_STEPS = BATCH * GLA_CTX_CHUNKS
GLA_STEPS = GLA_CTX_STEPS + DEC_BATCH * GLA_LAT_CHUNKS

VMEM_LIMIT = 56 * 1024 * 1024
ROW_CHUNK = 16


def _cparams(sem):
    return pltpu.CompilerParams(dimension_semantics=sem, vmem_limit_bytes=VMEM_LIMIT)


def _group_of_row(r0):
    return jnp.where(r0 < N_CTX, 0, 1 + (r0 - N_CTX) // DEC_SEQ)


def _silu(x):
    return x / (1.0 + jnp.exp(-x))


def _rms(x):
    return x * lax.rsqrt(jnp.mean(x * x, axis=-1, keepdims=True) + EPS)


def _for_row_chunks(n_rows, body):
    def step(c, carry):
        body(pl.ds(pl.multiple_of(c * ROW_CHUNK, ROW_CHUNK), ROW_CHUNK))
        return carry
    lax.fori_loop(0, n_rows // ROW_CHUNK, step, 0, unroll=2)


def _dot(a, b):
    return jnp.dot(a, b, preferred_element_type=F32)


def _dot_nt(a, b):
    return lax.dot_general(a, b, (((1,), (1,)), ((), ())), preferred_element_type=F32)


def _dot_tn(a, b):
    return lax.dot_general(a, b, (((0,), (0,)), ((), ())), preferred_element_type=F32)


def _ada_kernel(c_ref, w_ref, b_ref, o_ref):
    s = _silu(c_ref[...]).astype(BF16)
    o_ref[...] = _dot(s, w_ref[...].astype(BF16)) + b_ref[...]


def _ada_mod(cond, w_ada, b_ada):
    tn = 1024
    n = N_MOD * D
    out = pl.pallas_call(
        _ada_kernel,
        out_shape=jax.ShapeDtypeStruct((DEPTH, N_GROUPS, n), F32),
        grid=(DEPTH, n // tn),
        in_specs=[
            pl.BlockSpec((N_GROUPS, D), lambda l, j: (0, 0)),
            pl.BlockSpec((None, D, tn), lambda l, j: (l, 0, j)),
            pl.BlockSpec((None, 1, tn), lambda l, j: (l, 0, j)),
        ],
        out_specs=pl.BlockSpec((None, N_GROUPS, tn), lambda l, j: (l, 0, j)),
        compiler_params=_cparams(("parallel", "parallel")),
        name="ada_mod",
    )(cond, w_ada, b_ada.reshape(DEPTH, 1, n))
    return out.reshape(DEPTH, N_GROUPS, N_MOD, 1, D)


def _mod_spec(layer, j, tm, row_off):
    return pl.BlockSpec(
        (None, None, None, 1, D),
        lambda i, n: (layer, _group_of_row((i + row_off) * tm), j, 0, 0))


def _ffn_kernel(x_ref, g_ref, sh_ref, sc_ref, gt_ref, wg_ref, wu_ref, wo_ref, o_ref, h_ref, gm_ref):
    f = pl.program_id(1)

    @pl.when(f == 0)
    def _():
        gm_ref[...] = g_ref[...] * (1.0 + sc_ref[...])

        def rows_fn(rows):
            h_ref[rows, :] = (_rms(x_ref[rows, :]) * gm_ref[...] + sh_ref[...]).astype(BF16)
            o_ref[rows, :] = jnp.zeros((ROW_CHUNK, D), F32)

        _for_row_chunks(x_ref.shape[0], rows_fn)

    h = h_ref[...]
    a = _silu(_dot(h, wg_ref[...])) * _dot(h, wu_ref[...])
    o_ref[...] += _dot(a.astype(BF16), wo_ref[...])

    @pl.when(f == pl.num_programs(1) - 1)
    def _():
        o_ref[...] = x_ref[...] + 0.5 * gt_ref[...] * o_ref[...]


def _ffn(x, mod, layer, j, g, w_in, w_out, *, tm=512, tf=512):
    nf = D_FF // tf
    return pl.pallas_call(
        _ffn_kernel,
        out_shape=jax.ShapeDtypeStruct((N_TOK, D), F32),
        grid=(N_TOK // tm, nf),
        in_specs=[
            pl.BlockSpec((tm, D), lambda i, f: (i, 0)),
            pl.BlockSpec((1, D), lambda i, f: (0, 0)),
            _mod_spec(layer, 3 * j, tm, 0),
            _mod_spec(layer, 3 * j + 1, tm, 0),
            _mod_spec(layer, 3 * j + 2, tm, 0),
            pl.BlockSpec((D, tf), lambda i, f: (0, f)),
            pl.BlockSpec((D, tf), lambda i, f: (0, nf + f)),
            pl.BlockSpec((tf, D), lambda i, f: (f, 0)),
        ],
        out_specs=pl.BlockSpec((tm, D), lambda i, f: (i, 0)),
        scratch_shapes=[pltpu.VMEM((tm, D), BF16), pltpu.VMEM((1, D), F32)],
        compiler_params=_cparams(("parallel", "arbitrary")),
        name="ffn_half",
    )(x, g.reshape(1, D), mod, mod, mod, w_in, w_in, w_out)


def _mm_kernel(*refs, pro, epi):
    it = iter(refs)
    x_ref = next(it)
    if pro == "gla":
        x2_ref, gin_ref = next(it), next(it)
    if pro in ("ada", "rms", "gla"):
        g_ref = next(it)
    if pro == "ada":
        sh_ref, sc_ref = next(it), next(it)
    w_ref = next(it)
    if epi == "resid":
        res_ref, gt_ref = next(it), next(it)
    o_ref = next(it)
    if pro != "cast":
        h_ref = next(it)
    if pro == "ada":
        gm_ref = next(it)

    if pro != "cast":
        @pl.when(pl.program_id(1) == 0)
        def _():
            if pro == "ada":
                gm_ref[...] = g_ref[...] * (1.0 + sc_ref[...])

            def rows_fn(rows):
                if pro == "ada":
                    h = _rms(x_ref[rows, :]) * gm_ref[...] + sh_ref[...]
                elif pro == "rms":
                    h = _rms(x_ref[rows, :]) * g_ref[...]
                else:
                    o = x_ref[rows, :] + x2_ref[rows, :]
                    parts = [_rms(o[:, k * GLA_DV:(k + 1) * GLA_DV]) * g_ref[...] for k in range(GLA_HEADS)]
                    h = jnp.concatenate(parts, axis=-1) * _silu(gin_ref[rows, :])
                h_ref[rows, :] = h.astype(BF16)

            _for_row_chunks(x_ref.shape[0], rows_fn)

    y = _dot(x_ref[...] if pro == "cast" else h_ref[...], w_ref[...])
    if epi == "resid":
        y = res_ref[...] + gt_ref[...] * y
    o_ref[...] = y.astype(o_ref.dtype)


def _mm(x, w, *, rows, row_off=0, xcol=0, pro, epi="plain", out_dtype=F32, tm=1024, tn=1024,
        g=None, mod=None, layer=None, jmod=None, x2=None, gin=None, gin_col=0, res=None):
    kdim, n = w.shape
    tn = min(tn, n)
    assert rows % tm == 0 and n % tn == 0
    xspec = pl.BlockSpec((tm, kdim), lambda i, j: (i + row_off, xcol))
    args, specs = [x], [xspec]
    if pro == "gla":
        args += [x2, gin]
        specs += [xspec, pl.BlockSpec((tm, kdim), lambda i, j: (i + row_off, gin_col))]
    if pro in ("ada", "rms", "gla"):
        args.append(g.reshape(1, -1))
        specs.append(pl.BlockSpec((1, g.shape[-1]), lambda i, j: (0, 0)))
    if pro == "ada":
        args += [mod, mod]
        specs += [_mod_spec(layer, 3 * jmod, tm, row_off), _mod_spec(layer, 3 * jmod + 1, tm, row_off)]
    args.append(w)
    specs.append(pl.BlockSpec((kdim, tn), lambda i, j: (0, j)))
    if epi == "resid":
        args += [res, mod]
        specs += [
            pl.BlockSpec((tm, tn), lambda i, j: (i + row_off, j)),
            pl.BlockSpec((None, None, None, 1, tn),
                         lambda i, j: (layer, _group_of_row((i + row_off) * tm), 3 * jmod + 2, 0, j)),
        ]
    scratch = []
    if pro != "cast":
        scratch.append(pltpu.VMEM((tm, kdim), BF16))
    if pro == "ada":
        scratch.append(pltpu.VMEM((1, kdim), F32))
    return pl.pallas_call(
        functools.partial(_mm_kernel, pro=pro, epi=epi),
        out_shape=jax.ShapeDtypeStruct((rows, n), out_dtype),
        grid=(rows // tm, n // tn),
        in_specs=specs,
        out_specs=pl.BlockSpec((tm, tn), lambda i, j: (i, j)),
        scratch_shapes=scratch,
        compiler_params=_cparams(("parallel", "arbitrary")),
        name="proj_" + pro + "_" + epi,
    )(*args)


def _rmsnorm_kernel(x_ref, g_ref, o_ref):
    o_ref[...] = _rms(x_ref[...]) * g_ref[...]


def _rmsnorm(x, g, *, rows, row_off=0, xcol=0, tm=512):
    width = g.shape[-1]
    return pl.pallas_call(
        _rmsnorm_kernel,
        out_shape=jax.ShapeDtypeStruct((rows, width), F32),
        grid=(rows // tm,),
        in_specs=[pl.BlockSpec((tm, width), lambda i: (i + row_off, xcol)),
                  pl.BlockSpec((1, width), lambda i: (0, 0))],
        out_specs=pl.BlockSpec((tm, width), lambda i: (i, 0)),
        compiler_params=_cparams(("parallel",)),
        name="rmsnorm",
    )(x, g.reshape(1, width))


def _rope(x, cos, sin):
    width = x.shape[-1]
    lane = lax.broadcasted_iota(jnp.int32, x.shape, 1)
    up = pltpu.roll(x, width - 16, 1)
    down = pltpu.roll(x, 16, 1)
    swapped = jnp.where((lane & 31) < 16, up, down)
    return x * cos + swapped * sin


def _attn_kernel(*refs, nseg, has_r, rope, aliased, scale, dh, dv):
    it = iter(refs)
    q_ref = next(it)
    qr_ref = next(it) if has_r else None
    segs = []
    for _ in range(nseg):
        k_ref = next(it)
        kr_ref = next(it) if has_r else None
        v_ref = next(it)
        segs.append((k_ref, kr_ref, v_ref))
    if rope:
        cq_ref, sq_ref, ck_ref, sk_ref = next(it), next(it), next(it), next(it)
    if aliased:
        next(it)
    o_ref = next(it)

    if has_r:
        qr = qr_ref[...].astype(F32)
        if rope:
            qr = _rope(qr, cq_ref[...], sq_ref[...])
        qrs = [qr[:, hh * MLA_ROPE:(hh + 1) * MLA_ROPE].astype(BF16) for hh in range(2)]
        krs = []
        for si, (_, kr_ref, _) in enumerate(segs):
            kr = kr_ref[...].astype(F32)
            if rope and si == 0:
                kr = _rope(kr, ck_ref[...], sk_ref[...])
            krs.append(kr[:, :MLA_ROPE].astype(BF16))

    for hh in range(2):
        q = q_ref[:, hh * dh:(hh + 1) * dh].astype(BF16)
        ss = []
        for si, (k_ref, _, _) in enumerate(segs):
            s = _dot_nt(q, k_ref[:, hh * dh:(hh + 1) * dh].astype(BF16))
            if has_r:
                s = s + _dot_nt(qrs[hh], krs[si])
            ss.append(s * scale)
        m = ss[0].max(axis=-1, keepdims=True)
        for s in ss[1:]:
            m = jnp.maximum(m, s.max(axis=-1, keepdims=True))
        acc = None
        den = None
        for s, (_, _, v_ref) in zip(ss, segs):
            e = jnp.exp(s - m)
            d = e.sum(axis=-1, keepdims=True)
            pv = _dot(e.astype(BF16), v_ref[:, hh * dv:(hh + 1) * dv].astype(BF16))
            acc = pv if acc is None else acc + pv
            den = d if den is None else den + d
        o_ref[:, hh * dv:(hh + 1) * dv] = (acc / den).astype(o_ref.dtype)


def _attention(q, segs, *, nb, lq, tq, q_row0, q_col, scale, qr=None, rope_tabs=None, out_buf=None,
               dh=128, dv=128):
    npairs = 8
    nq = lq // tq
    has_r = qr is not None
    q_blk0 = q_row0 // tq
    args, specs = [q], [pl.BlockSpec((tq, 2 * dh), lambda b, p, t: (q_blk0 + b * nq + t, q_col + p))]
    if has_r:
        qr_arr, qr_col = qr
        args.append(qr_arr)
        specs.append(pl.BlockSpec((tq, 2 * MLA_ROPE), lambda b, p, t: (q_blk0 + b * nq + t, qr_col + p)))
    for sg in segs:
        lk = sg["lk"]
        k_arr, k_row0, k_col = sg["k"]
        args.append(k_arr)
        specs.append(pl.BlockSpec((lk, 2 * dh), functools.partial(
            lambda b, p, t, r0, c0: (r0 + b, c0 + p), r0=k_row0 // lk, c0=k_col)))
        if has_r:
            kr_arr, kr_row0, kr_col, kr_w = sg["kr"]
            args.append(kr_arr)
            specs.append(pl.BlockSpec((lk, kr_w), functools.partial(
                lambda b, p, t, r0, c0: (r0 + b, c0), r0=kr_row0 // lk, c0=kr_col)))
        v_arr, v_row0, v_col = sg["v"]
        args.append(v_arr)
        specs.append(pl.BlockSpec((lk, 2 * dv), functools.partial(
            lambda b, p, t, r0, c0: (r0 + b, c0 + p), r0=v_row0 // lk, c0=v_col)))
    if rope_tabs is not None:
        cos, sin = rope_tabs
        lk0 = segs[0]["lk"]
        args += [cos, sin, cos, sin]
        specs += [pl.BlockSpec((tq, 128), lambda b, p, t: (t, 0)),
                  pl.BlockSpec((tq, 128), lambda b, p, t: (t, 0)),
                  pl.BlockSpec((lk0, 128), lambda b, p, t: (0, 0)),
                  pl.BlockSpec((lk0, 128), lambda b, p, t: (0, 0))]
    aliases = {}
    if out_buf is not None:
        aliases = {len(args): 0}
        args.append(out_buf)
        specs.append(pl.BlockSpec(memory_space=pl.ANY))
    return pl.pallas_call(
        functools.partial(_attn_kernel, nseg=len(segs), has_r=has_r, rope=rope_tabs is not None,
                          aliased=out_buf is not None, scale=scale, dh=dh, dv=dv),
        out_shape=jax.ShapeDtypeStruct((N_TOK, npairs * 2 * dv), BF16),
        grid=(nb, npairs, nq),
        in_specs=specs,
        out_specs=pl.BlockSpec((tq, 2 * dv), lambda b, p, t: (q_blk0 + b * nq + t, p)),
        input_output_aliases=aliases,
        compiler_params=_cparams(("parallel", "parallel", "arbitrary")),
        name="attention",
    )(*args)


def _na_kernel(q_ref, k_ref, v_ref, kc_ref, vc_ref, t2_ref, buf_ref, o_ref, bias_ref, mask_ref):
    del buf_ref
    r0 = pl.program_id(1) * NA_QR
    ks = jnp.clip(r0 - NA_WIN_ROWS // 2, 0, LAT_ROWS - NA_KR)
    nq, nk = NA_QR * GRID_W, NA_KR * GRID_W
    scale = NA_DH ** -0.5

    @pl.when(pl.program_id(2) == 0)
    def _():
        row = lax.broadcasted_iota(jnp.int32, (nq, nk), 0)
        lane = lax.broadcasted_iota(jnp.int32, (nq, nk), 1)
        qc = row & (GRID_W - 1)
        kc = lane & (GRID_W - 1)
        rs = jnp.clip(r0 + (row >> LOG2_GRID_W) - NA_WIN_ROWS // 2, 0, LAT_ROWS - NA_WIN_ROWS)
        kr = ks + (lane >> LOG2_GRID_W)
        cs = jnp.clip(qc - NA_WIN_COLS // 2, 0, GRID_W - NA_WIN_COLS)
        ok = (kr >= rs) & (kr < rs + NA_WIN_ROWS) & (kc >= cs) & (kc < cs + NA_WIN_COLS)
        mask_ref[...] = jnp.where(ok, 1.0, 0.0)
        for hh in range(2):
            for i in range(NA_QR):
                for jp in range(NA_KR // 2):
                    e = jnp.clip(ks - r0 + 2 * jp - i + NA_WIN_ROWS, 0, 2 * NA_WIN_ROWS - 1)
                    bias_ref[hh, i * GRID_W:(i + 1) * GRID_W, jp * 128:(jp + 1) * 128] = t2_ref[hh, e]

    start = pl.multiple_of(ks * GRID_W, GRID_W)
    kw = k_ref[pl.ds(start, nk), :]
    vw = v_ref[pl.ds(start, nk), :]
    ok = mask_ref[...] > 0.5
    for hh in range(2):
        sl = slice(hh * NA_DH, (hh + 1) * NA_DH)
        q = q_ref[:, sl]
        s_loc = jnp.where(ok, _dot_nt(q, kw[:, sl]) * scale + bias_ref[hh], NEG_INF)
        s_ctx = _dot_nt(q, kc_ref[:, sl]) * scale
        m = jnp.maximum(s_loc.max(axis=-1, keepdims=True), s_ctx.max(axis=-1, keepdims=True))
        e_loc = jnp.exp(s_loc - m)
        e_ctx = jnp.exp(s_ctx - m)
        den = e_loc.sum(axis=-1, keepdims=True) + e_ctx.sum(axis=-1, keepdims=True)
        acc = _dot(e_loc.astype(BF16), vw[:, sl]) + _dot(e_ctx.astype(BF16), vc_ref[:, sl])
        o_ref[:, sl] = (acc / den).astype(o_ref.dtype)


def _na_bias_table(rpb):
    col = jnp.arange(GRID_W)
    col_off = jnp.clip(col[None, :] - col[:, None] + NA_WIN_COLS - 1, 0, 2 * NA_WIN_COLS - 2)
    e = jnp.arange(2 * NA_WIN_ROWS)
    dr = jnp.clip(jnp.stack([e - 1, e], axis=1), 0, 2 * NA_WIN_ROWS - 2)
    tb = rpb[:, dr][:, :, :, col_off]
    return jnp.transpose(tb, (0, 1, 3, 2, 4)).reshape(NA_HEADS, 2 * NA_WIN_ROWS, GRID_W, 2 * GRID_W)


def _na_latent(qkv_l, kc, vc, t2, out_buf):
    npairs = NA_HEADS // 2
    w2 = 2 * NA_DH
    nrb = LAT_ROWS // NA_QR
    nq = NA_QR * GRID_W
    return pl.pallas_call(
        _na_kernel,
        out_shape=jax.ShapeDtypeStruct((N_TOK, D), BF16),
        grid=(npairs, nrb, DEC_BATCH),
        in_specs=[
            pl.BlockSpec((nq, w2), lambda p, r, b: (b * nrb + r, p)),
            pl.BlockSpec((DEC_SEQ, w2), lambda p, r, b: (b, npairs + p)),
            pl.BlockSpec((DEC_SEQ, w2), lambda p, r, b: (b, 2 * npairs + p)),
            pl.BlockSpec((PAST, w2), lambda p, r, b: (b, p)),
            pl.BlockSpec((PAST, w2), lambda p, r, b: (b, p)),
            pl.BlockSpec((2, 2 * NA_WIN_ROWS, GRID_W, 2 * GRID_W), lambda p, r, b: (p, 0, 0, 0)),
            pl.BlockSpec(memory_space=pl.ANY),
        ],
        out_specs=pl.BlockSpec((nq, w2), lambda p, r, b: (N_CTX // nq + b * nrb + r, p)),
        scratch_shapes=[pltpu.VMEM((2, nq, NA_KR * GRID_W), F32), pltpu.VMEM((nq, NA_KR * GRID_W), F32)],
        input_output_aliases={6: 0},
        compiler_params=_cparams(("parallel", "parallel", "arbitrary")),
        name="na_latent",
    )(qkv_l, qkv_l, qkv_l, kc, vc, t2, out_buf)


def _gla_pos(t):
    is_ctx = t < GLA_CTX_STEPS
    u = jnp.maximum(t - GLA_CTX_STEPS, 0)
    seq = jnp.where(is_ctx, t // GLA_CTX_CHUNKS, u // GLA_LAT_CHUNKS)
    n = jnp.where(is_ctx, t % GLA_CTX_CHUNKS, u % GLA_LAT_CHUNKS)
    return is_ctx, seq, n


def _gla_bwd_chunk(t):
    is_ctx, _, n = _gla_pos(t)
    return t + jnp.where(is_ctx, GLA_CTX_CHUNKS, GLA_LAT_CHUNKS) - 1 - 2 * n


def _gla_kernel(qf_ref, kf_ref, vf_ref, gdf_ref, qb_ref, kb_ref, vb_ref, gdb_ref, wgu_ref, bg_ref, s0_ref,
                of_ref, ob_ref, fin_ref, st_ref):
    is_ctx, _, n = _gla_pos(pl.program_id(0))
    is_lat = jnp.logical_not(is_ctx)

    @pl.when(jnp.logical_and(n == 0, is_ctx))
    def _():
        st_ref[...] = jnp.zeros_like(st_ref)

    @pl.when(jnp.logical_and(n == 0, is_lat))
    def _():
        st_ref[...] = s0_ref[...]

    ri = lax.broadcasted_iota(jnp.int32, (GLA_CHUNK, GLA_CHUNK), 0)
    ci = lax.broadcasted_iota(jnp.int32, (GLA_CHUNK, GLA_CHUNK), 1)
    streams = ((qf_ref, kf_ref, vf_ref, gdf_ref, of_ref), (qb_ref, kb_ref, vb_ref, gdb_ref, ob_ref))
    for d, (q_ref, k_ref, v_ref, gd_ref, o_ref) in enumerate(streams):
        keep = (ci <= ri) if d == 0 else (ci >= ri)
        tri = jnp.where(keep, 1.0, 0.0).astype(BF16)
        pre = _dot(gd_ref[...].astype(BF16), wgu_ref[d]) + bg_ref[d]
        la = (jnp.minimum(pre, 0.0) - jnp.log1p(jnp.exp(-jnp.abs(pre)))) / GLA_TAU
        hi = la.astype(BF16)
        r1 = la - hi.astype(F32)
        mid = r1.astype(BF16)
        lo = (r1 - mid.astype(F32)).astype(BF16)
        cum = _dot(tri, hi) + _dot(tri, mid) + _dot(tri, lo)
        tot = jnp.sum(la, axis=0, keepdims=True)
        k = k_ref[...]
        q_dec = (q_ref[...] * (GLA_DK ** -0.5) * jnp.exp(cum)).astype(BF16)
        k_inv = (k * jnp.exp(-cum)).astype(BF16)
        k_end = (k * jnp.exp(tot - cum)).astype(BF16)
        dec = jnp.exp(tot)
        v = v_ref[...].astype(BF16)
        for h in range(GLA_HEADS):
            ks = slice(h * GLA_DK, (h + 1) * GLA_DK)
            vs = slice(h * GLA_DV, (h + 1) * GLA_DV)
            att = jnp.where(keep, _dot_nt(q_dec[:, ks], k_inv[:, ks]), 0.0)
            st = st_ref[d, h]
            o_ref[:, vs] = _dot(att.astype(BF16), v[:, vs]) + _dot_nt(q_dec[:, ks], st.astype(BF16))
            st_ref[d, h] = st * dec[:, ks] + _dot_tn(v[:, vs], k_end[:, ks])

    @pl.when(jnp.logical_and(is_ctx, n == GLA_CTX_CHUNKS - 1))
    def _():
        fin_ref[...] = st_ref[...]


def _gla_scan(proj, wgu, bg, s0t):
    hk = GLA_HEADS * GLA_DK
    hv = GLA_HEADS * GLA_DV
    gd_col = (GLA_PROJ_N - 128) // 128

    def fwd(c):
        return lambda t: (t, c)

    def bwd(c):
        return lambda t: (_gla_bwd_chunk(t), c)

    in_specs = []
    for ix in (fwd, bwd):
        in_specs += [pl.BlockSpec((GLA_CHUNK, hk), ix(0)), pl.BlockSpec((GLA_CHUNK, hk), ix(1)),
                     pl.BlockSpec((GLA_CHUNK, hv), ix(1)), pl.BlockSpec((GLA_CHUNK, 128), ix(gd_col))]
    st_block = (None, 2, GLA_HEADS, GLA_DV, GLA_DK)
    in_specs += [
        pl.BlockSpec((2, 128, hk), lambda t: (0, 0, 0)),
        pl.BlockSpec((2, 1, hk), lambda t: (0, 0, 0)),
        pl.BlockSpec(st_block, lambda t: (jnp.where(_gla_pos(t)[0], 0, _gla_pos(t)[1]), 0, 0, 0, 0)),
    ]
    o_shape = jax.ShapeDtypeStruct((N_TOK, hv), F32)
    return pl.pallas_call(
        _gla_kernel,
        out_shape=(o_shape, o_shape, jax.ShapeDtypeStruct((BATCH, 2, GLA_HEADS, GLA_DV, GLA_DK), F32)),
        grid=(GLA_STEPS,),
        in_specs=in_specs,
        out_specs=(
            pl.BlockSpec((GLA_CHUNK, hv), lambda t: (t, 0)),
            pl.BlockSpec((GLA_CHUNK, hv), lambda t: (_gla_bwd_chunk(t), 0)),
            pl.BlockSpec(st_block, lambda t: (jnp.where(_gla_pos(t)[0], _gla_pos(t)[1], BATCH - 1), 0, 0, 0, 0)),
        ),
        scratch_shapes=[pltpu.VMEM((2, GLA_HEADS, GLA_DV, GLA_DK), F32)],
        compiler_params=_cparams(("arbitrary",)),
        name="gla_scan",
    )(*([proj] * 8), wgu, bg, s0t)


def _rope_tables():
    t = jnp.arange(DEC_SEQ)
    d = MLA_ROPE // 2
    inv = ROPE_THETA ** (-jnp.arange(0, d, 2, dtype=F32) / d)
    ang_r = (t // GRID_W).astype(F32)[:, None] * inv[None]
    ang_c = (t % GRID_W).astype(F32)[:, None] * inv[None]
    cos = jnp.concatenate([jnp.cos(ang_r)] * 2 + [jnp.cos(ang_c)] * 2, axis=-1)
    sin = jnp.concatenate([-jnp.sin(ang_r), jnp.sin(ang_r), -jnp.sin(ang_c), jnp.sin(ang_c)], axis=-1)
    return jnp.concatenate([cos, cos], axis=-1), jnp.concatenate([sin, sin], axis=-1)


def _mixer_na(x, mod, layer, g, w_qkv, w_o, rpb, cache_k, cache_v):
    w_qkv = w_qkv.astype(BF16)
    ada = dict(pro="ada", g=g, mod=mod, layer=layer, jmod=1, tn=1536)
    qkv_c = _mm(x, w_qkv, rows=N_CTX, out_dtype=F32, **ada)
    qkv_l = _mm(x, w_qkv, rows=N_LAT, row_off=N_CTX // 1024, out_dtype=BF16, **ada)
    npairs = NA_HEADS // 2
    o = _attention(qkv_c, [dict(k=(qkv_c, 0, npairs), v=(qkv_c, 0, 2 * npairs), lk=SEQ)],
                   nb=BATCH, lq=SEQ, tq=SEQ, q_row0=0, q_col=0, scale=NA_DH ** -0.5)
    kc = cache_k.reshape(DEC_BATCH * PAST, D).astype(BF16)
    vc = cache_v.reshape(DEC_BATCH * PAST, D).astype(BF16)
    o = _na_latent(qkv_l, kc, vc, _na_bias_table(rpb), o)
    x = _mm(o, w_o.astype(BF16), rows=N_TOK, pro="cast", epi="resid", res=x, mod=mod, layer=layer, jmod=1)
    k_c = qkv_c[:, D:2 * D].reshape(BATCH, SEQ, NA_HEADS, NA_DH)
    v_c = qkv_c[:, 2 * D:].reshape(BATCH, SEQ, NA_HEADS, NA_DH)
    return x, k_c, v_c


def _mixer_mla(x, mod, layer, g, w_down, g_q, w_uq, g_kv, w_ukv, w_o, cache_ckv, cache_kr):
    hq = MLA_NOPE + MLA_ROPE
    wd = jnp.concatenate([
        w_down[:, :MLA_Q_LORA],
        w_down[:, MLA_Q_LORA + MLA_KV_LORA:],
        jnp.zeros((D, 256 - MLA_ROPE), F32),
        w_down[:, MLA_Q_LORA:MLA_Q_LORA + MLA_KV_LORA]], axis=1).astype(BF16)
    wq = w_uq.reshape(MLA_Q_LORA, MLA_HEADS, hq)
    wq = jnp.concatenate([wq[:, :, :MLA_NOPE].reshape(MLA_Q_LORA, -1),
                          wq[:, :, MLA_NOPE:].reshape(MLA_Q_LORA, -1)], axis=1).astype(BF16)
    wkv = w_ukv.reshape(MLA_KV_LORA, MLA_HEADS, MLA_NOPE + MLA_V)
    wkv = jnp.concatenate([wkv[:, :, :MLA_NOPE].reshape(MLA_KV_LORA, -1),
                           wkv[:, :, MLA_NOPE:].reshape(MLA_KV_LORA, -1)], axis=1).astype(BF16)

    down = _mm(x, wd, rows=N_TOK, pro="ada", g=g, mod=mod, layer=layer, jmod=1, out_dtype=F32, tn=MLA_DOWN_N)
    q = _mm(down, wq, rows=N_TOK, pro="rms", g=g_q, out_dtype=F32, tn=1536)
    kv = _mm(down, wkv, rows=N_TOK, xcol=2, pro="rms", g=g_kv, out_dtype=BF16, tn=2048)
    kv_cache = _mm(cache_ckv.reshape(DEC_BATCH * PAST, MLA_KV_LORA).astype(BF16), wkv,
                   rows=DEC_BATCH * PAST, pro="cast", out_dtype=BF16, tn=2048)
    ckv_c = _rmsnorm(down, g_kv, rows=N_CTX, xcol=2)
    kr_c = down[:N_CTX, MLA_Q_LORA:MLA_Q_LORA + MLA_ROPE]

    npairs = MLA_HEADS // 2
    kr_col = MLA_Q_LORA // 128
    o = _attention(q, [dict(k=(kv, 0, 0), v=(kv, 0, npairs), kr=(down, 0, kr_col, 128), lk=SEQ)],
                   nb=BATCH, lq=SEQ, tq=SEQ, q_row0=0, q_col=0, qr=(q, 2 * npairs), scale=MLA_SCALE)
    o = _attention(
        q,
        [dict(k=(kv, N_CTX, 0), v=(kv, N_CTX, npairs), kr=(down, N_CTX, kr_col, 128), lk=DEC_SEQ),
         dict(k=(kv_cache, 0, 0), v=(kv_cache, 0, npairs),
              kr=(cache_kr.reshape(DEC_BATCH * PAST, MLA_ROPE), 0, 0, MLA_ROPE), lk=PAST)],
        nb=DEC_BATCH, lq=DEC_SEQ, tq=256, q_row0=N_CTX, q_col=0, qr=(q, 2 * npairs), scale=MLA_SCALE,
        rope_tabs=_rope_tables(), out_buf=o)
    x = _mm(o, w_o.astype(BF16), rows=N_TOK, pro="cast", epi="resid", res=x, mod=mod, layer=layer, jmod=1)
    return x, ckv_c.reshape(BATCH, SEQ, MLA_KV_LORA), kr_c.reshape(BATCH, SEQ, MLA_ROPE)


def _mixer_gla(x, mod, layer, g, w_in, w_gd, w_gu, b_g, g_norm, w_o, state):
    w_cat = jnp.concatenate([w_in, w_gd[0], w_gd[1], jnp.zeros((D, 128 - 2 * GLA_RANK), F32)], axis=1).astype(BF16)
    proj = _mm(x, w_cat, rows=N_TOK, pro="ada", g=g, mod=mod, layer=layer, jmod=1, out_dtype=F32, tn=896)
    wgu = jnp.zeros((2, 128, GLA_HEADS * GLA_DK), F32)
    wgu = wgu.at[0, :GLA_RANK].set(w_gu[0]).at[1, GLA_RANK:2 * GLA_RANK].set(w_gu[1]).astype(BF16)
    bg = b_g.reshape(2, 1, GLA_HEADS * GLA_DK)
    o_f, o_b, st_c = _gla_scan(proj, wgu, bg, jnp.swapaxes(state, -1, -2))
    x = _mm(o_f, w_o.astype(BF16), rows=N_TOK, pro="gla", x2=o_b, gin=proj, gin_col=2, g=g_norm,
            epi="resid", res=x, mod=mod, layer=layer, jmod=1, tm=512)
    return x, jnp.swapaxes(st_c, -1, -2)


def kernel(x_prompt, x_sample, cache_na_k, cache_na_v, cache_mla_ckv, cache_mla_krope, state_gla, c, c_ctx, norm_g, w_ada, b_ada, w_ffn_in, w_ffn_out, w_na_qkv, w_na_o, na_rpb, w_mla_down, g_mla_q, w_mla_uq, g_mla_kv, w_mla_ukv, w_mla_o, w_gla_in, w_gla_gate_down, w_gla_gate_up, b_gla_gate, g_gla_norm, w_gla_o, final_norm_g):
    x = jnp.concatenate([x_prompt.reshape(N_CTX, D), x_sample.reshape(N_LAT, D)], axis=0)
    cond = jnp.concatenate([c_ctx[None], c, jnp.zeros((N_GROUPS - 1 - DEC_BATCH, D), F32)], axis=0)
    mod = _ada_mod(cond, w_ada, b_ada)
    w_in = w_ffn_in.astype(BF16)
    w_out = w_ffn_out.astype(BF16)

    na_k, na_v, ckv, krope, gla_st = [], [], [], [], []
    for i in range(DEPTH):
        kind, slot = i % 3, i // 3
        x = _ffn(x, mod, i, 0, norm_g[i, 0], w_in[i, 0], w_out[i, 0])
        if kind == 0:
            x, k_c, v_c = _mixer_na(x, mod, i, norm_g[i, 1], w_na_qkv[slot], w_na_o[slot], na_rpb[slot],
                                    cache_na_k[:, slot], cache_na_v[:, slot])
            na_k.append(k_c)
            na_v.append(v_c)
        elif kind == 1:
            x, ckv_c, kr_c = _mixer_mla(x, mod, i, norm_g[i, 1], w_mla_down[slot], g_mla_q[slot], w_mla_uq[slot],
                                        g_mla_kv[slot], w_mla_ukv[slot], w_mla_o[slot],
                                        cache_mla_ckv[:, slot], cache_mla_krope[:, slot])
            ckv.append(ckv_c)
            krope.append(kr_c)
        else:
            x, st = _mixer_gla(x, mod, i, norm_g[i, 1], w_gla_in[slot], w_gla_gate_down[slot],
                               w_gla_gate_up[slot], b_gla_gate[slot], g_gla_norm[slot], w_gla_o[slot],
                               state_gla[:, slot])
            gla_st.append(st)
        x = _ffn(x, mod, i, 2, norm_g[i, 2], w_in[i, 1], w_out[i, 1])

    y_prompt = _rmsnorm(x, final_norm_g, rows=N_CTX).reshape(BATCH, SEQ, D)
    y_sample = _rmsnorm(x, final_norm_g, rows=N_LAT, row_off=N_CTX // 512).reshape(DEC_BATCH, DEC_SEQ, D)
    return (y_prompt, y_sample, jnp.stack(na_k, axis=1), jnp.stack(na_v, axis=1), jnp.stack(ckv, axis=1),
            jnp.stack(krope, axis=1), jnp.stack(gla_st, axis=1))
```

```python
import functools

import jax
import jax.numpy as jnp
from jax import lax
from jax.experimental import pallas as pl
from jax.experimental.pallas import tpu as pltpu

F32 = jnp.float32
BF16 = jnp.bfloat16

D = 2048
BATCH, SEQ = 16, 256
DEC_BATCH, DEC_SEQ = 4, 2048
PAST = 512
DEPTH = 4
N_MOD = 9
EPS = 1e-6
D_FF = 5632
GRID_W = 64
LOG2_GRID_W = 6
NEG_INF = -1e30

N_CTX = BATCH * SEQ
N_LAT = DEC_BATCH * DEC_SEQ
N_TOK = N_CTX + N_LAT
N_GROUPS = 8

NA_HEADS, NA_DH = 16, 128
NA_WIN_ROWS, NA_WIN_COLS = 8, 16
LAT_ROWS = DEC_SEQ // GRID_W
NA_QR = 8
NA_KR = NA_QR + NA_WIN_ROWS

MLA_HEADS = 16
MLA_Q_LORA, MLA_KV_LORA = 768, 512
MLA_NOPE, MLA_ROPE, MLA_V = 128, 64, 128
MLA_SCALE = (MLA_NOPE + MLA_ROPE) ** -0.5
ROPE_THETA = 10000.0
MLA_DOWN_N = 1536

GLA_HEADS, GLA_DK, GLA_DV = 4, 256, 512
GLA_RANK = 16
GLA_TAU = 16.0
GLA_CHUNK = 64
GLA_PROJ_N = 2 * GLA_HEADS * GLA_DK + 2 * GLA_HEADS * GLA_DV + 128
GLA_CTX_CHUNKS = SEQ // GLA_CHUNK
GLA_LAT_CHUNKS = DEC_SEQ // GLA_CHUNK
GLA_CTX_STEPS = BATCH * GLA_CTX_CHUNKS
GLA_STEPS = GLA_CTX_STEPS + DEC_BATCH * GLA_LAT_CHUNKS

VMEM_LIMIT = 56 * 1024 * 1024
ROW_CHUNK = 16
FFN_TF = 512
FFN_SUB = 512


def _cparams(sem):
    return pltpu.CompilerParams(dimension_semantics=sem, vmem_limit_bytes=VMEM_LIMIT)


def _group_of_row(r0):
    return jnp.where(r0 < N_CTX, 0, 1 + (r0 - N_CTX) // DEC_SEQ)


def _silu(x):
    return x / (1.0 + jnp.exp(-x))


def _rms(x):
    return x * lax.rsqrt(jnp.mean(x * x, axis=-1, keepdims=True) + EPS)


def _for_row_chunks(n_rows, body):
    def step(c, carry):
        body(pl.ds(pl.multiple_of(c * ROW_CHUNK, ROW_CHUNK), ROW_CHUNK))
        return carry
    lax.fori_loop(0, n_rows // ROW_CHUNK, step, 0, unroll=8)


def _dot(a, b):
    return jnp.dot(a, b, preferred_element_type=F32)


def _dot_nt(a, b):
    return lax.dot_general(a, b, (((1,), (1,)), ((), ())), preferred_element_type=F32)


def _dot_tn(a, b):
    return lax.dot_general(a, b, (((0,), (0,)), ((), ())), preferred_element_type=F32)


def _ada_kernel(c_ref, w_ref, b_ref, o_ref):
    s = _silu(c_ref[...]).astype(BF16)
    o_ref[...] = _dot(s, w_ref[...].astype(BF16)) + b_ref[...]


def _ada_mod(cond, w_ada, b_ada):
    tn = 1024
    n = N_MOD * D
    out = pl.pallas_call(
        _ada_kernel,
        out_shape=jax.ShapeDtypeStruct((DEPTH, N_GROUPS, n), F32),
        grid=(DEPTH, n // tn),
        in_specs=[
            pl.BlockSpec((N_GROUPS, D), lambda l, j: (0, 0)),
            pl.BlockSpec((None, D, tn), lambda l, j: (l, 0, j)),
            pl.BlockSpec((None, 1, tn), lambda l, j: (l, 0, j)),
        ],
        out_specs=pl.BlockSpec((None, N_GROUPS, tn), lambda l, j: (l, 0, j)),
        compiler_params=_cparams(("parallel", "parallel")),
        name="ada_mod",
    )(cond, w_ada, b_ada.reshape(DEPTH, 1, n))
    return out.reshape(DEPTH, N_GROUPS, N_MOD, 1, D)


def _mod_spec(layer, j, tm, row_off):
    return pl.BlockSpec(
        (None, None, None, 1, D),
        lambda i, n: (layer, _group_of_row((i + row_off) * tm), j, 0, 0))


def _ffn_kernel(x_ref, g_ref, sh_ref, sc_ref, gt_ref, wg_ref, wu_ref, wo_ref, o_ref, h_ref, gm_ref):
    f = pl.program_id(1)
    tm = x_ref.shape[0]

    @pl.when(f == 0)
    def _():
        gm_ref[...] = g_ref[...] * (1.0 + sc_ref[...])

        def rows_fn(rows):
            h_ref[rows, :] = (_rms(x_ref[rows, :]) * gm_ref[...] + sh_ref[...]).astype(BF16)
            o_ref[rows, :] = jnp.zeros((ROW_CHUNK, D), F32)

        _for_row_chunks(tm, rows_fn)

    for r in range(0, tm, FFN_SUB):
        h = h_ref[r:r + FFN_SUB, :]
        a = _silu(_dot(h, wg_ref[...])) * _dot(h, wu_ref[...])
        o_ref[r:r + FFN_SUB, :] += _dot(a.astype(BF16), wo_ref[...])

    @pl.when(f == pl.num_programs(1) - 1)
    def _():
        o_ref[...] = x_ref[...] + 0.5 * gt_ref[...] * o_ref[...]


def _ffn(x, mod, layer, half, w_in, w_out, g, *, tm=1024):
    tf = FFN_TF
    nf = D_FF // tf
    j = 2 * half
    return pl.pallas_call(
        _ffn_kernel,
        out_shape=jax.ShapeDtypeStruct((N_TOK, D), F32),
        grid=(N_TOK // tm, nf),
        in_specs=[
            pl.BlockSpec((tm, D), lambda i, f: (i, 0)),
            pl.BlockSpec((1, D), lambda i, f: (0, 0)),
            _mod_spec(layer, 3 * j, tm, 0),
            _mod_spec(layer, 3 * j + 1, tm, 0),
            _mod_spec(layer, 3 * j + 2, tm, 0),
            pl.BlockSpec((None, None, D, tf), lambda i, f: (layer, half, 0, f)),
            pl.BlockSpec((None, None, D, tf), lambda i, f: (layer, half, 0, nf + f)),
            pl.BlockSpec((None, None, tf, D), lambda i, f: (layer, half, f, 0)),
        ],
        out_specs=pl.BlockSpec((tm, D), lambda i, f: (i, 0)),
        scratch_shapes=[pltpu.VMEM((tm, D), BF16), pltpu.VMEM((1, D), F32)],
        compiler_params=_cparams(("parallel", "arbitrary")),
        name="ffn_half",
    )(x, g.reshape(1, D), mod, mod, mod, w_in, w_in, w_out)


def _mm_kernel(*refs, pro, epi):
    it = iter(refs)
    x_ref = next(it)
    if pro == "gla":
        x2_ref, gin_ref = next(it), next(it)
    if pro in ("ada", "rms", "gla"):
        g_ref = next(it)
    if pro == "ada":
        sh_ref, sc_ref = next(it), next(it)
    w_ref = next(it)
    if epi == "resid":
        res_ref, gt_ref = next(it), next(it)
    o_ref = next(it)
    if pro != "cast":
        h_ref = next(it)
    if pro == "ada":
        gm_ref = next(it)

    if pro != "cast":
        @pl.when(pl.program_id(1) == 0)
        def _():
            if pro == "ada":
                gm_ref[...] = g_ref[...] * (1.0 + sc_ref[...])

            def rows_fn(rows):
                if pro == "ada":
                    h = _rms(x_ref[rows, :]) * gm_ref[...] + sh_ref[...]
                elif pro == "rms":
                    h = _rms(x_ref[rows, :]) * g_ref[...]
                else:
                    o = x_ref[rows, :] + x2_ref[rows, :]
                    parts = [_rms(o[:, k * GLA_DV:(k + 1) * GLA_DV]) * g_ref[...] for k in range(GLA_HEADS)]
                    h = jnp.concatenate(parts, axis=-1) * _silu(gin_ref[rows, :])
                h_ref[rows, :] = h.astype(BF16)

            _for_row_chunks(x_ref.shape[0], rows_fn)

    y = _dot(x_ref[...] if pro == "cast" else h_ref[...], w_ref[...])
    if epi == "resid":
        y = res_ref[...] + gt_ref[...] * y
    o_ref[...] = y.astype(o_ref.dtype)


def _mm(x, w, *, rows, row_off=0, xcol=0, pro, epi="plain", out_dtype=F32, tm=1024, tn=1024,
        g=None, mod=None, layer=None, jmod=None, x2=None, gin=None, gin_col=0, res=None):
    kdim, n = w.shape
    tn = min(tn, n)
    assert rows % tm == 0 and n % tn == 0
    xspec = pl.BlockSpec((tm, kdim), lambda i, j: (i + row_off, xcol))
    args, specs = [x], [xspec]
    if pro == "gla":
        args += [x2, gin]
        specs += [xspec, pl.BlockSpec((tm, kdim), lambda i, j: (i + row_off, gin_col))]
    if pro in ("ada", "rms", "gla"):
        args.append(g.reshape(1, -1))
        specs.append(pl.BlockSpec((1, g.shape[-1]), lambda i, j: (0, 0)))
    if pro == "ada":
        args += [mod, mod]
        specs += [_mod_spec(layer, 3 * jmod, tm, row_off), _mod_spec(layer, 3 * jmod + 1, tm, row_off)]
    args.append(w)
    specs.append(pl.BlockSpec((kdim, tn), lambda i, j: (0, j)))
    if epi == "resid":
        args += [res, mod]
        specs += [
            pl.BlockSpec((tm, tn), lambda i, j: (i + row_off, j)),
            pl.BlockSpec((None, None, None, 1, tn),
                         lambda i, j: (layer, _group_of_row((i + row_off) * tm), 3 * jmod + 2, 0, j)),
        ]
    scratch = []
    if pro != "cast":
        scratch.append(pltpu.VMEM((tm, kdim), BF16))
    if pro == "ada":
        scratch.append(pltpu.VMEM((1, kdim), F32))
    return pl.pallas_call(
        functools.partial(_mm_kernel, pro=pro, epi=epi),
        out_shape=jax.ShapeDtypeStruct((rows, n), out_dtype),
        grid=(rows // tm, n // tn),
        in_specs=specs,
        out_specs=pl.BlockSpec((tm, tn), lambda i, j: (i, j)),
        scratch_shapes=scratch,
        compiler_params=_cparams(("parallel", "arbitrary")),
        name="proj_" + pro + "_" + epi,
    )(*args)


def _rmsnorm_kernel(x_ref, g_ref, o_ref):
    o_ref[...] = _rms(x_ref[...]) * g_ref[...]


def _rmsnorm(x, g, *, rows, row_off=0, xcol=0, tm=512):
    width = g.shape[-1]
    return pl.pallas_call(
        _rmsnorm_kernel,
        out_shape=jax.ShapeDtypeStruct((rows, width), F32),
        grid=(rows // tm,),
        in_specs=[pl.BlockSpec((tm, width), lambda i: (i + row_off, xcol)),
                  pl.BlockSpec((1, width), lambda i: (0, 0))],
        out_specs=pl.BlockSpec((tm, width), lambda i: (i, 0)),
        compiler_params=_cparams(("parallel",)),
        name="rmsnorm",
    )(x, g.reshape(1, width))


def _rope(x, cos, sin):
    width = x.shape[-1]
    lane = lax.broadcasted_iota(jnp.int32, x.shape, 1)
    up = pltpu.roll(x, width - 16, 1)
    down = pltpu.roll(x, 16, 1)
    swapped = jnp.where((lane & 31) < 16, up, down)
    return x * cos + swapped * sin


def _rope_keys_kernel(x_ref, cos_ref, sin_ref, o_ref):
    o_ref[...] = _rope(x_ref[...], cos_ref[...], sin_ref[...]).astype(o_ref.dtype)


def _rope_keys(down, rope_tabs, *, tm=1024):
    cos, sin = rope_tabs
    per_seq = DEC_SEQ // tm
    return pl.pallas_call(
        _rope_keys_kernel,
        out_shape=jax.ShapeDtypeStruct((N_LAT, 128), BF16),
        grid=(N_LAT // tm,),
        in_specs=[pl.BlockSpec((tm, 128), lambda i: (N_CTX // tm + i, MLA_Q_LORA // 128)),
                  pl.BlockSpec((tm, 128), lambda i: (i % per_seq, 0)),
                  pl.BlockSpec((tm, 128), lambda i: (i % per_seq, 0))],
        out_specs=pl.BlockSpec((tm, 128), lambda i: (i, 0)),
        compiler_params=_cparams(("parallel",)),
        name="rope_keys",
    )(down, cos, sin)


def _attn_kernel(*refs, nseg, has_r, rope, aliased, scale, dh, dv):
    it = iter(refs)
    q_ref = next(it)
    qr_ref = next(it) if has_r else None
    segs = []
    for _ in range(nseg):
        k_ref = next(it)
        kr_ref = next(it) if has_r else None
        v_ref = next(it)
        segs.append((k_ref, kr_ref, v_ref))
    if rope:
        cq_ref, sq_ref = next(it), next(it)
    if aliased:
        next(it)
    o_ref = next(it)

    if has_r:
        qr = qr_ref[...].astype(F32)
        if rope:
            qr = _rope(qr, cq_ref[...], sq_ref[...])
        qrs = [qr[:, hh * MLA_ROPE:(hh + 1) * MLA_ROPE].astype(BF16) for hh in range(2)]
        krs = [kr_ref[:, :MLA_ROPE].astype(BF16) for _, kr_ref, _ in segs]

    scores = []
    for hh in range(2):
        q = q_ref[:, hh * dh:(hh + 1) * dh].astype(BF16)
        ss = []
        for si, (k_ref, _, _) in enumerate(segs):
            s = _dot_nt(q, k_ref[:, hh * dh:(hh + 1) * dh].astype(BF16))
            if has_r:
                s = s + _dot_nt(qrs[hh], krs[si])
            ss.append(s * scale)
        scores.append(ss)
    for hh in range(2):
        ss = scores[hh]
        m = ss[0].max(axis=-1, keepdims=True)
        for s in ss[1:]:
            m = jnp.maximum(m, s.max(axis=-1, keepdims=True))
        acc = None
        den = None
        for s, (_, _, v_ref) in zip(ss, segs):
            e = jnp.exp(s - m)
            d = e.sum(axis=-1, keepdims=True)
            pv = _dot(e.astype(BF16), v_ref[:, hh * dv:(hh + 1) * dv].astype(BF16))
            acc = pv if acc is None else acc + pv
            den = d if den is None else den + d
        o_ref[:, hh * dv:(hh + 1) * dv] = (acc / den).astype(o_ref.dtype)


def _attention(q, segs, *, nb, lq, tq, q_row0, q_col, scale, qr=None, rope_tabs=None, out_buf=None,
               dh=128, dv=128):
    npairs = 8
    nq = lq // tq
    has_r = qr is not None
    q_blk0 = q_row0 // tq
    args, specs = [q], [pl.BlockSpec((tq, 2 * dh), lambda b, p, t: (q_blk0 + b * nq + t, q_col + p))]
    if has_r:
        qr_arr, qr_col = qr
        args.append(qr_arr)
        specs.append(pl.BlockSpec((tq, 2 * MLA_ROPE), lambda b, p, t: (q_blk0 + b * nq + t, qr_col + p)))
    for sg in segs:
        lk = sg["lk"]
        k_arr, k_row0, k_col = sg["k"]
        args.append(k_arr)
        specs.append(pl.BlockSpec((lk, 2 * dh), functools.partial(
            lambda b, p, t, r0, c0: (r0 + b, c0 + p), r0=k_row0 // lk, c0=k_col)))
        if has_r:
            kr_arr, kr_row0, kr_col, kr_w = sg["kr"]
            args.append(kr_arr)
            specs.append(pl.BlockSpec((lk, kr_w), functools.partial(
                lambda b, p, t, r0, c0: (r0 + b, c0), r0=kr_row0 // lk, c0=kr_col)))
        v_arr, v_row0, v_col = sg["v"]
        args.append(v_arr)
        specs.append(pl.BlockSpec((lk, 2 * dv), functools.partial(
            lambda b, p, t, r0, c0: (r0 + b, c0 + p), r0=v_row0 // lk, c0=v_col)))
    if rope_tabs is not None:
        cos, sin = rope_tabs
        args += [cos, sin]
        specs += [pl.BlockSpec((tq, 128), lambda b, p, t: (t, 0)),
                  pl.BlockSpec((tq, 128), lambda b, p, t: (t, 0))]
    aliases = {}
    if out_buf is not None:
        aliases = {len(args): 0}
        args.append(out_buf)
        specs.append(pl.BlockSpec(memory_space=pl.ANY))
    return pl.pallas_call(
        functools.partial(_attn_kernel, nseg=len(segs), has_r=has_r, rope=rope_tabs is not None,
                          aliased=out_buf is not None, scale=scale, dh=dh, dv=dv),
        out_shape=jax.ShapeDtypeStruct((N_TOK, npairs * 2 * dv), BF16),
        grid=(nb, npairs, nq),
        in_specs=specs,
        out_specs=pl.BlockSpec((tq, 2 * dv), lambda b, p, t: (q_blk0 + b * nq + t, p)),
        input_output_aliases=aliases,
        compiler_params=_cparams(("parallel", "parallel", "arbitrary")),
        name="attention",
    )(*args)


def _na_kernel(q_ref, k_ref, v_ref, kc_ref, vc_ref, t2_ref, buf_ref, o_ref, bias_ref, mask_ref):
    del buf_ref
    r0 = pl.program_id(1) * NA_QR
    ks = jnp.clip(r0 - NA_WIN_ROWS // 2, 0, LAT_ROWS - NA_KR)
    nq, nk = NA_QR * GRID_W, NA_KR * GRID_W
    scale = NA_DH ** -0.5

    @pl.when(pl.program_id(2) == 0)
    def _():
        row = lax.broadcasted_iota(jnp.int32, (nq, nk), 0)
        lane = lax.broadcasted_iota(jnp.int32, (nq, nk), 1)
        qc = row & (GRID_W - 1)
        kc = lane & (GRID_W - 1)
        rs = jnp.clip(r0 + (row >> LOG2_GRID_W) - NA_WIN_ROWS // 2, 0, LAT_ROWS - NA_WIN_ROWS)
        kr = ks + (lane >> LOG2_GRID_W)
        cs = jnp.clip(qc - NA_WIN_COLS // 2, 0, GRID_W - NA_WIN_COLS)
        ok = (kr >= rs) & (kr < rs + NA_WIN_ROWS) & (kc >= cs) & (kc < cs + NA_WIN_COLS)
        mask_ref[...] = jnp.where(ok, 1.0, 0.0)
        for hh in range(2):
            for i in range(NA_QR):
                for jp in range(NA_KR // 2):
                    e = jnp.clip(ks - r0 + 2 * jp - i + NA_WIN_ROWS, 0, 2 * NA_WIN_ROWS - 1)
                    bias_ref[hh, i * GRID_W:(i + 1) * GRID_W, jp * 128:(jp + 1) * 128] = t2_ref[hh, e]

    start = pl.multiple_of(ks * GRID_W, GRID_W)
    kw = k_ref[pl.ds(start, nk), :]
    vw = v_ref[pl.ds(start, nk), :]
    ok = mask_ref[...] > 0.5
    for hh in range(2):
        sl = slice(hh * NA_DH, (hh + 1) * NA_DH)
        q = q_ref[:, sl]
        s_loc = jnp.where(ok, _dot_nt(q, kw[:, sl]) * scale + bias_ref[hh], NEG_INF)
        s_ctx = _dot_nt(q, kc_ref[:, sl]) * scale
        m = jnp.maximum(s_loc.max(axis=-1, keepdims=True), s_ctx.max(axis=-1, keepdims=True))
        e_loc = jnp.exp(s_loc - m)
        e_ctx = jnp.exp(s_ctx - m)
        den = e_loc.sum(axis=-1, keepdims=True) + e_ctx.sum(axis=-1, keepdims=True)
        acc = _dot(e_loc.astype(BF16), vw[:, sl]) + _dot(e_ctx.astype(BF16), vc_ref[:, sl])
        o_ref[:, sl] = (acc / den).astype(o_ref.dtype)


def _na_bias_table(rpb):
    col = jnp.arange(GRID_W)
    col_off = jnp.clip(col[None, :] - col[:, None] + NA_WIN_COLS - 1, 0, 2 * NA_WIN_COLS - 2)
    e = jnp.arange(2 * NA_WIN_ROWS)
    dr = jnp.clip(jnp.stack([e - 1, e], axis=1), 0, 2 * NA_WIN_ROWS - 2)
    tb = rpb[:, dr][:, :, :, col_off]
    return jnp.transpose(tb, (0, 1, 3, 2, 4)).reshape(NA_HEADS, 2 * NA_WIN_ROWS, GRID_W, 2 * GRID_W)


def _na_latent(qkv_l, kc, vc, t2, out_buf):
    npairs = NA_HEADS // 2
    w2 = 2 * NA_DH
    nrb = LAT_ROWS // NA_QR
    nq = NA_QR * GRID_W
    return pl.pallas_call(
        _na_kernel,
        out_shape=jax.ShapeDtypeStruct((N_TOK, D), BF16),
        grid=(npairs, nrb, DEC_BATCH),
        in_specs=[
            pl.BlockSpec((nq, w2), lambda p, r, b: (b * nrb + r, p)),
            pl.BlockSpec((DEC_SEQ, w2), lambda p, r, b: (b, npairs + p)),
            pl.BlockSpec((DEC_SEQ, w2), lambda p, r, b: (b, 2 * npairs + p)),
            pl.BlockSpec((PAST, w2), lambda p, r, b: (b, p)),
            pl.BlockSpec((PAST, w2), lambda p, r, b: (b, p)),
            pl.BlockSpec((2, 2 * NA_WIN_ROWS, GRID_W, 2 * GRID_W), lambda p, r, b: (p, 0, 0, 0)),
            pl.BlockSpec(memory_space=pl.ANY),
        ],
        out_specs=pl.BlockSpec((nq, w2), lambda p, r, b: (N_CTX // nq + b * nrb + r, p)),
        scratch_shapes=[pltpu.VMEM((2, nq, NA_KR * GRID_W), F32), pltpu.VMEM((nq, NA_KR * GRID_W), F32)],
        input_output_aliases={6: 0},
        compiler_params=_cparams(("parallel", "parallel", "arbitrary")),
        name="na_latent",
    )(qkv_l, qkv_l, qkv_l, kc, vc, t2, out_buf)


def _gla_pos(t):
    is_ctx = t < GLA_CTX_STEPS
    u = jnp.maximum(t - GLA_CTX_STEPS, 0)
    seq = jnp.where(is_ctx, t // GLA_CTX_CHUNKS, u // GLA_LAT_CHUNKS)
    n = jnp.where(is_ctx, t % GLA_CTX_CHUNKS, u % GLA_LAT_CHUNKS)
    return is_ctx, seq, n


def _gla_bwd_chunk(t):
    is_ctx, _, n = _gla_pos(t)
    return t + jnp.where(is_ctx, GLA_CTX_CHUNKS, GLA_LAT_CHUNKS) - 1 - 2 * n


def _gla_kernel(qf_ref, kf_ref, vf_ref, gdf_ref, qb_ref, kb_ref, vb_ref, gdb_ref, wgu_ref, bg_ref, s0_ref,
                of_ref, ob_ref, fin_ref, st_ref):
    is_ctx, _, n = _gla_pos(pl.program_id(0))
    is_lat = jnp.logical_not(is_ctx)

    @pl.when(jnp.logical_and(n == 0, is_ctx))
    def _():
        st_ref[...] = jnp.zeros_like(st_ref)

    @pl.when(jnp.logical_and(n == 0, is_lat))
    def _():
        st_ref[...] = s0_ref[...]

    ri = lax.broadcasted_iota(jnp.int32, (GLA_CHUNK, GLA_CHUNK), 0)
    ci = lax.broadcasted_iota(jnp.int32, (GLA_CHUNK, GLA_CHUNK), 1)
    streams = ((qf_ref, kf_ref, vf_ref, gdf_ref, of_ref), (qb_ref, kb_ref, vb_ref, gdb_ref, ob_ref))
    for d, (q_ref, k_ref, v_ref, gd_ref, o_ref) in enumerate(streams):
        keep = (ci <= ri) if d == 0 else (ci >= ri)
        tri = jnp.where(keep, 1.0, 0.0).astype(BF16)
        pre = _dot(gd_ref[...].astype(BF16), wgu_ref[d]) + bg_ref[d]
        la = (jnp.minimum(pre, 0.0) - jnp.log1p(jnp.exp(-jnp.abs(pre)))) / GLA_TAU
        hi = la.astype(BF16)
        r1 = la - hi.astype(F32)
        mid = r1.astype(BF16)
        lo = (r1 - mid.astype(F32)).astype(BF16)
        cum = _dot(tri, hi) + _dot(tri, mid) + _dot(tri, lo)
        tot = jnp.sum(la, axis=0, keepdims=True)
        k = k_ref[...]
        q_dec = (q_ref[...] * (GLA_DK ** -0.5) * jnp.exp(cum)).astype(BF16)
        k_inv = (k * jnp.exp(-cum)).astype(BF16)
        k_end = (k * jnp.exp(tot - cum)).astype(BF16)
        dec = jnp.exp(tot)
        v = v_ref[...].astype(BF16)
        for h in range(GLA_HEADS):
            ks = slice(h * GLA_DK, (h + 1) * GLA_DK)
            vs = slice(h * GLA_DV, (h + 1) * GLA_DV)
            att = jnp.where(keep, _dot_nt(q_dec[:, ks], k_inv[:, ks]), 0.0)
            st = st_ref[d, h]
            o_ref[:, vs] = _dot(att.astype(BF16), v[:, vs]) + _dot_nt(q_dec[:, ks], st.astype(BF16))
            st_ref[d, h] = st * dec[:, ks] + _dot_tn(v[:, vs], k_end[:, ks])

    @pl.when(jnp.logical_and(is_ctx, n == GLA_CTX_CHUNKS - 1))
    def _():
        fin_ref[...] = st_ref[...]


def _gla_scan(proj, wgu, bg, s0t):
    hk = GLA_HEADS * GLA_DK
    hv = GLA_HEADS * GLA_DV
    gd_col = (GLA_PROJ_N - 128) // 128

    def fwd(c):
        return lambda t: (t, c)

    def bwd(c):
        return lambda t: (_gla_bwd_chunk(t), c)

    in_specs = []
    for ix in (fwd, bwd):
        in_specs += [pl.BlockSpec((GLA_CHUNK, hk), ix(0)), pl.BlockSpec((GLA_CHUNK, hk), ix(1)),
                     pl.BlockSpec((GLA_CHUNK, hv), ix(1)), pl.BlockSpec((GLA_CHUNK, 128), ix(gd_col))]
    st_block = (None, 2, GLA_HEADS, GLA_DV, GLA_DK)
    in_specs += [
        pl.BlockSpec((2, 128, hk), lambda t: (0, 0, 0)),
        pl.BlockSpec((2, 1, hk), lambda t: (0, 0, 0)),
        pl.BlockSpec(st_block, lambda t: (jnp.where(_gla_pos(t)[0], 0, _gla_pos(t)[1]), 0, 0, 0, 0)),
    ]
    o_shape = jax.ShapeDtypeStruct((N_TOK, hv), F32)
    return pl.pallas_call(
        _gla_kernel,
        out_shape=(o_shape, o_shape, jax.ShapeDtypeStruct((BATCH, 2, GLA_HEADS, GLA_DV, GLA_DK), F32)),
        grid=(GLA_STEPS,),
        in_specs=in_specs,
        out_specs=(
            pl.BlockSpec((GLA_CHUNK, hv), lambda t: (t, 0)),
            pl.BlockSpec((GLA_CHUNK, hv), lambda t: (_gla_bwd_chunk(t), 0)),
            pl.BlockSpec(st_block, lambda t: (jnp.where(_gla_pos(t)[0], _gla_pos(t)[1], BATCH - 1), 0, 0, 0, 0)),
        ),
        scratch_shapes=[pltpu.VMEM((2, GLA_HEADS, GLA_DV, GLA_DK), F32)],
        compiler_params=_cparams(("arbitrary",)),
        name="gla_scan",
    )(*([proj] * 8), wgu, bg, s0t)


def _rope_tables():
    t = jnp.arange(DEC_SEQ)
    d = MLA_ROPE // 2
    inv = ROPE_THETA ** (-jnp.arange(0, d, 2, dtype=F32) / d)
    ang_r = (t // GRID_W).astype(F32)[:, None] * inv[None]
    ang_c = (t % GRID_W).astype(F32)[:, None] * inv[None]
    cos = jnp.concatenate([jnp.cos(ang_r)] * 2 + [jnp.cos(ang_c)] * 2, axis=-1)
    sin = jnp.concatenate([-jnp.sin(ang_r), jnp.sin(ang_r), -jnp.sin(ang_c), jnp.sin(ang_c)], axis=-1)
    return jnp.concatenate([cos, cos], axis=-1), jnp.concatenate([sin, sin], axis=-1)


def _mixer_na(x, mod, layer, g, w_qkv, w_o, rpb, cache_k, cache_v):
    w_qkv = w_qkv.astype(BF16)
    ada = dict(pro="ada", g=g, mod=mod, layer=layer, jmod=1, tn=1536)
    qkv_c = _mm(x, w_qkv, rows=N_CTX, out_dtype=F32, **ada)
    qkv_l = _mm(x, w_qkv, rows=N_LAT, row_off=N_CTX // 1024, out_dtype=BF16, **ada)
    npairs = NA_HEADS // 2
    o = _attention(qkv_c, [dict(k=(qkv_c, 0, npairs), v=(qkv_c, 0, 2 * npairs), lk=SEQ)],
                   nb=BATCH, lq=SEQ, tq=SEQ, q_row0=0, q_col=0, scale=NA_DH ** -0.5)
    kc = cache_k.reshape(DEC_BATCH * PAST, D).astype(BF16)
    vc = cache_v.reshape(DEC_BATCH * PAST, D).astype(BF16)
    o = _na_latent(qkv_l, kc, vc, _na_bias_table(rpb), o)
    x = _mm(o, w_o.astype(BF16), rows=N_TOK, pro="cast", epi="resid", res=x, mod=mod, layer=layer, jmod=1)
    k_c = qkv_c[:, D:2 * D].reshape(BATCH, SEQ, NA_HEADS, NA_DH)
    v_c = qkv_c[:, 2 * D:].reshape(BATCH, SEQ, NA_HEADS, NA_DH)
    return x, k_c, v_c


def _mixer_mla(x, mod, layer, g, w_down, g_q, w_uq, g_kv, w_ukv, w_o, cache_ckv, cache_kr):
    hq = MLA_NOPE + MLA_ROPE
    wd = jnp.concatenate([
        w_down[:, :MLA_Q_LORA],
        w_down[:, MLA_Q_LORA + MLA_KV_LORA:],
        jnp.zeros((D, 256 - MLA_ROPE), F32),
        w_down[:, MLA_Q_LORA:MLA_Q_LORA + MLA_KV_LORA]], axis=1).astype(BF16)
    wq = w_uq.reshape(MLA_Q_LORA, MLA_HEADS, hq)
    wq = jnp.concatenate([wq[:, :, :MLA_NOPE].reshape(MLA_Q_LORA, -1),
                          wq[:, :, MLA_NOPE:].reshape(MLA_Q_LORA, -1)], axis=1).astype(BF16)
    wkv = w_ukv.reshape(MLA_KV_LORA, MLA_HEADS, MLA_NOPE + MLA_V)
    wkv = jnp.concatenate([wkv[:, :, :MLA_NOPE].reshape(MLA_KV_LORA, -1),
                           wkv[:, :, MLA_NOPE:].reshape(MLA_KV_LORA, -1)], axis=1).astype(BF16)

    down = _mm(x, wd, rows=N_TOK, pro="ada", g=g, mod=mod, layer=layer, jmod=1, out_dtype=F32, tn=MLA_DOWN_N)
    q = _mm(down, wq, rows=N_TOK, pro="rms", g=g_q, out_dtype=F32, tn=1536)
    kv = _mm(down, wkv, rows=N_TOK, xcol=2, pro="rms", g=g_kv, out_dtype=BF16, tn=2048)
    kv_cache = _mm(cache_ckv.reshape(DEC_BATCH * PAST, MLA_KV_LORA).astype(BF16), wkv,
                   rows=DEC_BATCH * PAST, pro="cast", out_dtype=BF16, tn=2048)
    ckv_c = _rmsnorm(down, g_kv, rows=N_CTX, xcol=2)
    kr_c = down[:N_CTX, MLA_Q_LORA:MLA_Q_LORA + MLA_ROPE]

    npairs = MLA_HEADS // 2
    kr_col = MLA_Q_LORA // 128
    o = _attention(q, [dict(k=(kv, 0, 0), v=(kv, 0, npairs), kr=(down, 0, kr_col, 128), lk=SEQ)],
                   nb=BATCH, lq=SEQ, tq=SEQ, q_row0=0, q_col=0, qr=(q, 2 * npairs), scale=MLA_SCALE)
    tabs = _rope_tables()
    kr_l = _rope_keys(down, tabs)
    o = _attention(
        q,
        [dict(k=(kv, N_CTX, 0), v=(kv, N_CTX, npairs), kr=(kr_l, 0, 0, 128), lk=DEC_SEQ),
         dict(k=(kv_cache, 0, 0), v=(kv_cache, 0, npairs),
              kr=(cache_kr.reshape(DEC_BATCH * PAST, MLA_ROPE), 0, 0, MLA_ROPE), lk=PAST)],
        nb=DEC_BATCH, lq=DEC_SEQ, tq=512, q_row0=N_CTX, q_col=0, qr=(q, 2 * npairs), scale=MLA_SCALE,
        rope_tabs=tabs, out_buf=o)
    x = _mm(o, w_o.astype(BF16), rows=N_TOK, pro="cast", epi="resid", res=x, mod=mod, layer=layer, jmod=1)
    return x, ckv_c.reshape(BATCH, SEQ, MLA_KV_LORA), kr_c.reshape(BATCH, SEQ, MLA_ROPE)


def _mixer_gla(x, mod, layer, g, w_in, w_gd, w_gu, b_g, g_norm, w_o, state):
    w_cat = jnp.concatenate([w_in, w_gd[0], w_gd[1], jnp.zeros((D, 128 - 2 * GLA_RANK), F32)], axis=1).astype(BF16)
    proj = _mm(x, w_cat, rows=N_TOK, pro="ada", g=g, mod=mod, layer=layer, jmod=1, out_dtype=F32, tn=896)
    wgu = jnp.zeros((2, 128, GLA_HEADS * GLA_DK), F32)
    wgu = wgu.at[0, :GLA_RANK].set(w_gu[0]).at[1, GLA_RANK:2 * GLA_RANK].set(w_gu[1]).astype(BF16)
    bg = b_g.reshape(2, 1, GLA_HEADS * GLA_DK)
    o_f, o_b, st_c = _gla_scan(proj, wgu, bg, jnp.swapaxes(state, -1, -2))
    x = _mm(o_f, w_o.astype(BF16), rows=N_TOK, pro="gla", x2=o_b, gin=proj, gin_col=2, g=g_norm,
            epi="resid", res=x, mod=mod, layer=layer, jmod=1, tm=512)
    return x, jnp.swapaxes(st_c, -1, -2)


def kernel(x_prompt, x_sample, cache_na_k, cache_na_v, cache_mla_ckv, cache_mla_krope, state_gla, c, c_ctx, norm_g, w_ada, b_ada, w_ffn_in, w_ffn_out, w_na_qkv, w_na_o, na_rpb, w_mla_down, g_mla_q, w_mla_uq, g_mla_kv, w_mla_ukv, w_mla_o, w_gla_in, w_gla_gate_down, w_gla_gate_up, b_gla_gate, g_gla_norm, w_gla_o, final_norm_g):
    x = jnp.concatenate([x_prompt.reshape(N_CTX, D), x_sample.reshape(N_LAT, D)], axis=0)
    cond = jnp.concatenate([c_ctx[None], c, jnp.zeros((N_GROUPS - 1 - DEC_BATCH, D), F32)], axis=0)
    mod = _ada_mod(cond, w_ada, b_ada)
    w_in = w_ffn_in.astype(BF16)
    w_out = w_ffn_out.astype(BF16)

    na_k, na_v, ckv, krope, gla_st = [], [], [], [], []
    for i in range(DEPTH):
        kind, slot = i % 3, i // 3
        x = _ffn(x, mod, i, 0, w_in, w_out, norm_g[i, 0])
        if kind == 0:
            x, k_c, v_c = _mixer_na(x, mod, i, norm_g[i, 1], w_na_qkv[slot], w_na_o[slot], na_rpb[slot],
                                    cache_na_k[:, slot], cache_na_v[:, slot])
            na_k.append(k_c)
            na_v.append(v_c)
        elif kind == 1:
            x, ckv_c, kr_c = _mixer_mla(x, mod, i, norm_g[i, 1], w_mla_down[slot], g_mla_q[slot], w_mla_uq[slot],
                                        g_mla_kv[slot], w_mla_ukv[slot], w_mla_o[slot],
                                        cache_mla_ckv[:, slot], cache_mla_krope[:, slot])
            ckv.append(ckv_c)
            krope.append(kr_c)
        else:
            x, st = _mixer_gla(x, mod, i, norm_g[i, 1], w_gla_in[slot], w_gla_gate_down[slot],
                               w_gla_gate_up[slot], b_gla_gate[slot], g_gla_norm[slot], w_gla_o[slot],
                               state_gla[:, slot])
            gla_st.append(st)
        x = _ffn(x, mod, i, 1, w_in, w_out, norm_g[i, 2])

    y_prompt = _rmsnorm(x, final_norm_g, rows=N_CTX).reshape(BATCH, SEQ, D)
    y_sample = _rmsnorm(x, final_norm_g, rows=N_LAT, row_off=N_CTX // 512).reshape(DEC_BATCH, DEC_SEQ, D)
    return (y_prompt, y_sample, jnp.stack(na_k, axis=1), jnp.stack(na_v, axis=1), jnp.stack(ckv, axis=1),
            jnp.stack(krope, axis=1), jnp.stack(gla_st, axis=1))
```

```python
import functools

import jax
import jax.numpy as jnp
from jax import lax
from jax.experimental import pallas as pl
from jax.experimental.pallas import tpu as pltpu

F32 = jnp.float32
BF16 = jnp.bfloat16

D = 2048
BATCH, SEQ = 16, 256
DEC_BATCH, DEC_SEQ = 4, 2048
PAST = 512
DEPTH = 4
N_MOD = 9
EPS = 1e-6
D_FF = 5632
GRID_W = 64
LOG2_GRID_W = 6
NEG_INF = -1e30
LOG2E = 1.4426950408889634

N_CTX = BATCH * SEQ
N_LAT = DEC_BATCH * DEC_SEQ
N_TOK = N_CTX + N_LAT
N_GROUPS = 8

NA_HEADS, NA_DH = 16, 128
NA_WIN_ROWS, NA_WIN_COLS = 8, 16
LAT_ROWS = DEC_SEQ // GRID_W
NA_QR = 8
NA_KR = NA_QR + NA_WIN_ROWS

MLA_HEADS = 16
MLA_Q_LORA, MLA_KV_LORA = 768, 512
MLA_NOPE, MLA_ROPE, MLA_V = 128, 64, 128
MLA_SCALE = (MLA_NOPE + MLA_ROPE) ** -0.5
ROPE_THETA = 10000.0
MLA_DOWN_N = 1536

GLA_HEADS, GLA_DK, GLA_DV = 4, 256, 512
GLA_RANK = 16
GLA_TAU = 16.0
GLA_CHUNK = 64
GLA_PROJ_N = 2 * GLA_HEADS * GLA_DK + 2 * GLA_HEADS * GLA_DV + 128
GLA_CTX_CHUNKS = SEQ // GLA_CHUNK
GLA_LAT_CHUNKS = DEC_SEQ // GLA_CHUNK
GLA_CTX_STEPS = BATCH * GLA_CTX_CHUNKS
GLA_STEPS = GLA_CTX_STEPS + DEC_BATCH * GLA_LAT_CHUNKS

VMEM_LIMIT = 56 * 1024 * 1024
ROW_CHUNK = 16
FFN_TF = 512
FFN_SUB = 512


def _cparams(sem):
    return pltpu.CompilerParams(dimension_semantics=sem, vmem_limit_bytes=VMEM_LIMIT)


def _group_of_row(r0):
    return jnp.where(r0 < N_CTX, 0, 1 + (r0 - N_CTX) // DEC_SEQ)


def _silu(x):
    return x / (1.0 + jnp.exp(-x))


def _rms(x):
    return x * lax.rsqrt(jnp.mean(x * x, axis=-1, keepdims=True) + EPS)


def _for_row_chunks(n_rows, body):
    def step(c, carry):
        body(pl.ds(pl.multiple_of(c * ROW_CHUNK, ROW_CHUNK), ROW_CHUNK))
        return carry
    lax.fori_loop(0, n_rows // ROW_CHUNK, step, 0, unroll=8)


def _dot(a, b):
    return jnp.dot(a, b, preferred_element_type=F32)


def _dot_nt(a, b):
    return lax.dot_general(a, b, (((1,), (1,)), ((), ())), preferred_element_type=F32)


def _dot_tn(a, b):
    return lax.dot_general(a, b, (((0,), (0,)), ((), ())), preferred_element_type=F32)


def _ada_kernel(c_ref, w_ref, b_ref, o_ref):
    s = _silu(c_ref[...]).astype(BF16)
    o_ref[...] = _dot(s, w_ref[...].astype(BF16)) + b_ref[...]


def _ada_mod(cond, w_ada, b_ada):
    tn = 1024
    n = N_MOD * D
    out = pl.pallas_call(
        _ada_kernel,
        out_shape=jax.ShapeDtypeStruct((DEPTH, N_GROUPS, n), F32),
        grid=(DEPTH, n // tn),
        in_specs=[
            pl.BlockSpec((N_GROUPS, D), lambda l, j: (0, 0)),
            pl.BlockSpec((None, D, tn), lambda l, j: (l, 0, j)),
            pl.BlockSpec((None, 1, tn), lambda l, j: (l, 0, j)),
        ],
        out_specs=pl.BlockSpec((None, N_GROUPS, tn), lambda l, j: (l, 0, j)),
        compiler_params=_cparams(("parallel", "parallel")),
        name="ada_mod",
    )(cond, w_ada, b_ada.reshape(DEPTH, 1, n))
    return out.reshape(DEPTH, N_GROUPS, N_MOD, 1, D)


def _mod_spec(layer, j, tm, row_off):
    return pl.BlockSpec(
        (None, None, None, 1, D),
        lambda i, n: (layer, _group_of_row((i + row_off) * tm), j, 0, 0))


def _ffn_kernel(x_ref, g_ref, sh_ref, sc_ref, gt_ref, wg_ref, wu_ref, wo_ref, o_ref, h_ref, gm_ref):
    f = pl.program_id(1)
    tm = x_ref.shape[0]

    @pl.when(f == 0)
    def _():
        gm_ref[...] = g_ref[...] * (1.0 + sc_ref[...])

        def rows_fn(rows):
            h_ref[rows, :] = (_rms(x_ref[rows, :]) * gm_ref[...] + sh_ref[...]).astype(BF16)
            o_ref[rows, :] = jnp.zeros((ROW_CHUNK, D), F32)

        _for_row_chunks(tm, rows_fn)

    for r in range(0, tm, FFN_SUB):
        h = h_ref[r:r + FFN_SUB, :]
        a = _silu(_dot(h, wg_ref[...])) * _dot(h, wu_ref[...])
        o_ref[r:r + FFN_SUB, :] += _dot(a.astype(BF16), wo_ref[...])

    @pl.when(f == pl.num_programs(1) - 1)
    def _():
        o_ref[...] = x_ref[...] + 0.5 * gt_ref[...] * o_ref[...]


def _ffn(x, mod, layer, half, w_in, w_out, g, *, tm=1024):
    tf = FFN_TF
    nf = D_FF // tf
    j = 2 * half
    return pl.pallas_call(
        _ffn_kernel,
        out_shape=jax.ShapeDtypeStruct((N_TOK, D), F32),
        grid=(N_TOK // tm, nf),
        in_specs=[
            pl.BlockSpec((tm, D), lambda i, f: (i, 0)),
            pl.BlockSpec((1, D), lambda i, f: (0, 0)),
            _mod_spec(layer, 3 * j, tm, 0),
            _mod_spec(layer, 3 * j + 1, tm, 0),
            _mod_spec(layer, 3 * j + 2, tm, 0),
            pl.BlockSpec((None, None, D, tf), lambda i, f: (layer, half, 0, f)),
            pl.BlockSpec((None, None, D, tf), lambda i, f: (layer, half, 0, nf + f)),
            pl.BlockSpec((None, None, tf, D), lambda i, f: (layer, half, f, 0)),
        ],
        out_specs=pl.BlockSpec((tm, D), lambda i, f: (i, 0)),
        scratch_shapes=[pltpu.VMEM((tm, D), BF16), pltpu.VMEM((1, D), F32)],
        compiler_params=_cparams(("parallel", "arbitrary")),
        name="ffn_half",
    )(x, g.reshape(1, D), mod, mod, mod, w_in, w_in, w_out)


def _mm_kernel(*refs, pro, epi, split):
    it = iter(refs)
    x_ref = next(it)
    if pro == "cast" and split is not None:
        x2_ref = next(it)
    if pro == "gla":
        x2_ref, gin_ref = next(it), next(it)
    if pro in ("ada", "rms", "gla"):
        g_ref = next(it)
    if pro == "ada":
        sh_ref, sc_ref = next(it), next(it)
    w_ref = next(it)
    if epi == "resid":
        res_ref, gt_ref = next(it), next(it)
    o_ref = next(it)
    if pro != "cast":
        h_ref = next(it)
    if pro == "ada":
        gm_ref = next(it)

    if pro != "cast":
        @pl.when(pl.program_id(1) == 0)
        def _():
            if pro == "ada":
                gm_ref[...] = g_ref[...] * (1.0 + sc_ref[...])

            def rows_fn(rows):
                if pro == "ada":
                    h = _rms(x_ref[rows, :]) * gm_ref[...] + sh_ref[...]
                elif pro == "rms":
                    h = _rms(x_ref[rows, :]) * g_ref[...]
                else:
                    o = x_ref[rows, :] + x2_ref[rows, :]
                    parts = [_rms(o[:, k * GLA_DV:(k + 1) * GLA_DV]) * g_ref[...] for k in range(GLA_HEADS)]
                    h = jnp.concatenate(parts, axis=-1) * _silu(gin_ref[rows, :])
                h_ref[rows, :] = h.astype(BF16)

            _for_row_chunks(x_ref.shape[0], rows_fn)

    def finish(lhs_ref):
        y = _dot(lhs_ref[...], w_ref[...])
        if epi == "resid":
            y = res_ref[...] + gt_ref[...] * y
        o_ref[...] = y.astype(o_ref.dtype)

    if pro != "cast":
        finish(h_ref)
    elif split is None:
        finish(x_ref)
    else:
        pl.when(pl.program_id(0) < split)(lambda: finish(x_ref))
        pl.when(pl.program_id(0) >= split)(lambda: finish(x2_ref))


def _mm(x, w, *, rows, row_off=0, xcol=0, wcol=0, n_out=None, pro, epi="plain", out_dtype=F32, tm=1024, tn=1024,
        g=None, mod=None, layer=None, jmod=None, x2=None, gin=None, gin_col=0, res=None):
    kdim = w.shape[0]
    n = n_out or w.shape[1]
    tn = min(tn, n)
    assert rows % tm == 0 and n % tn == 0
    split = None
    if pro == "cast" and x2 is not None:
        split = x.shape[0] // tm
        xspec = pl.BlockSpec((tm, kdim), lambda i, j: (jnp.minimum(i, split - 1), 0))
        args, specs = [x, x2], [xspec, pl.BlockSpec((tm, kdim), lambda i, j: (jnp.maximum(i - split, 0), 0))]
    else:
        xspec = pl.BlockSpec((tm, kdim), lambda i, j: (i + row_off, xcol))
        args, specs = [x], [xspec]
    if pro == "gla":
        args += [x2, gin]
        specs += [xspec, pl.BlockSpec((tm, kdim), lambda i, j: (i + row_off, gin_col))]
    if pro in ("ada", "rms", "gla"):
        args.append(g.reshape(1, -1))
        specs.append(pl.BlockSpec((1, g.shape[-1]), lambda i, j: (0, 0)))
    if pro == "ada":
        args += [mod, mod]
        specs += [_mod_spec(layer, 3 * jmod, tm, row_off), _mod_spec(layer, 3 * jmod + 1, tm, row_off)]
    args.append(w)
    specs.append(pl.BlockSpec((kdim, tn), lambda i, j: (0, wcol * (n // tn) + j)))
    if epi == "resid":
        args += [res, mod]
        specs += [
            pl.BlockSpec((tm, tn), lambda i, j: (i + row_off, j)),
            pl.BlockSpec((None, None, None, 1, tn),
                         lambda i, j: (layer, _group_of_row((i + row_off) * tm), 3 * jmod + 2, 0, j)),
        ]
    scratch = []
    if pro != "cast":
        scratch.append(pltpu.VMEM((tm, kdim), BF16))
    if pro == "ada":
        scratch.append(pltpu.VMEM((1, kdim), F32))
    return pl.pallas_call(
        functools.partial(_mm_kernel, pro=pro, epi=epi, split=split),
        out_shape=jax.ShapeDtypeStruct((rows, n), out_dtype),
        grid=(rows // tm, n // tn),
        in_specs=specs,
        out_specs=pl.BlockSpec((tm, tn), lambda i, j: (i, j)),
        scratch_shapes=scratch,
        compiler_params=_cparams(("parallel", "arbitrary")),
        name="proj_" + pro + "_" + epi,
    )(*args)


def _rmsnorm_kernel(x_ref, g_ref, o_ref):
    o_ref[...] = _rms(x_ref[...]) * g_ref[...]


def _rmsnorm(x, g, *, rows, row_off=0, xcol=0, tm=512):
    width = g.shape[-1]
    return pl.pallas_call(
        _rmsnorm_kernel,
        out_shape=jax.ShapeDtypeStruct((rows, width), F32),
        grid=(rows // tm,),
        in_specs=[pl.BlockSpec((tm, width), lambda i: (i + row_off, xcol)),
                  pl.BlockSpec((1, width), lambda i: (0, 0))],
        out_specs=pl.BlockSpec((tm, width), lambda i: (i, 0)),
        compiler_params=_cparams(("parallel",)),
        name="rmsnorm",
    )(x, g.reshape(1, width))


def _rope(x, cos, sin):
    width = x.shape[-1]
    lane = lax.broadcasted_iota(jnp.int32, x.shape, 1)
    up = pltpu.roll(x, width - 16, 1)
    down = pltpu.roll(x, 16, 1)
    swapped = jnp.where((lane & 31) < 16, up, down)
    return x * cos + swapped * sin


def _rope_keys_kernel(x_ref, cos_ref, sin_ref, o_ref):
    o_ref[...] = _rope(x_ref[...], cos_ref[...], sin_ref[...]).astype(o_ref.dtype)


def _rope_keys(down, rope_tabs, *, tm=1024):
    cos, sin = rope_tabs
    per_seq = DEC_SEQ // tm
    return pl.pallas_call(
        _rope_keys_kernel,
        out_shape=jax.ShapeDtypeStruct((N_LAT, 128), BF16),
        grid=(N_LAT // tm,),
        in_specs=[pl.BlockSpec((tm, 128), lambda i: (N_CTX // tm + i, MLA_Q_LORA // 128)),
                  pl.BlockSpec((tm, 128), lambda i: (i % per_seq, 0)),
                  pl.BlockSpec((tm, 128), lambda i: (i % per_seq, 0))],
        out_specs=pl.BlockSpec((tm, 128), lambda i: (i, 0)),
        compiler_params=_cparams(("parallel",)),
        name="rope_keys",
    )(down, cos, sin)


def _attn_kernel(*refs, nseg, has_r, rope, scale, dh, dv):
    it = iter(refs)
    q_ref = next(it)
    qr_ref = next(it) if has_r else None
    segs = []
    for _ in range(nseg):
        k_ref = next(it)
        kr_ref = next(it) if has_r else None
        v_ref = next(it)
        segs.append((k_ref, kr_ref, v_ref))
    if rope:
        cq_ref, sq_ref = next(it), next(it)
    o_ref = next(it)

    if has_r:
        qr = qr_ref[...].astype(F32)
        if rope:
            qr = _rope(qr, cq_ref[...], sq_ref[...])
        qrs = [qr[:, hh * MLA_ROPE:(hh + 1) * MLA_ROPE].astype(BF16) for hh in range(2)]
        krs = [kr_ref[:, :MLA_ROPE].astype(BF16) for _, kr_ref, _ in segs]

    scores = []
    for hh in range(2):
        q = q_ref[:, hh * dh:(hh + 1) * dh].astype(BF16)
        ss = []
        for si, (k_ref, _, _) in enumerate(segs):
            s = _dot_nt(q, k_ref[:, hh * dh:(hh + 1) * dh].astype(BF16))
            if has_r:
                s = s + _dot_nt(qrs[hh], krs[si])
            ss.append(s * (scale * LOG2E))
        scores.append(ss)
    for hh in range(2):
        ss = scores[hh]
        m = ss[0].max(axis=-1, keepdims=True)
        for s in ss[1:]:
            m = jnp.maximum(m, s.max(axis=-1, keepdims=True))
        acc = None
        den = None
        for s, (_, _, v_ref) in zip(ss, segs):
            e = jnp.exp2(s - m)
            d = e.sum(axis=-1, keepdims=True)
            pv = _dot(e.astype(BF16), v_ref[:, hh * dv:(hh + 1) * dv].astype(BF16))
            acc = pv if acc is None else acc + pv
            den = d if den is None else den + d
        o_ref[:, hh * dv:(hh + 1) * dv] = (acc / den).astype(o_ref.dtype)


def _attention(q, segs, *, nb, lq, tq, q_row0, q_col, scale, qr=None, rope_tabs=None, dh=128, dv=128):
    npairs = 8
    nq = lq // tq
    has_r = qr is not None
    q_blk0 = q_row0 // tq
    args, specs = [q], [pl.BlockSpec((tq, 2 * dh), lambda b, p, t: (q_blk0 + b * nq + t, q_col + p))]
    if has_r:
        qr_arr, qr_col = qr
        args.append(qr_arr)
        specs.append(pl.BlockSpec((tq, 2 * MLA_ROPE), lambda b, p, t: (q_blk0 + b * nq + t, qr_col + p)))
    for sg in segs:
        lk = sg["lk"]
        k_arr, k_row0, k_col = sg["k"]
        args.append(k_arr)
        specs.append(pl.BlockSpec((lk, 2 * dh), functools.partial(
            lambda b, p, t, r0, c0: (r0 + b, c0 + p), r0=k_row0 // lk, c0=k_col)))
        if has_r:
            kr_arr, kr_row0, kr_col, kr_w = sg["kr"]
            args.append(kr_arr)
            specs.append(pl.BlockSpec((lk, kr_w), functools.partial(
                lambda b, p, t, r0, c0: (r0 + b, c0), r0=kr_row0 // lk, c0=kr_col)))
        v_arr, v_row0, v_col = sg["v"]
        args.append(v_arr)
        specs.append(pl.BlockSpec((lk, 2 * dv), functools.partial(
            lambda b, p, t, r0, c0: (r0 + b, c0 + p), r0=v_row0 // lk, c0=v_col)))
    if rope_tabs is not None:
        cos, sin = rope_tabs
        args += [cos, sin]
        specs += [pl.BlockSpec((tq, 128), lambda b, p, t: (t, 0)),
                  pl.BlockSpec((tq, 128), lambda b, p, t: (t, 0))]
    return pl.pallas_call(
        functools.partial(_attn_kernel, nseg=len(segs), has_r=has_r, rope=rope_tabs is not None,
                          scale=scale, dh=dh, dv=dv),
        out_shape=jax.ShapeDtypeStruct((nb * lq, npairs * 2 * dv), BF16),
        grid=(nb, npairs, nq),
        in_specs=specs,
        out_specs=pl.BlockSpec((tq, 2 * dv), lambda b, p, t: (b * nq + t, p)),
        compiler_params=_cparams(("parallel", "parallel", "arbitrary")),
        name="attention",
    )(*args)


def _na_kernel(q_ref, k_ref, v_ref, kc_ref, vc_ref, t2_ref, o_ref, bias_ref, cap_ref):
    r0 = pl.program_id(1) * NA_QR
    ks = jnp.clip(r0 - NA_WIN_ROWS // 2, 0, LAT_ROWS - NA_KR)
    nq, nk = NA_QR * GRID_W, NA_KR * GRID_W
    scale = NA_DH ** -0.5 * LOG2E

    @pl.when(pl.program_id(2) == 0)
    def _():
        row = lax.broadcasted_iota(jnp.int32, (nq, nk), 0)
        lane = lax.broadcasted_iota(jnp.int32, (nq, nk), 1)
        qc = row & (GRID_W - 1)
        kc = lane & (GRID_W - 1)
        rs = jnp.clip(r0 + (row >> LOG2_GRID_W) - NA_WIN_ROWS // 2, 0, LAT_ROWS - NA_WIN_ROWS)
        kr = ks + (lane >> LOG2_GRID_W)
        cs = jnp.clip(qc - NA_WIN_COLS // 2, 0, GRID_W - NA_WIN_COLS)
        ok = (kr >= rs) & (kr < rs + NA_WIN_ROWS) & (kc >= cs) & (kc < cs + NA_WIN_COLS)
        cap_ref[...] = jnp.where(ok, jnp.inf, NEG_INF)
        for hh in range(2):
            for i in range(NA_QR):
                for jp in range(NA_KR // 2):
                    e = jnp.clip(ks - r0 + 2 * jp - i + NA_WIN_ROWS, 0, 2 * NA_WIN_ROWS - 1)
                    bias_ref[hh, i * GRID_W:(i + 1) * GRID_W, jp * 128:(jp + 1) * 128] = t2_ref[hh, e] * LOG2E

    start = pl.multiple_of(ks * GRID_W, GRID_W)
    kw = k_ref[pl.ds(start, nk), :]
    vw = v_ref[pl.ds(start, nk), :]
    cap = cap_ref[...]
    for hh in range(2):
        sl = slice(hh * NA_DH, (hh + 1) * NA_DH)
        q = q_ref[:, sl]
        s_loc = jnp.minimum(_dot_nt(q, kw[:, sl]) * scale + bias_ref[hh], cap)
        s_ctx = _dot_nt(q, kc_ref[:, sl]) * scale
        m = jnp.maximum(s_loc.max(axis=-1, keepdims=True), s_ctx.max(axis=-1, keepdims=True))
        e_loc = jnp.exp2(s_loc - m)
        e_ctx = jnp.exp2(s_ctx - m)
        den = e_loc.sum(axis=-1, keepdims=True) + e_ctx.sum(axis=-1, keepdims=True)
        acc = _dot(e_loc.astype(BF16), vw[:, sl]) + _dot(e_ctx.astype(BF16), vc_ref[:, sl])
        o_ref[:, sl] = (acc / den).astype(o_ref.dtype)


def _na_bias_table(rpb):
    col = jnp.arange(GRID_W)
    col_off = jnp.clip(col[None, :] - col[:, None] + NA_WIN_COLS - 1, 0, 2 * NA_WIN_COLS - 2)
    e = jnp.arange(2 * NA_WIN_ROWS)
    dr = jnp.clip(jnp.stack([e - 1, e], axis=1), 0, 2 * NA_WIN_ROWS - 2)
    tb = rpb[:, dr][:, :, :, col_off]
    return jnp.transpose(tb, (0, 1, 3, 2, 4)).reshape(NA_HEADS, 2 * NA_WIN_ROWS, GRID_W, 2 * GRID_W)


def _na_latent(qkv_l, kc, vc, t2):
    npairs = NA_HEADS // 2
    w2 = 2 * NA_DH
    nrb = LAT_ROWS // NA_QR
    nq = NA_QR * GRID_W
    return pl.pallas_call(
        _na_kernel,
        out_shape=jax.ShapeDtypeStruct((N_LAT, D), BF16),
        grid=(npairs, nrb, DEC_BATCH),
        in_specs=[
            pl.BlockSpec((nq, w2), lambda p, r, b: (b * nrb + r, p)),
            pl.BlockSpec((DEC_SEQ, w2), lambda p, r, b: (b, npairs + p)),
            pl.BlockSpec((DEC_SEQ, w2), lambda p, r, b: (b, 2 * npairs + p)),
            pl.BlockSpec((PAST, w2), lambda p, r, b: (b, p)),
            pl.BlockSpec((PAST, w2), lambda p, r, b: (b, p)),
            pl.BlockSpec((2, 2 * NA_WIN_ROWS, GRID_W, 2 * GRID_W), lambda p, r, b: (p, 0, 0, 0)),
        ],
        out_specs=pl.BlockSpec((nq, w2), lambda p, r, b: (b * nrb + r, p)),
        scratch_shapes=[pltpu.VMEM((2, nq, NA_KR * GRID_W), F32), pltpu.VMEM((nq, NA_KR * GRID_W), F32)],
        compiler_params=_cparams(("parallel", "parallel", "arbitrary")),
        name="na_latent",
    )(qkv_l, qkv_l, qkv_l, kc, vc, t2)


def _gla_pos(t):
    is_ctx = t < GLA_CTX_STEPS
    u = jnp.maximum(t - GLA_CTX_STEPS, 0)
    seq = jnp.where(is_ctx, t // GLA_CTX_CHUNKS, u // GLA_LAT_CHUNKS)
    n = jnp.where(is_ctx, t % GLA_CTX_CHUNKS, u % GLA_LAT_CHUNKS)
    return is_ctx, seq, n


def _gla_bwd_chunk(t):
    is_ctx, _, n = _gla_pos(t)
    return t + jnp.where(is_ctx, GLA_CTX_CHUNKS, GLA_LAT_CHUNKS) - 1 - 2 * n


def _gla_kernel(qf_ref, kf_ref, vf_ref, gdf_ref, qb_ref, kb_ref, vb_ref, gdb_ref, wgu_ref, bg_ref, s0_ref,
                of_ref, ob_ref, fin_ref, st_ref):
    is_ctx, _, n = _gla_pos(pl.program_id(0))
    is_lat = jnp.logical_not(is_ctx)

    @pl.when(jnp.logical_and(n == 0, is_ctx))
    def _():
        st_ref[...] = jnp.zeros_like(st_ref)

    @pl.when(jnp.logical_and(n == 0, is_lat))
    def _():
        st_ref[...] = s0_ref[...]

    ri = lax.broadcasted_iota(jnp.int32, (GLA_CHUNK, GLA_CHUNK), 0)
    ci = lax.broadcasted_iota(jnp.int32, (GLA_CHUNK, GLA_CHUNK), 1)
    streams = ((qf_ref, kf_ref, vf_ref, gdf_ref, of_ref), (qb_ref, kb_ref, vb_ref, gdb_ref, ob_ref))
    for d, (q_ref, k_ref, v_ref, gd_ref, o_ref) in enumerate(streams):
        keep = (ci <= ri) if d == 0 else (ci >= ri)
        tri = jnp.where(keep, 1.0, 0.0).astype(BF16)
        pre = _dot(gd_ref[...].astype(BF16), wgu_ref[d]) + bg_ref[d]
        la = (jnp.minimum(pre, 0.0) - jnp.log1p(jnp.exp(-jnp.abs(pre)))) / GLA_TAU
        hi = la.astype(BF16)
        r1 = la - hi.astype(F32)
        mid = r1.astype(BF16)
        lo = (r1 - mid.astype(F32)).astype(BF16)
        cum = _dot(tri, hi) + _dot(tri, mid) + _dot(tri, lo)
        tot = jnp.sum(la, axis=0, keepdims=True)
        k = k_ref[...]
        q_dec = (q_ref[...] * (GLA_DK ** -0.5) * jnp.exp(cum)).astype(BF16)
        k_inv = (k * jnp.exp(-cum)).astype(BF16)
        k_end = (k * jnp.exp(tot - cum)).astype(BF16)
        dec = jnp.exp(tot)
        v = v_ref[...].astype(BF16)
        for h in range(GLA_HEADS):
            ks = slice(h * GLA_DK, (h + 1) * GLA_DK)
            vs = slice(h * GLA_DV, (h + 1) * GLA_DV)
            att = jnp.where(keep, _dot_nt(q_dec[:, ks], k_inv[:, ks]), 0.0)
            st = st_ref[d, h]
            o_ref[:, vs] = _dot(att.astype(BF16), v[:, vs]) + _dot_nt(q_dec[:, ks], st.astype(BF16))
            st_ref[d, h] = st * dec[:, ks] + _dot_tn(v[:, vs], k_end[:, ks])

    @pl.when(jnp.logical_and(is_ctx, n == GLA_CTX_CHUNKS - 1))
    def _():
        fin_ref[...] = st_ref[...]


def _gla_scan(proj, wgu, bg, s0t):
    hk = GLA_HEADS * GLA_DK
    hv = GLA_HEADS * GLA_DV
    gd_col = (GLA_PROJ_N - 128) // 128

    def fwd(c):
        return lambda t: (t, c)

    def bwd(c):
        return lambda t: (_gla_bwd_chunk(t), c)

    in_specs = []
    for ix in (fwd, bwd):
        in_specs += [pl.BlockSpec((GLA_CHUNK, hk), ix(0)), pl.BlockSpec((GLA_CHUNK, hk), ix(1)),
                     pl.BlockSpec((GLA_CHUNK, hv), ix(1)), pl.BlockSpec((GLA_CHUNK, 128), ix(gd_col))]
    st_block = (None, 2, GLA_HEADS, GLA_DV, GLA_DK)
    in_specs += [
        pl.BlockSpec((2, 128, hk), lambda t: (0, 0, 0)),
        pl.BlockSpec((2, 1, hk), lambda t: (0, 0, 0)),
        pl.BlockSpec(st_block, lambda t: (jnp.where(_gla_pos(t)[0], 0, _gla_pos(t)[1]), 0, 0, 0, 0)),
    ]
    o_shape = jax.ShapeDtypeStruct((N_TOK, hv), F32)
    return pl.pallas_call(
        _gla_kernel,
        out_shape=(o_shape, o_shape, jax.ShapeDtypeStruct((BATCH, 2, GLA_HEADS, GLA_DV, GLA_DK), F32)),
        grid=(GLA_STEPS,),
        in_specs=in_specs,
        out_specs=(
            pl.BlockSpec((GLA_CHUNK, hv), lambda t: (t, 0)),
            pl.BlockSpec((GLA_CHUNK, hv), lambda t: (_gla_bwd_chunk(t), 0)),
            pl.BlockSpec(st_block, lambda t: (jnp.where(_gla_pos(t)[0], _gla_pos(t)[1], BATCH - 1), 0, 0, 0, 0)),
        ),
        scratch_shapes=[pltpu.VMEM((2, GLA_HEADS, GLA_DV, GLA_DK), F32)],
        compiler_params=_cparams(("arbitrary",)),
        name="gla_scan",
    )(*([proj] * 8), wgu, bg, s0t)


def _rope_tables():
    t = jnp.arange(DEC_SEQ)
    d = MLA_ROPE // 2
    inv = ROPE_THETA ** (-jnp.arange(0, d, 2, dtype=F32) / d)
    ang_r = (t // GRID_W).astype(F32)[:, None] * inv[None]
    ang_c = (t % GRID_W).astype(F32)[:, None] * inv[None]
    cos = jnp.concatenate([jnp.cos(ang_r)] * 2 + [jnp.cos(ang_c)] * 2, axis=-1)
    sin = jnp.concatenate([-jnp.sin(ang_r), jnp.sin(ang_r), -jnp.sin(ang_c), jnp.sin(ang_c)], axis=-1)
    return jnp.concatenate([cos, cos], axis=-1), jnp.concatenate([sin, sin], axis=-1)


def _mixer_na(x, mod, layer, g, w_qkv, w_o, rpb, cache_k, cache_v):
    w_qkv = w_qkv.astype(BF16)
    ada = dict(pro="ada", g=g, mod=mod, layer=layer, jmod=1)
    q_c, k_c, v_c = (_mm(x, w_qkv, rows=N_CTX, wcol=c, n_out=D, out_dtype=F32, **ada) for c in range(3))
    qkv_l = _mm(x, w_qkv, rows=N_LAT, row_off=N_CTX // 1024, out_dtype=BF16, tn=1536, **ada)
    o_c = _attention(q_c, [dict(k=(k_c, 0, 0), v=(v_c, 0, 0), lk=SEQ)],
                     nb=BATCH, lq=SEQ, tq=SEQ, q_row0=0, q_col=0, scale=NA_DH ** -0.5)
    kc = cache_k.reshape(DEC_BATCH * PAST, D).astype(BF16)
    vc = cache_v.reshape(DEC_BATCH * PAST, D).astype(BF16)
    o_l = _na_latent(qkv_l, kc, vc, _na_bias_table(rpb))
    x = _mm(o_c, w_o.astype(BF16), rows=N_TOK, pro="cast", x2=o_l, epi="resid", res=x, mod=mod, layer=layer,
            jmod=1)
    return x, k_c.reshape(BATCH, SEQ, NA_HEADS, NA_DH), v_c.reshape(BATCH, SEQ, NA_HEADS, NA_DH)


def _mixer_mla(x, mod, layer, g, w_down, g_q, w_uq, g_kv, w_ukv, w_o, cache_ckv, cache_kr):
    hq = MLA_NOPE + MLA_ROPE
    wd = jnp.concatenate([
        w_down[:, :MLA_Q_LORA],
        w_down[:, MLA_Q_LORA + MLA_KV_LORA:],
        jnp.zeros((D, 256 - MLA_ROPE), F32),
        w_down[:, MLA_Q_LORA:MLA_Q_LORA + MLA_KV_LORA]], axis=1).astype(BF16)
    wq = w_uq.reshape(MLA_Q_LORA, MLA_HEADS, hq)
    wq = jnp.concatenate([wq[:, :, :MLA_NOPE].reshape(MLA_Q_LORA, -1),
                          wq[:, :, MLA_NOPE:].reshape(MLA_Q_LORA, -1)], axis=1).astype(BF16)
    wkv = w_ukv.reshape(MLA_KV_LORA, MLA_HEADS, MLA_NOPE + MLA_V)
    wkv = jnp.concatenate([wkv[:, :, :MLA_NOPE].reshape(MLA_KV_LORA, -1),
                           wkv[:, :, MLA_NOPE:].reshape(MLA_KV_LORA, -1)], axis=1).astype(BF16)

    down = _mm(x, wd, rows=N_TOK, pro="ada", g=g, mod=mod, layer=layer, jmod=1, out_dtype=F32, tn=MLA_DOWN_N)
    q = _mm(down, wq, rows=N_TOK, pro="rms", g=g_q, out_dtype=F32, tn=1536)
    kv = _mm(down, wkv, rows=N_TOK, xcol=2, pro="rms", g=g_kv, out_dtype=BF16, tn=2048)
    kv_cache = _mm(cache_ckv.reshape(DEC_BATCH * PAST, MLA_KV_LORA).astype(BF16), wkv,
                   rows=DEC_BATCH * PAST, pro="cast", out_dtype=BF16, tn=2048)
    ckv_c = _rmsnorm(down, g_kv, rows=N_CTX, xcol=2)
    kr_c = down[:N_CTX, MLA_Q_LORA:MLA_Q_LORA + MLA_ROPE]

    npairs = MLA_HEADS // 2
    kr_col = MLA_Q_LORA // 128
    o_c = _attention(q, [dict(k=(kv, 0, 0), v=(kv, 0, npairs), kr=(down, 0, kr_col, 128), lk=SEQ)],
                     nb=BATCH, lq=SEQ, tq=SEQ, q_row0=0, q_col=0, qr=(q, 2 * npairs), scale=MLA_SCALE)
    tabs = _rope_tables()
    kr_l = _rope_keys(down, tabs)
    o_l = _attention(
        q,
        [dict(k=(kv, N_CTX, 0), v=(kv, N_CTX, npairs), kr=(kr_l, 0, 0, 128), lk=DEC_SEQ),
         dict(k=(kv_cache, 0, 0), v=(kv_cache, 0, npairs),
              kr=(cache_kr.reshape(DEC_BATCH * PAST, MLA_ROPE), 0, 0, MLA_ROPE), lk=PAST)],
        nb=DEC_BATCH, lq=DEC_SEQ, tq=512, q_row0=N_CTX, q_col=0, qr=(q, 2 * npairs), scale=MLA_SCALE,
        rope_tabs=tabs)
    x = _mm(o_c, w_o.astype(BF16), rows=N_TOK, pro="cast", x2=o_l, epi="resid", res=x, mod=mod, layer=layer,
            jmod=1)
    return x, ckv_c.reshape(BATCH, SEQ, MLA_KV_LORA), kr_c.reshape(BATCH, SEQ, MLA_ROPE)


def _mixer_gla(x, mod, layer, g, w_in, w_gd, w_gu, b_g, g_norm, w_o, state):
    w_cat = jnp.concatenate([w_in, w_gd[0], w_gd[1], jnp.zeros((D, 128 - 2 * GLA_RANK), F32)], axis=1).astype(BF16)
    proj = _mm(x, w_cat, rows=N_TOK, pro="ada", g=g, mod=mod, layer=layer, jmod=1, out_dtype=F32, tn=896)
    wgu = jnp.zeros((2, 128, GLA_HEADS * GLA_DK), F32)
    wgu = wgu.at[0, :GLA_RANK].set(w_gu[0]).at[1, GLA_RANK:2 * GLA_RANK].set(w_gu[1]).astype(BF16)
    bg = b_g.reshape(2, 1, GLA_HEADS * GLA_DK)
    o_f, o_b, st_c = _gla_scan(proj, wgu, bg, jnp.swapaxes(state, -1, -2))
    x = _mm(o_f, w_o.astype(BF16), rows=N_TOK, pro="gla", x2=o_b, gin=proj, gin_col=2, g=g_norm,
            epi="resid", res=x, mod=mod, layer=layer, jmod=1, tm=256, tn=2048)
    return x, jnp.swapaxes(st_c, -1, -2)


def kernel(x_prompt, x_sample, cache_na_k, cache_na_v, cache_mla_ckv, cache_mla_krope, state_gla, c, c_ctx, norm_g, w_ada, b_ada, w_ffn_in, w_ffn_out, w_na_qkv, w_na_o, na_rpb, w_mla_down, g_mla_q, w_mla_uq, g_mla_kv, w_mla_ukv, w_mla_o, w_gla_in, w_gla_gate_down, w_gla_gate_up, b_gla_gate, g_gla_norm, w_gla_o, final_norm_g):
    x = jnp.concatenate([x_prompt.reshape(N_CTX, D), x_sample.reshape(N_LAT, D)], axis=0)
    cond = jnp.concatenate([c_ctx[None], c, jnp.zeros((N_GROUPS - 1 - DEC_BATCH, D), F32)], axis=0)
    mod = _ada_mod(cond, w_ada, b_ada)
    w_in = w_ffn_in.astype(BF16)
    w_out = w_ffn_out.astype(BF16)

    na_k, na_v, ckv, krope, gla_st = [], [], [], [], []
    for i in range(DEPTH):
        kind, slot = i % 3, i // 3
        x = _ffn(x, mod, i, 0, w_in, w_out, norm_g[i, 0])
        if kind == 0:
            x, k_c, v_c = _mixer_na(x, mod, i, norm_g[i, 1], w_na_qkv[slot], w_na_o[slot], na_rpb[slot],
                                    cache_na_k[:, slot], cache_na_v[:, slot])
            na_k.append(k_c)
            na_v.append(v_c)
        elif kind == 1:
            x, ckv_c, kr_c = _mixer_mla(x, mod, i, norm_g[i, 1], w_mla_down[slot], g_mla_q[slot], w_mla_uq[slot],
                                        g_mla_kv[slot], w_mla_ukv[slot], w_mla_o[slot],
                                        cache_mla_ckv[:, slot], cache_mla_krope[:, slot])
            ckv.append(ckv_c)
            krope.append(kr_c)
        else:
            x, st = _mixer_gla(x, mod, i, norm_g[i, 1], w_gla_in[slot], w_gla_gate_down[slot],
                               w_gla_gate_up[slot], b_gla_gate[slot], g_gla_norm[slot], w_gla_o[slot],
                               state_gla[:, slot])
            gla_st.append(st)
        x = _ffn(x, mod, i, 1, w_in, w_out, norm_g[i, 2])

    y_prompt = _rmsnorm(x, final_norm_g, rows=N_CTX).reshape(BATCH, SEQ, D)
    y_sample = _rmsnorm(x, final_norm_g, rows=N_LAT, row_off=N_CTX // 512).reshape(DEC_BATCH, DEC_SEQ, D)
    return (y_prompt, y_sample, jnp.stack(na_k, axis=1), jnp.stack(na_v, axis=1), jnp.stack(ckv, axis=1),
            jnp.stack(krope, axis=1), jnp.stack(gla_st, axis=1))
```

```python
import functools

import jax
import jax.numpy as jnp
from jax import lax
from jax.experimental import pallas as pl
from jax.experimental.pallas import tpu as pltpu

F32 = jnp.float32
BF16 = jnp.bfloat16

D = 2048
BATCH, SEQ = 16, 256
DEC_BATCH, DEC_SEQ = 4, 2048
PAST = 512
DEPTH = 4
N_MOD = 9
EPS = 1e-6
D_FF = 5632
GRID_W = 64
LOG2_GRID_W = 6
NEG_INF = -1e30
LOG2E = 1.4426950408889634

N_CTX = BATCH * SEQ
N_LAT = DEC_BATCH * DEC_SEQ
N_TOK = N_CTX + N_LAT
N_GROUPS = 8

NA_HEADS, NA_DH = 16, 128
NA_WIN_ROWS, NA_WIN_COLS = 8, 16
LAT_ROWS = DEC_SEQ // GRID_W
NA_QR = 8
NA_KR = NA_QR + NA_WIN_ROWS

MLA_HEADS = 16
MLA_Q_LORA, MLA_KV_LORA = 768, 512
MLA_NOPE, MLA_ROPE, MLA_V = 128, 64, 128
MLA_SCALE = (MLA_NOPE + MLA_ROPE) ** -0.5
ROPE_THETA = 10000.0
MLA_DOWN_N = 1536

GLA_HEADS, GLA_DK, GLA_DV = 4, 256, 512
GLA_RANK = 16
GLA_TAU = 16.0
GLA_CHUNK = 128
GLA_PROJ_N = 2 * GLA_HEADS * GLA_DK + 2 * GLA_HEADS * GLA_DV + 128
GLA_CTX_CHUNKS = SEQ // GLA_CHUNK
GLA_LAT_CHUNKS = DEC_SEQ // GLA_CHUNK
GLA_CTX_STEPS = BATCH * GLA_CTX_CHUNKS
GLA_STEPS = GLA_CTX_STEPS + DEC_BATCH * GLA_LAT_CHUNKS

VMEM_LIMIT = 56 * 1024 * 1024
ROW_CHUNK = 16
CTX_PPS = 8
FFN_TF = 512
FFN_SUB = 512


def _cparams(sem):
    return pltpu.CompilerParams(dimension_semantics=sem, vmem_limit_bytes=VMEM_LIMIT)


def _group_of_row(r0):
    return jnp.where(r0 < N_CTX, 0, 1 + (r0 - N_CTX) // DEC_SEQ)


def _silu(x):
    return x / (1.0 + jnp.exp(-x))


def _rms(x):
    return x * lax.rsqrt(jnp.mean(x * x, axis=-1, keepdims=True) + EPS)


def _for_row_chunks(n_rows, body):
    def step(c, carry):
        body(pl.ds(pl.multiple_of(c * ROW_CHUNK, ROW_CHUNK), ROW_CHUNK))
        return carry
    lax.fori_loop(0, n_rows // ROW_CHUNK, step, 0, unroll=8)


def _dot(a, b):
    return jnp.dot(a, b, preferred_element_type=F32)


def _dot_nt(a, b):
    return lax.dot_general(a, b, (((1,), (1,)), ((), ())), preferred_element_type=F32)


def _dot_tn(a, b):
    return lax.dot_general(a, b, (((0,), (0,)), ((), ())), preferred_element_type=F32)


def _ada_kernel(c_ref, w_ref, b_ref, o_ref):
    s = _silu(c_ref[...]).astype(BF16)
    o_ref[...] = _dot(s, w_ref[...].astype(BF16)) + b_ref[...]


def _ada_mod(cond, w_ada, b_ada):
    tn = 1024
    n = N_MOD * D
    out = pl.pallas_call(
        _ada_kernel,
        out_shape=jax.ShapeDtypeStruct((DEPTH, N_GROUPS, n), F32),
        grid=(DEPTH, n // tn),
        in_specs=[
            pl.BlockSpec((N_GROUPS, D), lambda l, j: (0, 0)),
            pl.BlockSpec((None, D, tn), lambda l, j: (l, 0, j)),
            pl.BlockSpec((None, 1, tn), lambda l, j: (l, 0, j)),
        ],
        out_specs=pl.BlockSpec((None, N_GROUPS, tn), lambda l, j: (l, 0, j)),
        compiler_params=_cparams(("parallel", "parallel")),
        name="ada_mod",
    )(cond, w_ada, b_ada.reshape(DEPTH, 1, n))
    return out.reshape(DEPTH, N_GROUPS, N_MOD, 1, D)


def _mod_spec(layer, j, tm, row_off):
    return pl.BlockSpec(
        (None, None, None, 1, D),
        lambda i, n: (layer, _group_of_row((i + row_off) * tm), j, 0, 0))


def _ffn_kernel(x_ref, g_ref, sh_ref, sc_ref, gt_ref, wg_ref, wu_ref, wo_ref, o_ref, h_ref, gm_ref):
    f = pl.program_id(1)
    tm = x_ref.shape[0]

    @pl.when(f == 0)
    def _():
        gm_ref[...] = g_ref[...] * (1.0 + sc_ref[...])

        def rows_fn(rows):
            h_ref[rows, :] = (_rms(x_ref[rows, :]) * gm_ref[...] + sh_ref[...]).astype(BF16)
            o_ref[rows, :] = jnp.zeros((ROW_CHUNK, D), F32)

        _for_row_chunks(tm, rows_fn)

    for r in range(0, tm, FFN_SUB):
        h = h_ref[r:r + FFN_SUB, :]
        a = _silu(_dot(h, wg_ref[...])) * _dot(h, wu_ref[...])
        o_ref[r:r + FFN_SUB, :] += _dot(a.astype(BF16), wo_ref[...])

    @pl.when(f == pl.num_programs(1) - 1)
    def _():
        o_ref[...] = x_ref[...] + 0.5 * gt_ref[...] * o_ref[...]


def _ffn(x, mod, layer, half, w_in, w_out, g, *, tm=1024):
    tf = FFN_TF
    nf = D_FF // tf
    j = 2 * half
    return pl.pallas_call(
        _ffn_kernel,
        out_shape=jax.ShapeDtypeStruct((N_TOK, D), F32),
        grid=(N_TOK // tm, nf),
        in_specs=[
            pl.BlockSpec((tm, D), lambda i, f: (i, 0)),
            pl.BlockSpec((1, D), lambda i, f: (0, 0)),
            _mod_spec(layer, 3 * j, tm, 0),
            _mod_spec(layer, 3 * j + 1, tm, 0),
            _mod_spec(layer, 3 * j + 2, tm, 0),
            pl.BlockSpec((None, None, D, tf), lambda i, f: (layer, half, 0, f)),
            pl.BlockSpec((None, None, D, tf), lambda i, f: (layer, half, 0, nf + f)),
            pl.BlockSpec((None, None, tf, D), lambda i, f: (layer, half, f, 0)),
        ],
        out_specs=pl.BlockSpec((tm, D), lambda i, f: (i, 0)),
        scratch_shapes=[pltpu.VMEM((tm, D), BF16), pltpu.VMEM((1, D), F32)],
        compiler_params=_cparams(("parallel", "arbitrary")),
        name="ffn_half",
    )(x, g.reshape(1, D), mod, mod, mod, w_in, w_in, w_out)


def _mm_kernel(*refs, pro, epi, split):
    it = iter(refs)
    x_ref = next(it)
    if pro == "cast" and split is not None:
        x2_ref = next(it)
    if pro == "gla":
        x2_ref, gin_ref = next(it), next(it)
    if pro in ("ada", "rms", "gla"):
        g_ref = next(it)
    if pro == "ada":
        sh_ref, sc_ref = next(it), next(it)
    w_ref = next(it)
    if epi == "resid":
        res_ref, gt_ref = next(it), next(it)
    o_ref = next(it)
    if pro != "cast":
        h_ref = next(it)
    if pro == "ada":
        gm_ref = next(it)

    if pro != "cast":
        @pl.when(pl.program_id(1) == 0)
        def _():
            if pro == "ada":
                gm_ref[...] = g_ref[...] * (1.0 + sc_ref[...])

            def rows_fn(rows):
                if pro == "ada":
                    h = _rms(x_ref[rows, :]) * gm_ref[...] + sh_ref[...]
                elif pro == "rms":
                    h = _rms(x_ref[rows, :]) * g_ref[...]
                else:
                    o = x_ref[rows, :] + x2_ref[rows, :]
                    parts = [_rms(o[:, k * GLA_DV:(k + 1) * GLA_DV]) * g_ref[...] for k in range(GLA_HEADS)]
                    h = jnp.concatenate(parts, axis=-1) * _silu(gin_ref[rows, :])
                h_ref[rows, :] = h.astype(BF16)

            _for_row_chunks(x_ref.shape[0], rows_fn)

    def finish(lhs_ref):
        y = _dot(lhs_ref[...], w_ref[...])
        if epi == "resid":
            y = res_ref[...] + gt_ref[...] * y
        o_ref[...] = y.astype(o_ref.dtype)

    if pro != "cast":
        finish(h_ref)
    elif split is None:
        finish(x_ref)
    else:
        pl.when(pl.program_id(0) < split)(lambda: finish(x_ref))
        pl.when(pl.program_id(0) >= split)(lambda: finish(x2_ref))


def _mm(x, w, *, rows, row_off=0, xcol=0, wcol=0, n_out=None, pro, epi="plain", out_dtype=F32, tm=1024, tn=1024,
        g=None, mod=None, layer=None, jmod=None, x2=None, gin=None, gin_col=0, res=None):
    kdim = w.shape[0]
    n = n_out or w.shape[1]
    tn = min(tn, n)
    assert rows % tm == 0 and n % tn == 0
    split = None
    if pro == "cast" and x2 is not None:
        split = x.shape[0] // tm
        xspec = pl.BlockSpec((tm, kdim), lambda i, j: (jnp.minimum(i, split - 1), 0))
        args, specs = [x, x2], [xspec, pl.BlockSpec((tm, kdim), lambda i, j: (jnp.maximum(i - split, 0), 0))]
    else:
        xspec = pl.BlockSpec((tm, kdim), lambda i, j: (i + row_off, xcol))
        args, specs = [x], [xspec]
    if pro == "gla":
        args += [x2, gin]
        specs += [xspec, pl.BlockSpec((tm, kdim), lambda i, j: (i + row_off, gin_col))]
    if pro in ("ada", "rms", "gla"):
        args.append(g.reshape(1, -1))
        specs.append(pl.BlockSpec((1, g.shape[-1]), lambda i, j: (0, 0)))
    if pro == "ada":
        args += [mod, mod]
        specs += [_mod_spec(layer, 3 * jmod, tm, row_off), _mod_spec(layer, 3 * jmod + 1, tm, row_off)]
    args.append(w)
    specs.append(pl.BlockSpec((kdim, tn), lambda i, j: (0, wcol * (n // tn) + j)))
    if epi == "resid":
        args += [res, mod]
        specs += [
            pl.BlockSpec((tm, tn), lambda i, j: (i + row_off, j)),
            pl.BlockSpec((None, None, None, 1, tn),
                         lambda i, j: (layer, _group_of_row((i + row_off) * tm), 3 * jmod + 2, 0, j)),
        ]
    scratch = []
    if pro != "cast":
        scratch.append(pltpu.VMEM((tm, kdim), BF16))
    if pro == "ada":
        scratch.append(pltpu.VMEM((1, kdim), F32))
    return pl.pallas_call(
        functools.partial(_mm_kernel, pro=pro, epi=epi, split=split),
        out_shape=jax.ShapeDtypeStruct((rows, n), out_dtype),
        grid=(rows // tm, n // tn),
        in_specs=specs,
        out_specs=pl.BlockSpec((tm, tn), lambda i, j: (i, j)),
        scratch_shapes=scratch,
        compiler_params=_cparams(("parallel", "arbitrary")),
        name="proj_" + pro + "_" + epi,
    )(*args)


def _rmsnorm_kernel(x_ref, g_ref, o_ref):
    o_ref[...] = _rms(x_ref[...]) * g_ref[...]


def _rmsnorm(x, g, *, rows, row_off=0, xcol=0, tm=512):
    width = g.shape[-1]
    return pl.pallas_call(
        _rmsnorm_kernel,
        out_shape=jax.ShapeDtypeStruct((rows, width), F32),
        grid=(rows // tm,),
        in_specs=[pl.BlockSpec((tm, width), lambda i: (i + row_off, xcol)),
                  pl.BlockSpec((1, width), lambda i: (0, 0))],
        out_specs=pl.BlockSpec((tm, width), lambda i: (i, 0)),
        compiler_params=_cparams(("parallel",)),
        name="rmsnorm",
    )(x, g.reshape(1, width))


def _rope(x, cos, sin):
    width = x.shape[-1]
    lane = lax.broadcasted_iota(jnp.int32, x.shape, 1)
    up = pltpu.roll(x, width - 16, 1)
    down = pltpu.roll(x, 16, 1)
    swapped = jnp.where((lane & 31) < 16, up, down)
    return x * cos + swapped * sin


def _rope_keys_kernel(x_ref, cos_ref, sin_ref, o_ref):
    o_ref[...] = _rope(x_ref[...], cos_ref[...], sin_ref[...]).astype(o_ref.dtype)


def _rope_keys(down, rope_tabs, *, tm=1024):
    cos, sin = rope_tabs
    per_seq = DEC_SEQ // tm
    return pl.pallas_call(
        _rope_keys_kernel,
        out_shape=jax.ShapeDtypeStruct((N_LAT, 128), BF16),
        grid=(N_LAT // tm,),
        in_specs=[pl.BlockSpec((tm, 128), lambda i: (N_CTX // tm + i, MLA_Q_LORA // 128)),
                  pl.BlockSpec((tm, 128), lambda i: (i % per_seq, 0)),
                  pl.BlockSpec((tm, 128), lambda i: (i % per_seq, 0))],
        out_specs=pl.BlockSpec((tm, 128), lambda i: (i, 0)),
        compiler_params=_cparams(("parallel",)),
        name="rope_keys",
    )(down, cos, sin)


def _attn_kernel(*refs, nseg, has_r, rope, scale, dh, dv, pps):
    it = iter(refs)
    q_ref = next(it)
    qr_ref = next(it) if has_r else None
    segs = []
    for _ in range(nseg):
        k_ref = next(it)
        kr_ref = next(it) if has_r else None
        v_ref = next(it)
        segs.append((k_ref, kr_ref, v_ref))
    if rope:
        cq_ref, sq_ref = next(it), next(it)
    o_ref = next(it)

    if has_r:
        qr = qr_ref[...].astype(F32)
        if rope:
            qr = _rope(qr, cq_ref[...], sq_ref[...])
        qrs = [qr[:, h * MLA_ROPE:(h + 1) * MLA_ROPE].astype(BF16) for h in range(2 * pps)]
        krs = [kr_ref[:, :MLA_ROPE].astype(BF16) for _, kr_ref, _ in segs]

    for pair in range(pps):
        scores = []
        for h in (2 * pair, 2 * pair + 1):
            q = q_ref[:, h * dh:(h + 1) * dh].astype(BF16)
            ss = []
            for si, (k_ref, _, _) in enumerate(segs):
                s = _dot_nt(q, k_ref[:, h * dh:(h + 1) * dh].astype(BF16))
                if has_r:
                    s = s + _dot_nt(qrs[h], krs[si])
                ss.append(s * (scale * LOG2E))
            scores.append(ss)
        for h, ss in zip((2 * pair, 2 * pair + 1), scores):
            m = ss[0].max(axis=-1, keepdims=True)
            for s in ss[1:]:
                m = jnp.maximum(m, s.max(axis=-1, keepdims=True))
            acc = None
            den = None
            for s, (_, _, v_ref) in zip(ss, segs):
                e = jnp.exp2(s - m)
                d = e.sum(axis=-1, keepdims=True)
                pv = _dot(e.astype(BF16), v_ref[:, h * dv:(h + 1) * dv].astype(BF16))
                acc = pv if acc is None else acc + pv
                den = d if den is None else den + d
            o_ref[:, h * dv:(h + 1) * dv] = (acc / den).astype(o_ref.dtype)


def _attention(q, segs, *, nb, lq, tq, q_row0, q_col, scale, qr=None, rope_tabs=None, pps=1, dh=128, dv=128):
    npairs = 8
    ng = npairs // pps
    nq = lq // tq
    has_r = qr is not None
    q_blk0 = q_row0 // tq
    assert q_col % pps == 0 and (rope_tabs is None or pps == 1)
    args = [q]
    specs = [pl.BlockSpec((tq, 2 * dh * pps), lambda b, p, t: (q_blk0 + b * nq + t, q_col // pps + p))]
    if has_r:
        qr_arr, qr_col = qr
        assert qr_col % pps == 0
        args.append(qr_arr)
        specs.append(pl.BlockSpec((tq, 2 * MLA_ROPE * pps),
                                  lambda b, p, t: (q_blk0 + b * nq + t, qr_col // pps + p)))
    for sg in segs:
        lk = sg["lk"]
        k_arr, k_row0, k_col = sg["k"]
        assert k_col % pps == 0
        args.append(k_arr)
        specs.append(pl.BlockSpec((lk, 2 * dh * pps), functools.partial(
            lambda b, p, t, r0, c0: (r0 + b, c0 + p), r0=k_row0 // lk, c0=k_col // pps)))
        if has_r:
            kr_arr, kr_row0, kr_col, kr_w = sg["kr"]
            args.append(kr_arr)
            specs.append(pl.BlockSpec((lk, kr_w), functools.partial(
                lambda b, p, t, r0, c0: (r0 + b, c0), r0=kr_row0 // lk, c0=kr_col)))
        v_arr, v_row0, v_col = sg["v"]
        assert v_col % pps == 0
        args.append(v_arr)
        specs.append(pl.BlockSpec((lk, 2 * dv * pps), functools.partial(
            lambda b, p, t, r0, c0: (r0 + b, c0 + p), r0=v_row0 // lk, c0=v_col // pps)))
    if rope_tabs is not None:
        cos, sin = rope_tabs
        args += [cos, sin]
        specs += [pl.BlockSpec((tq, 128), lambda b, p, t: (t, 0)),
                  pl.BlockSpec((tq, 128), lambda b, p, t: (t, 0))]
    return pl.pallas_call(
        functools.partial(_attn_kernel, nseg=len(segs), has_r=has_r, rope=rope_tabs is not None,
                          scale=scale, dh=dh, dv=dv, pps=pps),
        out_shape=jax.ShapeDtypeStruct((nb * lq, npairs * 2 * dv), BF16),
        grid=(nb, ng, nq),
        in_specs=specs,
        out_specs=pl.BlockSpec((tq, 2 * dv * pps), lambda b, p, t: (b * nq + t, p)),
        compiler_params=_cparams(("parallel", "parallel", "arbitrary")),
        name="attention",
    )(*args)


def _na_kernel(q_ref, k_ref, v_ref, kc_ref, vc_ref, t2_ref, o_ref, bias_ref, cap_ref):
    r0 = pl.program_id(1) * NA_QR
    ks = jnp.clip(r0 - NA_WIN_ROWS // 2, 0, LAT_ROWS - NA_KR)
    nq, nk = NA_QR * GRID_W, NA_KR * GRID_W
    scale = NA_DH ** -0.5 * LOG2E

    @pl.when(pl.program_id(2) == 0)
    def _():
        row = lax.broadcasted_iota(jnp.int32, (nq, nk), 0)
        lane = lax.broadcasted_iota(jnp.int32, (nq, nk), 1)
        qc = row & (GRID_W - 1)
        kc = lane & (GRID_W - 1)
        rs = jnp.clip(r0 + (row >> LOG2_GRID_W) - NA_WIN_ROWS // 2, 0, LAT_ROWS - NA_WIN_ROWS)
        kr = ks + (lane >> LOG2_GRID_W)
        cs = jnp.clip(qc - NA_WIN_COLS // 2, 0, GRID_W - NA_WIN_COLS)
        ok = (kr >= rs) & (kr < rs + NA_WIN_ROWS) & (kc >= cs) & (kc < cs + NA_WIN_COLS)
        cap_ref[...] = jnp.where(ok, jnp.inf, NEG_INF)
        for hh in range(2):
            for i in range(NA_QR):
                for jp in range(NA_KR // 2):
                    e = jnp.clip(ks - r0 + 2 * jp - i + NA_WIN_ROWS, 0, 2 * NA_WIN_ROWS - 1)
                    bias_ref[hh, i * GRID_W:(i + 1) * GRID_W, jp * 128:(jp + 1) * 128] = t2_ref[hh, e] * LOG2E

    start = pl.multiple_of(ks * GRID_W, GRID_W)
    kw = k_ref[pl.ds(start, nk), :]
    vw = v_ref[pl.ds(start, nk), :]
    cap = cap_ref[...]
    for hh in range(2):
        sl = slice(hh * NA_DH, (hh + 1) * NA_DH)
        q = q_ref[:, sl]
        s_loc = jnp.minimum(_dot_nt(q, kw[:, sl]) * scale + bias_ref[hh], cap)
        s_ctx = _dot_nt(q, kc_ref[:, sl]) * scale
        m = jnp.maximum(s_loc.max(axis=-1, keepdims=True), s_ctx.max(axis=-1, keepdims=True))
        e_loc = jnp.exp2(s_loc - m)
        e_ctx = jnp.exp2(s_ctx - m)
        den = e_loc.sum(axis=-1, keepdims=True) + e_ctx.sum(axis=-1, keepdims=True)
        acc = _dot(e_loc.astype(BF16), vw[:, sl]) + _dot(e_ctx.astype(BF16), vc_ref[:, sl])
        o_ref[:, sl] = (acc / den).astype(o_ref.dtype)


def _na_bias_table(rpb):
    col = jnp.arange(GRID_W)
    col_off = jnp.clip(col[None, :] - col[:, None] + NA_WIN_COLS - 1, 0, 2 * NA_WIN_COLS - 2)
    e = jnp.arange(2 * NA_WIN_ROWS)
    dr = jnp.clip(jnp.stack([e - 1, e], axis=1), 0, 2 * NA_WIN_ROWS - 2)
    tb = rpb[:, dr][:, :, :, col_off]
    return jnp.transpose(tb, (0, 1, 3, 2, 4)).reshape(NA_HEADS, 2 * NA_WIN_ROWS, GRID_W, 2 * GRID_W)


def _na_latent(qkv_l, kc, vc, t2):
    npairs = NA_HEADS // 2
    w2 = 2 * NA_DH
    nrb = LAT_ROWS // NA_QR
    nq = NA_QR * GRID_W
    return pl.pallas_call(
        _na_kernel,
        out_shape=jax.ShapeDtypeStruct((N_LAT, D), BF16),
        grid=(npairs, nrb, DEC_BATCH),
        in_specs=[
            pl.BlockSpec((nq, w2), lambda p, r, b: (b * nrb + r, p)),
            pl.BlockSpec((DEC_SEQ, w2), lambda p, r, b: (b, npairs + p)),
            pl.BlockSpec((DEC_SEQ, w2), lambda p, r, b: (b, 2 * npairs + p)),
            pl.BlockSpec((PAST, w2), lambda p, r, b: (b, p)),
            pl.BlockSpec((PAST, w2), lambda p, r, b: (b, p)),
            pl.BlockSpec((2, 2 * NA_WIN_ROWS, GRID_W, 2 * GRID_W), lambda p, r, b: (p, 0, 0, 0)),
        ],
        out_specs=pl.BlockSpec((nq, w2), lambda p, r, b: (b * nrb + r, p)),
        scratch_shapes=[pltpu.VMEM((2, nq, NA_KR * GRID_W), F32), pltpu.VMEM((nq, NA_KR * GRID_W), F32)],
        compiler_params=_cparams(("parallel", "parallel", "arbitrary")),
        name="na_latent",
    )(qkv_l, qkv_l, qkv_l, kc, vc, t2)


def _gla_pos(t):
    is_ctx = t < GLA_CTX_STEPS
    u = jnp.maximum(t - GLA_CTX_STEPS, 0)
    seq = jnp.where(is_ctx, t // GLA_CTX_CHUNKS, u // GLA_LAT_CHUNKS)
    n = jnp.where(is_ctx, t % GLA_CTX_CHUNKS, u % GLA_LAT_CHUNKS)
    return is_ctx, seq, n


def _gla_bwd_chunk(t):
    is_ctx, _, n = _gla_pos(t)
    return t + jnp.where(is_ctx, GLA_CTX_CHUNKS, GLA_LAT_CHUNKS) - 1 - 2 * n


def _gla_kernel(qf_ref, kf_ref, vf_ref, gdf_ref, qb_ref, kb_ref, vb_ref, gdb_ref, wgu_ref, bg_ref, s0_ref,
                of_ref, ob_ref, fin_ref, st_ref):
    is_ctx, _, n = _gla_pos(pl.program_id(0))
    is_lat = jnp.logical_not(is_ctx)

    @pl.when(jnp.logical_and(n == 0, is_ctx))
    def _():
        st_ref[...] = jnp.zeros_like(st_ref)

    @pl.when(jnp.logical_and(n == 0, is_lat))
    def _():
        st_ref[...] = s0_ref[...]

    ri = lax.broadcasted_iota(jnp.int32, (GLA_CHUNK, GLA_CHUNK), 0)
    ci = lax.broadcasted_iota(jnp.int32, (GLA_CHUNK, GLA_CHUNK), 1)
    streams = ((qf_ref, kf_ref, vf_ref, gdf_ref, of_ref), (qb_ref, kb_ref, vb_ref, gdb_ref, ob_ref))
    for d, (q_ref, k_ref, v_ref, gd_ref, o_ref) in enumerate(streams):
        keep = (ci <= ri) if d == 0 else (ci >= ri)
        tri = jnp.where(keep, 1.0, 0.0).astype(BF16)
        pre = _dot(gd_ref[...].astype(BF16), wgu_ref[d]) + bg_ref[d]
        la = (jnp.minimum(pre, 0.0) - jnp.log1p(jnp.exp(-jnp.abs(pre)))) / GLA_TAU
        hi = la.astype(BF16)
        r1 = la - hi.astype(F32)
        mid = r1.astype(BF16)
        lo = (r1 - mid.astype(F32)).astype(BF16)
        cum = _dot(tri, hi) + _dot(tri, mid) + _dot(tri, lo)
        tot = jnp.sum(la, axis=0, keepdims=True)
        k = k_ref[...]
        q_dec = (q_ref[...] * (GLA_DK ** -0.5) * jnp.exp(cum)).astype(BF16)
        k_inv = (k * jnp.exp(-cum)).astype(BF16)
        k_end = (k * jnp.exp(tot - cum)).astype(BF16)
        dec = jnp.exp(tot)
        v = v_ref[...].astype(BF16)
        for h in range(GLA_HEADS):
            ks = slice(h * GLA_DK, (h + 1) * GLA_DK)
            vs = slice(h * GLA_DV, (h + 1) * GLA_DV)
            att = jnp.where(keep, _dot_nt(q_dec[:, ks], k_inv[:, ks]), 0.0)
            st = st_ref[d, h]
            o_ref[:, vs] = _dot(att.astype(BF16), v[:, vs]) + _dot_nt(q_dec[:, ks], st.astype(BF16))
            st_ref[d, h] = st * dec[:, ks] + _dot_tn(v[:, vs], k_end[:, ks])

    @pl.when(jnp.logical_and(is_ctx, n == GLA_CTX_CHUNKS - 1))
    def _():
        fin_ref[...] = st_ref[...]


def _gla_scan(proj, wgu, bg, s0t):
    hk = GLA_HEADS * GLA_DK
    hv = GLA_HEADS * GLA_DV
    gd_col = (GLA_PROJ_N - 128) // 128

    def fwd(c):
        return lambda t: (t, c)

    def bwd(c):
        return lambda t: (_gla_bwd_chunk(t), c)

    in_specs = []
    for ix in (fwd, bwd):
        in_specs += [pl.BlockSpec((GLA_CHUNK, hk), ix(0)), pl.BlockSpec((GLA_CHUNK, hk), ix(1)),
                     pl.BlockSpec((GLA_CHUNK, hv), ix(1)), pl.BlockSpec((GLA_CHUNK, 128), ix(gd_col))]
    st_block = (None, 2, GLA_HEADS, GLA_DV, GLA_DK)
    in_specs += [
        pl.BlockSpec((2, 128, hk), lambda t: (0, 0, 0)),
        pl.BlockSpec((2, 1, hk), lambda t: (0, 0, 0)),
        pl.BlockSpec(st_block, lambda t: (jnp.where(_gla_pos(t)[0], 0, _gla_pos(t)[1]), 0, 0, 0, 0)),
    ]
    o_shape = jax.ShapeDtypeStruct((N_TOK, hv), F32)
    return pl.pallas_call(
        _gla_kernel,
        out_shape=(o_shape, o_shape, jax.ShapeDtypeStruct((BATCH, 2, GLA_HEADS, GLA_DV, GLA_DK), F32)),
        grid=(GLA_STEPS,),
        in_specs=in_specs,
        out_specs=(
            pl.BlockSpec((GLA_CHUNK, hv), lambda t: (t, 0)),
            pl.BlockSpec((GLA_CHUNK, hv), lambda t: (_gla_bwd_chunk(t), 0)),
            pl.BlockSpec(st_block, lambda t: (jnp.where(_gla_pos(t)[0], _gla_pos(t)[1], BATCH - 1), 0, 0, 0, 0)),
        ),
        scratch_shapes=[pltpu.VMEM((2, GLA_HEADS, GLA_DV, GLA_DK), F32)],
        compiler_params=_cparams(("arbitrary",)),
        name="gla_scan",
    )(*([proj] * 8), wgu, bg, s0t)


def _rope_tables():
    t = jnp.arange(DEC_SEQ)
    d = MLA_ROPE // 2
    inv = ROPE_THETA ** (-jnp.arange(0, d, 2, dtype=F32) / d)
    ang_r = (t // GRID_W).astype(F32)[:, None] * inv[None]
    ang_c = (t % GRID_W).astype(F32)[:, None] * inv[None]
    cos = jnp.concatenate([jnp.cos(ang_r)] * 2 + [jnp.cos(ang_c)] * 2, axis=-1)
    sin = jnp.concatenate([-jnp.sin(ang_r), jnp.sin(ang_r), -jnp.sin(ang_c), jnp.sin(ang_c)], axis=-1)
    return jnp.concatenate([cos, cos], axis=-1), jnp.concatenate([sin, sin], axis=-1)


def _mixer_na(x, mod, layer, g, w_qkv, w_o, rpb, cache_k, cache_v):
    w_qkv = w_qkv.astype(BF16)
    ada = dict(pro="ada", g=g, mod=mod, layer=layer, jmod=1)
    q_c, k_c, v_c = (_mm(x, w_qkv, rows=N_CTX, wcol=c, n_out=D, out_dtype=F32, **ada) for c in range(3))
    qkv_l = _mm(x, w_qkv, rows=N_LAT, row_off=N_CTX // 1024, out_dtype=BF16, tn=1536, **ada)
    o_c = _attention(q_c, [dict(k=(k_c, 0, 0), v=(v_c, 0, 0), lk=SEQ)],
                     nb=BATCH, lq=SEQ, tq=SEQ, q_row0=0, q_col=0, scale=NA_DH ** -0.5, pps=CTX_PPS)
    kc = cache_k.reshape(DEC_BATCH * PAST, D).astype(BF16)
    vc = cache_v.reshape(DEC_BATCH * PAST, D).astype(BF16)
    o_l = _na_latent(qkv_l, kc, vc, _na_bias_table(rpb))
    x = _mm(o_c, w_o.astype(BF16), rows=N_TOK, pro="cast", x2=o_l, epi="resid", res=x, mod=mod, layer=layer,
            jmod=1)
    return x, k_c.reshape(BATCH, SEQ, NA_HEADS, NA_DH), v_c.reshape(BATCH, SEQ, NA_HEADS, NA_DH)


def _mixer_mla(x, mod, layer, g, w_down, g_q, w_uq, g_kv, w_ukv, w_o, cache_ckv, cache_kr):
    hq = MLA_NOPE + MLA_ROPE
    wd = jnp.concatenate([
        w_down[:, :MLA_Q_LORA],
        w_down[:, MLA_Q_LORA + MLA_KV_LORA:],
        jnp.zeros((D, 256 - MLA_ROPE), F32),
        w_down[:, MLA_Q_LORA:MLA_Q_LORA + MLA_KV_LORA]], axis=1).astype(BF16)
    wq = w_uq.reshape(MLA_Q_LORA, MLA_HEADS, hq)
    wq = jnp.concatenate([wq[:, :, :MLA_NOPE].reshape(MLA_Q_LORA, -1),
                          wq[:, :, MLA_NOPE:].reshape(MLA_Q_LORA, -1)], axis=1).astype(BF16)
    wkv = w_ukv.reshape(MLA_KV_LORA, MLA_HEADS, MLA_NOPE + MLA_V)
    wkv = jnp.concatenate([wkv[:, :, :MLA_NOPE].reshape(MLA_KV_LORA, -1),
                           wkv[:, :, MLA_NOPE:].reshape(MLA_KV_LORA, -1)], axis=1).astype(BF16)

    down = _mm(x, wd, rows=N_TOK, pro="ada", g=g, mod=mod, layer=layer, jmod=1, out_dtype=F32, tn=MLA_DOWN_N)
    q = _mm(down, wq, rows=N_TOK, pro="rms", g=g_q, out_dtype=F32, tn=1536)
    kv = _mm(down, wkv, rows=N_TOK, xcol=2, pro="rms", g=g_kv, out_dtype=BF16, tn=2048)
    kv_cache = _mm(cache_ckv.reshape(DEC_BATCH * PAST, MLA_KV_LORA).astype(BF16), wkv,
                   rows=DEC_BATCH * PAST, pro="cast", out_dtype=BF16, tn=2048)
    ckv_c = _rmsnorm(down, g_kv, rows=N_CTX, xcol=2)
    kr_c = down[:N_CTX, MLA_Q_LORA:MLA_Q_LORA + MLA_ROPE]

    npairs = MLA_HEADS // 2
    kr_col = MLA_Q_LORA // 128
    o_c = _attention(q, [dict(k=(kv, 0, 0), v=(kv, 0, npairs), kr=(down, 0, kr_col, 128), lk=SEQ)],
                     nb=BATCH, lq=SEQ, tq=SEQ, q_row0=0, q_col=0, qr=(q, 2 * npairs), scale=MLA_SCALE, pps=CTX_PPS)
    tabs = _rope_tables()
    kr_l = _rope_keys(down, tabs)
    o_l = _attention(
        q,
        [dict(k=(kv, N_CTX, 0), v=(kv, N_CTX, npairs), kr=(kr_l, 0, 0, 128), lk=DEC_SEQ),
         dict(k=(kv_cache, 0, 0), v=(kv_cache, 0, npairs),
              kr=(cache_kr.reshape(DEC_BATCH * PAST, MLA_ROPE), 0, 0, MLA_ROPE), lk=PAST)],
        nb=DEC_BATCH, lq=DEC_SEQ, tq=512, q_row0=N_CTX, q_col=0, qr=(q, 2 * npairs), scale=MLA_SCALE,
        rope_tabs=tabs)
    x = _mm(o_c, w_o.astype(BF16), rows=N_TOK, pro="cast", x2=o_l, epi="resid", res=x, mod=mod, layer=layer,
            jmod=1)
    return x, ckv_c.reshape(BATCH, SEQ, MLA_KV_LORA), kr_c.reshape(BATCH, SEQ, MLA_ROPE)


def _mixer_gla(x, mod, layer, g, w_in, w_gd, w_gu, b_g, g_norm, w_o, state):
    w_cat = jnp.concatenate([w_in, w_gd[0], w_gd[1], jnp.zeros((D, 128 - 2 * GLA_RANK), F32)], axis=1).astype(BF16)
    proj = _mm(x, w_cat, rows=N_TOK, pro="ada", g=g, mod=mod, layer=layer, jmod=1, out_dtype=F32, tn=896)
    wgu = jnp.zeros((2, 128, GLA_HEADS * GLA_DK), F32)
    wgu = wgu.at[0, :GLA_RANK].set(w_gu[0]).at[1, GLA_RANK:2 * GLA_RANK].set(w_gu[1]).astype(BF16)
    bg = b_g.reshape(2, 1, GLA_HEADS * GLA_DK)
    o_f, o_b, st_c = _gla_scan(proj, wgu, bg, jnp.swapaxes(state, -1, -2))
    x = _mm(o_f, w_o.astype(BF16), rows=N_TOK, pro="gla", x2=o_b, gin=proj, gin_col=2, g=g_norm,
            epi="resid", res=x, mod=mod, layer=layer, jmod=1, tm=256, tn=2048)
    return x, jnp.swapaxes(st_c, -1, -2)


def kernel(x_prompt, x_sample, cache_na_k, cache_na_v, cache_mla_ckv, cache_mla_krope, state_gla, c, c_ctx, norm_g, w_ada, b_ada, w_ffn_in, w_ffn_out, w_na_qkv, w_na_o, na_rpb, w_mla_down, g_mla_q, w_mla_uq, g_mla_kv, w_mla_ukv, w_mla_o, w_gla_in, w_gla_gate_down, w_gla_gate_up, b_gla_gate, g_gla_norm, w_gla_o, final_norm_g):
    x = jnp.concatenate([x_prompt.reshape(N_CTX, D), x_sample.reshape(N_LAT, D)], axis=0)
    cond = jnp.concatenate([c_ctx[None], c, jnp.zeros((N_GROUPS - 1 - DEC_BATCH, D), F32)], axis=0)
    mod = _ada_mod(cond, w_ada, b_ada)
    w_in = w_ffn_in.astype(BF16)
    w_out = w_ffn_out.astype(BF16)

    na_k, na_v, ckv, krope, gla_st = [], [], [], [], []
    for i in range(DEPTH):
        kind, slot = i % 3, i // 3
        x = _ffn(x, mod, i, 0, w_in, w_out, norm_g[i, 0])
        if kind == 0:
            x, k_c, v_c = _mixer_na(x, mod, i, norm_g[i, 1], w_na_qkv[slot], w_na_o[slot], na_rpb[slot],
                                    cache_na_k[:, slot], cache_na_v[:, slot])
            na_k.append(k_c)
            na_v.append(v_c)
        elif kind == 1:
            x, ckv_c, kr_c = _mixer_mla(x, mod, i, norm_g[i, 1], w_mla_down[slot], g_mla_q[slot], w_mla_uq[slot],
                                        g_mla_kv[slot], w_mla_ukv[slot], w_mla_o[slot],
                                        cache_mla_ckv[:, slot], cache_mla_krope[:, slot])
            ckv.append(ckv_c)
            krope.append(kr_c)
        else:
            x, st = _mixer_gla(x, mod, i, norm_g[i, 1], w_gla_in[slot], w_gla_gate_down[slot],
                               w_gla_gate_up[slot], b_gla_gate[slot], g_gla_norm[slot], w_gla_o[slot],
                               state_gla[:, slot])
            gla_st.append(st)
        x = _ffn(x, mod, i, 1, w_in, w_out, norm_g[i, 2])

    y_prompt = _rmsnorm(x, final_norm_g, rows=N_CTX).reshape(BATCH, SEQ, D)
    y_sample = _rmsnorm(x, final_norm_g, rows=N_LAT, row_off=N_CTX // 512).reshape(DEC_BATCH, DEC_SEQ, D)
    return (y_prompt, y_sample, jnp.stack(na_k, axis=1), jnp.stack(na_v, axis=1), jnp.stack(ckv, axis=1),
            jnp.stack(krope, axis=1), jnp.stack(gla_st, axis=1))
```

```python
import functools

import jax
import jax.numpy as jnp
from jax import lax
from jax.experimental import pallas as pl
from jax.experimental.pallas import tpu as pltpu

F32 = jnp.float32
BF16 = jnp.bfloat16

D = 2048
BATCH, SEQ = 16, 256
DEC_BATCH, DEC_SEQ = 4, 2048
PAST = 512
DEPTH = 4
N_MOD = 9
EPS = 1e-6
D_FF = 5632
GRID_W = 64
LOG2_GRID_W = 6
NEG_INF = -1e30
LOG2E = 1.4426950408889634

N_CTX = BATCH * SEQ
N_LAT = DEC_BATCH * DEC_SEQ
N_TOK = N_CTX + N_LAT
N_GROUPS = 8

NA_HEADS, NA_DH = 16, 128
NA_WIN_ROWS, NA_WIN_COLS = 8, 16
LAT_ROWS = DEC_SEQ // GRID_W
NA_QR = 8
NA_KR = NA_QR + NA_WIN_ROWS

MLA_HEADS = 16
MLA_Q_LORA, MLA_KV_LORA = 768, 512
MLA_NOPE, MLA_ROPE, MLA_V = 128, 64, 128
MLA_SCALE = (MLA_NOPE + MLA_ROPE) ** -0.5
ROPE_THETA = 10000.0
MLA_DOWN_N = 1536

GLA_HEADS, GLA_DK, GLA_DV = 4, 256, 512
GLA_RANK = 16
GLA_TAU = 16.0
GLA_CHUNK = 128
GLA_PROJ_N = 2 * GLA_HEADS * GLA_DK + 2 * GLA_HEADS * GLA_DV + 128
GLA_CTX_CHUNKS = SEQ // GLA_CHUNK
GLA_LAT_CHUNKS = DEC_SEQ // GLA_CHUNK
GLA_CTX_STEPS = BATCH * GLA_CTX_CHUNKS
GLA_STEPS = GLA_CTX_STEPS + DEC_BATCH * GLA_LAT_CHUNKS

VMEM_LIMIT = 56 * 1024 * 1024
ROW_CHUNK = 16
CTX_PPS = 8
FFN_TF = 512
FFN_SUB = 512


def _cparams(sem):
    return pltpu.CompilerParams(dimension_semantics=sem, vmem_limit_bytes=VMEM_LIMIT)


def _group_of_row(r0):
    return jnp.where(r0 < N_CTX, 0, 1 + (r0 - N_CTX) // DEC_SEQ)


def _silu(x):
    return x / (1.0 + jnp.exp(-x))


def _rms(x):
    return x * lax.rsqrt(jnp.mean(x * x, axis=-1, keepdims=True) + EPS)


def _for_row_chunks(n_rows, body):
    def step(c, carry):
        body(pl.ds(pl.multiple_of(c * ROW_CHUNK, ROW_CHUNK), ROW_CHUNK))
        return carry
    lax.fori_loop(0, n_rows // ROW_CHUNK, step, 0, unroll=8)


def _dot(a, b):
    return jnp.dot(a, b, preferred_element_type=F32)


def _dot_nt(a, b):
    return lax.dot_general(a, b, (((1,), (1,)), ((), ())), preferred_element_type=F32)


def _dot_tn(a, b):
    return lax.dot_general(a, b, (((0,), (0,)), ((), ())), preferred_element_type=F32)


def _ada_kernel(c_ref, w_ref, b_ref, o_ref):
    s = _silu(c_ref[...]).astype(BF16)
    o_ref[...] = _dot(s, w_ref[...].astype(BF16)) + b_ref[...]


def _ada_mod(cond, w_ada, b_ada):
    tn = 1024
    n = N_MOD * D
    out = pl.pallas_call(
        _ada_kernel,
        out_shape=jax.ShapeDtypeStruct((DEPTH, N_GROUPS, n), F32),
        grid=(DEPTH, n // tn),
        in_specs=[
            pl.BlockSpec((N_GROUPS, D), lambda l, j: (0, 0)),
            pl.BlockSpec((None, D, tn), lambda l, j: (l, 0, j)),
            pl.BlockSpec((None, 1, tn), lambda l, j: (l, 0, j)),
        ],
        out_specs=pl.BlockSpec((None, N_GROUPS, tn), lambda l, j: (l, 0, j)),
        compiler_params=_cparams(("parallel", "parallel")),
        name="ada_mod",
    )(cond, w_ada, b_ada.reshape(DEPTH, 1, n))
    return out.reshape(DEPTH, N_GROUPS, N_MOD, 1, D)


def _mod_spec(layer, j, tm, row_off):
    return pl.BlockSpec(
        (None, None, None, 1, D),
        lambda i, n: (layer, _group_of_row((i + row_off) * tm), j, 0, 0))


def _ffn_kernel(x_ref, g_ref, sh_ref, sc_ref, gt_ref, wg_ref, wu_ref, wo_ref, o_ref, h_ref, gm_ref):
    f = pl.program_id(1)
    tm = x_ref.shape[0]

    @pl.when(f == 0)
    def _():
        gm_ref[...] = g_ref[...] * (1.0 + sc_ref[...])

        def rows_fn(rows):
            h_ref[rows, :] = (_rms(x_ref[rows, :]) * gm_ref[...] + sh_ref[...]).astype(BF16)
            o_ref[rows, :] = jnp.zeros((ROW_CHUNK, D), F32)

        _for_row_chunks(tm, rows_fn)

    for r in range(0, tm, FFN_SUB):
        h = h_ref[r:r + FFN_SUB, :]
        a = _silu(_dot(h, wg_ref[...])) * _dot(h, wu_ref[...])
        o_ref[r:r + FFN_SUB, :] += _dot(a.astype(BF16), wo_ref[...])

    @pl.when(f == pl.num_programs(1) - 1)
    def _():
        o_ref[...] = x_ref[...] + 0.5 * gt_ref[...] * o_ref[...]


def _ffn(x, mod, layer, half, w_in, w_out, g, *, tm=1024):
    tf = FFN_TF
    nf = D_FF // tf
    j = 2 * half
    return pl.pallas_call(
        _ffn_kernel,
        out_shape=jax.ShapeDtypeStruct((N_TOK, D), F32),
        grid=(N_TOK // tm, nf),
        in_specs=[
            pl.BlockSpec((tm, D), lambda i, f: (i, 0)),
            pl.BlockSpec((1, D), lambda i, f: (0, 0)),
            _mod_spec(layer, 3 * j, tm, 0),
            _mod_spec(layer, 3 * j + 1, tm, 0),
            _mod_spec(layer, 3 * j + 2, tm, 0),
            pl.BlockSpec((None, None, D, tf), lambda i, f: (layer, half, 0, f)),
            pl.BlockSpec((None, None, D, tf), lambda i, f: (layer, half, 0, nf + f)),
            pl.BlockSpec((None, None, tf, D), lambda i, f: (layer, half, f, 0)),
        ],
        out_specs=pl.BlockSpec((tm, D), lambda i, f: (i, 0)),
        scratch_shapes=[pltpu.VMEM((tm, D), BF16), pltpu.VMEM((1, D), F32)],
        compiler_params=_cparams(("parallel", "arbitrary")),
        name="ffn_half",
    )(x, g.reshape(1, D), mod, mod, mod, w_in, w_in, w_out)


def _mm_kernel(*refs, pro, epi, split, n_outs, tiles_per_out):
    it = iter(refs)
    x_ref = next(it)
    if pro == "cast" and split is not None:
        x2_ref = next(it)
    if pro == "gla":
        x2_ref, gin_ref = next(it), next(it)
    if pro in ("ada", "rms", "gla"):
        g_ref = next(it)
    if pro == "ada":
        sh_ref, sc_ref = next(it), next(it)
    w_ref = next(it)
    if epi == "resid":
        res_ref, gt_ref = next(it), next(it)
    o_refs = [next(it) for _ in range(n_outs)]
    if pro != "cast":
        h_ref = next(it)
    if pro == "ada":
        gm_ref = next(it)

    if pro != "cast":
        @pl.when(pl.program_id(1) == 0)
        def _():
            if pro == "ada":
                gm_ref[...] = g_ref[...] * (1.0 + sc_ref[...])

            def rows_fn(rows):
                if pro == "ada":
                    h = _rms(x_ref[rows, :]) * gm_ref[...] + sh_ref[...]
                elif pro == "rms":
                    h = _rms(x_ref[rows, :]) * g_ref[...]
                else:
                    o = x_ref[rows, :] + x2_ref[rows, :]
                    parts = [_rms(o[:, k * GLA_DV:(k + 1) * GLA_DV]) * g_ref[...] for k in range(GLA_HEADS)]
                    h = jnp.concatenate(parts, axis=-1) * _silu(gin_ref[rows, :])
                h_ref[rows, :] = h.astype(BF16)

            _for_row_chunks(x_ref.shape[0], rows_fn)

    def finish(lhs_ref):
        y = _dot(lhs_ref[...], w_ref[...])
        if epi == "resid":
            y = res_ref[...] + gt_ref[...] * y
        if n_outs == 1:
            o_refs[0][...] = y.astype(o_refs[0].dtype)
        else:
            for k, o_ref in enumerate(o_refs):
                @pl.when(pl.program_id(1) // tiles_per_out == k)
                def _(o_ref=o_ref):
                    o_ref[...] = y.astype(o_ref.dtype)

    if pro != "cast":
        finish(h_ref)
    elif split is None:
        finish(x_ref)
    else:
        pl.when(pl.program_id(0) < split)(lambda: finish(x_ref))
        pl.when(pl.program_id(0) >= split)(lambda: finish(x2_ref))


def _mm(x, w, *, rows, row_off=0, xcol=0, n_outs=1, pro, epi="plain", out_dtype=F32, tm=1024, tn=1024,
        g=None, mod=None, layer=None, jmod=None, x2=None, gin=None, gin_col=0, res=None):
    kdim, n = w.shape
    tn = min(tn, n // n_outs)
    tiles_per_out = n // n_outs // tn
    assert rows % tm == 0 and n % (n_outs * tn) == 0 and (n_outs == 1 or epi == "plain")
    split = None
    if pro == "cast" and x2 is not None:
        split = x.shape[0] // tm
        xspec = pl.BlockSpec((tm, kdim), lambda i, j: (jnp.minimum(i, split - 1), 0))
        args, specs = [x, x2], [xspec, pl.BlockSpec((tm, kdim), lambda i, j: (jnp.maximum(i - split, 0), 0))]
    else:
        xspec = pl.BlockSpec((tm, kdim), lambda i, j: (i + row_off, xcol))
        args, specs = [x], [xspec]
    if pro == "gla":
        args += [x2, gin]
        specs += [xspec, pl.BlockSpec((tm, kdim), lambda i, j: (i + row_off, gin_col))]
    if pro in ("ada", "rms", "gla"):
        args.append(g.reshape(1, -1))
        specs.append(pl.BlockSpec((1, g.shape[-1]), lambda i, j: (0, 0)))
    if pro == "ada":
        args += [mod, mod]
        specs += [_mod_spec(layer, 3 * jmod, tm, row_off), _mod_spec(layer, 3 * jmod + 1, tm, row_off)]
    args.append(w)
    specs.append(pl.BlockSpec((kdim, tn), lambda i, j: (0, j)))
    if epi == "resid":
        args += [res, mod]
        specs += [
            pl.BlockSpec((tm, tn), lambda i, j: (i + row_off, j)),
            pl.BlockSpec((None, None, None, 1, tn),
                         lambda i, j: (layer, _group_of_row((i + row_off) * tm), 3 * jmod + 2, 0, j)),
        ]
    scratch = []
    if pro != "cast":
        scratch.append(pltpu.VMEM((tm, kdim), BF16))
    if pro == "ada":
        scratch.append(pltpu.VMEM((1, kdim), F32))
    out_specs = [pl.BlockSpec((tm, tn), functools.partial(
        lambda i, j, k: (i, jnp.clip(j - k * tiles_per_out, 0, tiles_per_out - 1)), k=k)) for k in range(n_outs)]
    outs = pl.pallas_call(
        functools.partial(_mm_kernel, pro=pro, epi=epi, split=split, n_outs=n_outs, tiles_per_out=tiles_per_out),
        out_shape=[jax.ShapeDtypeStruct((rows, n // n_outs), out_dtype)] * n_outs,
        grid=(rows // tm, n // tn),
        in_specs=specs,
        out_specs=out_specs,
        scratch_shapes=scratch,
        compiler_params=_cparams(("parallel", "arbitrary")),
        name="proj_" + pro + "_" + epi,
    )(*args)
    return outs[0] if n_outs == 1 else outs


def _rmsnorm_kernel(x_ref, g_ref, o_ref):
    o_ref[...] = _rms(x_ref[...]) * g_ref[...]


def _rmsnorm(x, g, *, rows, row_off=0, xcol=0, tm=512):
    width = g.shape[-1]
    return pl.pallas_call(
        _rmsnorm_kernel,
        out_shape=jax.ShapeDtypeStruct((rows, width), F32),
        grid=(rows // tm,),
        in_specs=[pl.BlockSpec((tm, width), lambda i: (i + row_off, xcol)),
                  pl.BlockSpec((1, width), lambda i: (0, 0))],
        out_specs=pl.BlockSpec((tm, width), lambda i: (i, 0)),
        compiler_params=_cparams(("parallel",)),
        name="rmsnorm",
    )(x, g.reshape(1, width))


def _rope(x, cos, sin):
    width = x.shape[-1]
    lane = lax.broadcasted_iota(jnp.int32, x.shape, 1)
    up = pltpu.roll(x, width - 16, 1)
    down = pltpu.roll(x, 16, 1)
    swapped = jnp.where((lane & 31) < 16, up, down)
    return x * cos + swapped * sin


def _rope_keys_kernel(x_ref, cos_ref, sin_ref, o_ref):
    o_ref[...] = _rope(x_ref[...], cos_ref[...], sin_ref[...]).astype(o_ref.dtype)


def _rope_keys(down, rope_tabs, *, tm=1024):
    cos, sin = rope_tabs
    per_seq = DEC_SEQ // tm
    return pl.pallas_call(
        _rope_keys_kernel,
        out_shape=jax.ShapeDtypeStruct((N_LAT, 128), BF16),
        grid=(N_LAT // tm,),
        in_specs=[pl.BlockSpec((tm, 128), lambda i: (N_CTX // tm + i, MLA_Q_LORA // 128)),
                  pl.BlockSpec((tm, 128), lambda i: (i % per_seq, 0)),
                  pl.BlockSpec((tm, 128), lambda i: (i % per_seq, 0))],
        out_specs=pl.BlockSpec((tm, 128), lambda i: (i, 0)),
        compiler_params=_cparams(("parallel",)),
        name="rope_keys",
    )(down, cos, sin)


def _attn_kernel(*refs, nseg, has_r, rope, scale, dh, dv, pps):
    it = iter(refs)
    q_ref = next(it)
    qr_ref = next(it) if has_r else None
    segs = []
    for _ in range(nseg):
        k_ref = next(it)
        kr_ref = next(it) if has_r else None
        v_ref = next(it)
        segs.append((k_ref, kr_ref, v_ref))
    if rope:
        cq_ref, sq_ref = next(it), next(it)
    o_ref = next(it)

    if has_r:
        qr = qr_ref[...].astype(F32)
        if rope:
            qr = _rope(qr, cq_ref[...], sq_ref[...])
        qrs = [qr[:, h * MLA_ROPE:(h + 1) * MLA_ROPE].astype(BF16) for h in range(2 * pps)]
        krs = [kr_ref[:, :MLA_ROPE].astype(BF16) for _, kr_ref, _ in segs]

    for pair in range(pps):
        scores = []
        for h in (2 * pair, 2 * pair + 1):
            q = q_ref[:, h * dh:(h + 1) * dh].astype(BF16)
            ss = []
            for si, (k_ref, _, _) in enumerate(segs):
                s = _dot_nt(q, k_ref[:, h * dh:(h + 1) * dh].astype(BF16))
                if has_r:
                    s = s + _dot_nt(qrs[h], krs[si])
                ss.append(s * (scale * LOG2E))
            scores.append(ss)
        for h, ss in zip((2 * pair, 2 * pair + 1), scores):
            m = ss[0].max(axis=-1, keepdims=True)
            for s in ss[1:]:
                m = jnp.maximum(m, s.max(axis=-1, keepdims=True))
            acc = None
            den = None
            for s, (_, _, v_ref) in zip(ss, segs):
                e = jnp.exp2(s - m)
                d = e.sum(axis=-1, keepdims=True)
                pv = _dot(e.astype(BF16), v_ref[:, h * dv:(h + 1) * dv].astype(BF16))
                acc = pv if acc is None else acc + pv
                den = d if den is None else den + d
            o_ref[:, h * dv:(h + 1) * dv] = (acc / den).astype(o_ref.dtype)


def _attention(q, segs, *, nb, lq, tq, q_row0, q_col, scale, qr=None, rope_tabs=None, pps=1, dh=128, dv=128):
    npairs = 8
    ng = npairs // pps
    nq = lq // tq
    has_r = qr is not None
    q_blk0 = q_row0 // tq
    assert q_col % pps == 0 and (rope_tabs is None or pps == 1)
    args = [q]
    specs = [pl.BlockSpec((tq, 2 * dh * pps), lambda b, p, t: (q_blk0 + b * nq + t, q_col // pps + p))]
    if has_r:
        qr_arr, qr_col = qr
        assert qr_col % pps == 0
        args.append(qr_arr)
        specs.append(pl.BlockSpec((tq, 2 * MLA_ROPE * pps),
                                  lambda b, p, t: (q_blk0 + b * nq + t, qr_col // pps + p)))
    for sg in segs:
        lk = sg["lk"]
        k_arr, k_row0, k_col = sg["k"]
        assert k_col % pps == 0
        args.append(k_arr)
        specs.append(pl.BlockSpec((lk, 2 * dh * pps), functools.partial(
            lambda b, p, t, r0, c0: (r0 + b, c0 + p), r0=k_row0 // lk, c0=k_col // pps)))
        if has_r:
            kr_arr, kr_row0, kr_col, kr_w = sg["kr"]
            args.append(kr_arr)
            specs.append(pl.BlockSpec((lk, kr_w), functools.partial(
                lambda b, p, t, r0, c0: (r0 + b, c0), r0=kr_row0 // lk, c0=kr_col)))
        v_arr, v_row0, v_col = sg["v"]
        assert v_col % pps == 0
        args.append(v_arr)
        specs.append(pl.BlockSpec((lk, 2 * dv * pps), functools.partial(
            lambda b, p, t, r0, c0: (r0 + b, c0 + p), r0=v_row0 // lk, c0=v_col // pps)))
    if rope_tabs is not None:
        cos, sin = rope_tabs
        args += [cos, sin]
        specs += [pl.BlockSpec((tq, 128), lambda b, p, t: (t, 0)),
                  pl.BlockSpec((tq, 128), lambda b, p, t: (t, 0))]
    return pl.pallas_call(
        functools.partial(_attn_kernel, nseg=len(segs), has_r=has_r, rope=rope_tabs is not None,
                          scale=scale, dh=dh, dv=dv, pps=pps),
        out_shape=jax.ShapeDtypeStruct((nb * lq, npairs * 2 * dv), BF16),
        grid=(nb, ng, nq),
        in_specs=specs,
        out_specs=pl.BlockSpec((tq, 2 * dv * pps), lambda b, p, t: (b * nq + t, p)),
        compiler_params=_cparams(("parallel", "parallel", "arbitrary")),
        name="attention",
    )(*args)


def _na_kernel(q_ref, k_ref, v_ref, kc_ref, vc_ref, t2_ref, o_ref, bias_ref, cap_ref):
    r0 = pl.program_id(1) * NA_QR
    ks = jnp.clip(r0 - NA_WIN_ROWS // 2, 0, LAT_ROWS - NA_KR)
    nq, nk = NA_QR * GRID_W, NA_KR * GRID_W
    scale = NA_DH ** -0.5 * LOG2E

    @pl.when(pl.program_id(2) == 0)
    def _():
        row = lax.broadcasted_iota(jnp.int32, (nq, nk), 0)
        lane = lax.broadcasted_iota(jnp.int32, (nq, nk), 1)
        qc = row & (GRID_W - 1)
        kc = lane & (GRID_W - 1)
        rs = jnp.clip(r0 + (row >> LOG2_GRID_W) - NA_WIN_ROWS // 2, 0, LAT_ROWS - NA_WIN_ROWS)
        kr = ks + (lane >> LOG2_GRID_W)
        cs = jnp.clip(qc - NA_WIN_COLS // 2, 0, GRID_W - NA_WIN_COLS)
        ok = (kr >= rs) & (kr < rs + NA_WIN_ROWS) & (kc >= cs) & (kc < cs + NA_WIN_COLS)
        cap_ref[...] = jnp.where(ok, jnp.inf, NEG_INF)
        for hh in range(2):
            for i in range(NA_QR):
                for jp in range(NA_KR // 2):
                    e = jnp.clip(ks - r0 + 2 * jp - i + NA_WIN_ROWS, 0, 2 * NA_WIN_ROWS - 1)
                    bias_ref[hh, i * GRID_W:(i + 1) * GRID_W, jp * 128:(jp + 1) * 128] = t2_ref[hh, e] * LOG2E

    start = pl.multiple_of(ks * GRID_W, GRID_W)
    kw = k_ref[pl.ds(start, nk), :]
    vw = v_ref[pl.ds(start, nk), :]
    cap = cap_ref[...]
    for hh in range(2):
        sl = slice(hh * NA_DH, (hh + 1) * NA_DH)
        q = q_ref[:, sl]
        s_loc = jnp.minimum(_dot_nt(q, kw[:, sl]) * scale + bias_ref[hh], cap)
        s_ctx = _dot_nt(q, kc_ref[:, sl]) * scale
        m = jnp.maximum(s_loc.max(axis=-1, keepdims=True), s_ctx.max(axis=-1, keepdims=True))
        e_loc = jnp.exp2(s_loc - m)
        e_ctx = jnp.exp2(s_ctx - m)
        den = e_loc.sum(axis=-1, keepdims=True) + e_ctx.sum(axis=-1, keepdims=True)
        acc = _dot(e_loc.astype(BF16), vw[:, sl]) + _dot(e_ctx.astype(BF16), vc_ref[:, sl])
        o_ref[:, sl] = (acc / den).astype(o_ref.dtype)


def _na_bias_table(rpb):
    col = jnp.arange(GRID_W)
    col_off = jnp.clip(col[None, :] - col[:, None] + NA_WIN_COLS - 1, 0, 2 * NA_WIN_COLS - 2)
    e = jnp.arange(2 * NA_WIN_ROWS)
    dr = jnp.clip(jnp.stack([e - 1, e], axis=1), 0, 2 * NA_WIN_ROWS - 2)
    tb = rpb[:, dr][:, :, :, col_off]
    return jnp.transpose(tb, (0, 1, 3, 2, 4)).reshape(NA_HEADS, 2 * NA_WIN_ROWS, GRID_W, 2 * GRID_W)


def _na_latent(qkv_l, kc, vc, t2):
    npairs = NA_HEADS // 2
    w2 = 2 * NA_DH
    nrb = LAT_ROWS // NA_QR
    nq = NA_QR * GRID_W
    return pl.pallas_call(
        _na_kernel,
        out_shape=jax.ShapeDtypeStruct((N_LAT, D), BF16),
        grid=(npairs, nrb, DEC_BATCH),
        in_specs=[
            pl.BlockSpec((nq, w2), lambda p, r, b: (b * nrb + r, p)),
            pl.BlockSpec((DEC_SEQ, w2), lambda p, r, b: (b, npairs + p)),
            pl.BlockSpec((DEC_SEQ, w2), lambda p, r, b: (b, 2 * npairs + p)),
            pl.BlockSpec((PAST, w2), lambda p, r, b: (b, p)),
            pl.BlockSpec((PAST, w2), lambda p, r, b: (b, p)),
            pl.BlockSpec((2, 2 * NA_WIN_ROWS, GRID_W, 2 * GRID_W), lambda p, r, b: (p, 0, 0, 0)),
        ],
        out_specs=pl.BlockSpec((nq, w2), lambda p, r, b: (b * nrb + r, p)),
        scratch_shapes=[pltpu.VMEM((2, nq, NA_KR * GRID_W), F32), pltpu.VMEM((nq, NA_KR * GRID_W), F32)],
        compiler_params=_cparams(("parallel", "parallel", "arbitrary")),
        name="na_latent",
    )(qkv_l, qkv_l, qkv_l, kc, vc, t2)


def _gla_pos(t):
    is_ctx = t < GLA_CTX_STEPS
    u = jnp.maximum(t - GLA_CTX_STEPS, 0)
    seq = jnp.where(is_ctx, t // GLA_CTX_CHUNKS, u // GLA_LAT_CHUNKS)
    n = jnp.where(is_ctx, t % GLA_CTX_CHUNKS, u % GLA_LAT_CHUNKS)
    return is_ctx, seq, n


def _gla_bwd_chunk(t):
    is_ctx, _, n = _gla_pos(t)
    return t + jnp.where(is_ctx, GLA_CTX_CHUNKS, GLA_LAT_CHUNKS) - 1 - 2 * n


def _gla_kernel(qf_ref, kf_ref, vf_ref, gdf_ref, qb_ref, kb_ref, vb_ref, gdb_ref, wgu_ref, bg_ref, s0_ref,
                of_ref, ob_ref, fin_ref, st_ref):
    is_ctx, _, n = _gla_pos(pl.program_id(0))
    is_lat = jnp.logical_not(is_ctx)

    @pl.when(jnp.logical_and(n == 0, is_ctx))
    def _():
        st_ref[...] = jnp.zeros_like(st_ref)

    @pl.when(jnp.logical_and(n == 0, is_lat))
    def _():
        st_ref[...] = s0_ref[...]

    ri = lax.broadcasted_iota(jnp.int32, (GLA_CHUNK, GLA_CHUNK), 0)
    ci = lax.broadcasted_iota(jnp.int32, (GLA_CHUNK, GLA_CHUNK), 1)
    streams = ((qf_ref, kf_ref, vf_ref, gdf_ref, of_ref), (qb_ref, kb_ref, vb_ref, gdb_ref, ob_ref))
    for d, (q_ref, k_ref, v_ref, gd_ref, o_ref) in enumerate(streams):
        keep = (ci <= ri) if d == 0 else (ci >= ri)
        tri = jnp.where(keep, 1.0, 0.0).astype(BF16)
        pre = _dot(gd_ref[...].astype(BF16), wgu_ref[d]) + bg_ref[d]
        la = (jnp.minimum(pre, 0.0) - jnp.log1p(jnp.exp(-jnp.abs(pre)))) / GLA_TAU
        hi = la.astype(BF16)
        r1 = la - hi.astype(F32)
        mid = r1.astype(BF16)
        lo = (r1 - mid.astype(F32)).astype(BF16)
        cum = _dot(tri, hi) + _dot(tri, mid) + _dot(tri, lo)
        tot = jnp.sum(la, axis=0, keepdims=True)
        k = k_ref[...]
        q_dec = (q_ref[...] * (GLA_DK ** -0.5) * jnp.exp(cum)).astype(BF16)
        k_inv = (k * jnp.exp(-cum)).astype(BF16)
        k_end = (k * jnp.exp(tot - cum)).astype(BF16)
        dec = jnp.exp(tot)
        v = v_ref[...].astype(BF16)
        for h in range(GLA_HEADS):
            ks = slice(h * GLA_DK, (h + 1) * GLA_DK)
            vs = slice(h * GLA_DV, (h + 1) * GLA_DV)
            att = jnp.where(keep, _dot_nt(q_dec[:, ks], k_inv[:, ks]), 0.0)
            st = st_ref[d, h]
            o_ref[:, vs] = _dot(att.astype(BF16), v[:, vs]) + _dot_nt(q_dec[:, ks], st.astype(BF16))
            st_ref[d, h] = st * dec[:, ks] + _dot_tn(v[:, vs], k_end[:, ks])

    @pl.when(jnp.logical_and(is_ctx, n == GLA_CTX_CHUNKS - 1))
    def _():
        fin_ref[...] = st_ref[...]


def _gla_scan(proj, wgu, bg, s0t):
    hk = GLA_HEADS * GLA_DK
    hv = GLA_HEADS * GLA_DV
    gd_col = (GLA_PROJ_N - 128) // 128

    def fwd(c):
        return lambda t: (t, c)

    def bwd(c):
        return lambda t: (_gla_bwd_chunk(t), c)

    in_specs = []
    for ix in (fwd, bwd):
        in_specs += [pl.BlockSpec((GLA_CHUNK, hk), ix(0)), pl.BlockSpec((GLA_CHUNK, hk), ix(1)),
                     pl.BlockSpec((GLA_CHUNK, hv), ix(1)), pl.BlockSpec((GLA_CHUNK, 128), ix(gd_col))]
    st_block = (None, 2, GLA_HEADS, GLA_DV, GLA_DK)
    in_specs += [
        pl.BlockSpec((2, 128, hk), lambda t: (0, 0, 0)),
        pl.BlockSpec((2, 1, hk), lambda t: (0, 0, 0)),
        pl.BlockSpec(st_block, lambda t: (jnp.where(_gla_pos(t)[0], 0, _gla_pos(t)[1]), 0, 0, 0, 0)),
    ]
    o_shape = jax.ShapeDtypeStruct((N_TOK, hv), F32)
    return pl.pallas_call(
        _gla_kernel,
        out_shape=(o_shape, o_shape, jax.ShapeDtypeStruct((BATCH, 2, GLA_HEADS, GLA_DV, GLA_DK), F32)),
        grid=(GLA_STEPS,),
        in_specs=in_specs,
        out_specs=(
            pl.BlockSpec((GLA_CHUNK, hv), lambda t: (t, 0)),
            pl.BlockSpec((GLA_CHUNK, hv), lambda t: (_gla_bwd_chunk(t), 0)),
            pl.BlockSpec(st_block, lambda t: (jnp.where(_gla_pos(t)[0], _gla_pos(t)[1], BATCH - 1), 0, 0, 0, 0)),
        ),
        scratch_shapes=[pltpu.VMEM((2, GLA_HEADS, GLA_DV, GLA_DK), F32)],
        compiler_params=_cparams(("arbitrary",)),
        name="gla_scan",
    )(*([proj] * 8), wgu, bg, s0t)


def _rope_tables():
    t = jnp.arange(DEC_SEQ)
    d = MLA_ROPE // 2
    inv = ROPE_THETA ** (-jnp.arange(0, d, 2, dtype=F32) / d)
    ang_r = (t // GRID_W).astype(F32)[:, None] * inv[None]
    ang_c = (t % GRID_W).astype(F32)[:, None] * inv[None]
    cos = jnp.concatenate([jnp.cos(ang_r)] * 2 + [jnp.cos(ang_c)] * 2, axis=-1)
    sin = jnp.concatenate([-jnp.sin(ang_r), jnp.sin(ang_r), -jnp.sin(ang_c), jnp.sin(ang_c)], axis=-1)
    return jnp.concatenate([cos, cos], axis=-1), jnp.concatenate([sin, sin], axis=-1)


def _mixer_na(x, mod, layer, g, w_qkv, w_o, rpb, cache_k, cache_v):
    w_qkv = w_qkv.astype(BF16)
    ada = dict(pro="ada", g=g, mod=mod, layer=layer, jmod=1)
    q_c, k_c, v_c = _mm(x, w_qkv, rows=N_CTX, n_outs=3, out_dtype=F32, tm=512, **ada)
    qkv_l = _mm(x, w_qkv, rows=N_LAT, row_off=N_CTX // 1024, out_dtype=BF16, tn=1536, **ada)
    o_c = _attention(q_c, [dict(k=(k_c, 0, 0), v=(v_c, 0, 0), lk=SEQ)],
                     nb=BATCH, lq=SEQ, tq=SEQ, q_row0=0, q_col=0, scale=NA_DH ** -0.5, pps=CTX_PPS)
    kc = cache_k.reshape(DEC_BATCH * PAST, D).astype(BF16)
    vc = cache_v.reshape(DEC_BATCH * PAST, D).astype(BF16)
    o_l = _na_latent(qkv_l, kc, vc, _na_bias_table(rpb))
    x = _mm(o_c, w_o.astype(BF16), rows=N_TOK, pro="cast", x2=o_l, epi="resid", res=x, mod=mod, layer=layer,
            jmod=1, tm=512, tn=2048)
    return x, k_c.reshape(BATCH, SEQ, NA_HEADS, NA_DH), v_c.reshape(BATCH, SEQ, NA_HEADS, NA_DH)


def _mixer_mla(x, mod, layer, g, w_down, g_q, w_uq, g_kv, w_ukv, w_o, cache_ckv, cache_kr):
    hq = MLA_NOPE + MLA_ROPE
    wd = jnp.concatenate([
        w_down[:, :MLA_Q_LORA],
        w_down[:, MLA_Q_LORA + MLA_KV_LORA:],
        jnp.zeros((D, 256 - MLA_ROPE), F32),
        w_down[:, MLA_Q_LORA:MLA_Q_LORA + MLA_KV_LORA]], axis=1).astype(BF16)
    wq = w_uq.reshape(MLA_Q_LORA, MLA_HEADS, hq)
    wq = jnp.concatenate([wq[:, :, :MLA_NOPE].reshape(MLA_Q_LORA, -1),
                          wq[:, :, MLA_NOPE:].reshape(MLA_Q_LORA, -1)], axis=1).astype(BF16)
    wkv = w_ukv.reshape(MLA_KV_LORA, MLA_HEADS, MLA_NOPE + MLA_V)
    wkv = jnp.concatenate([wkv[:, :, :MLA_NOPE].reshape(MLA_KV_LORA, -1),
                           wkv[:, :, MLA_NOPE:].reshape(MLA_KV_LORA, -1)], axis=1).astype(BF16)

    down = _mm(x, wd, rows=N_TOK, pro="ada", g=g, mod=mod, layer=layer, jmod=1, out_dtype=F32, tn=MLA_DOWN_N)
    q = _mm(down, wq, rows=N_TOK, pro="rms", g=g_q, out_dtype=F32, tn=1536)
    kv = _mm(down, wkv, rows=N_TOK, xcol=2, pro="rms", g=g_kv, out_dtype=BF16, tn=2048)
    kv_cache = _mm(cache_ckv.reshape(DEC_BATCH * PAST, MLA_KV_LORA).astype(BF16), wkv,
                   rows=DEC_BATCH * PAST, pro="cast", out_dtype=BF16, tn=2048)
    ckv_c = _rmsnorm(down, g_kv, rows=N_CTX, xcol=2)
    kr_c = down[:N_CTX, MLA_Q_LORA:MLA_Q_LORA + MLA_ROPE]

    npairs = MLA_HEADS // 2
    kr_col = MLA_Q_LORA // 128
    o_c = _attention(q, [dict(k=(kv, 0, 0), v=(kv, 0, npairs), kr=(down, 0, kr_col, 128), lk=SEQ)],
                     nb=BATCH, lq=SEQ, tq=SEQ, q_row0=0, q_col=0, qr=(q, 2 * npairs), scale=MLA_SCALE, pps=CTX_PPS)
    tabs = _rope_tables()
    kr_l = _rope_keys(down, tabs)
    o_l = _attention(
        q,
        [dict(k=(kv, N_CTX, 0), v=(kv, N_CTX, npairs), kr=(kr_l, 0, 0, 128), lk=DEC_SEQ),
         dict(k=(kv_cache, 0, 0), v=(kv_cache, 0, npairs),
              kr=(cache_kr.reshape(DEC_BATCH * PAST, MLA_ROPE), 0, 0, MLA_ROPE), lk=PAST)],
        nb=DEC_BATCH, lq=DEC_SEQ, tq=512, q_row0=N_CTX, q_col=0, qr=(q, 2 * npairs), scale=MLA_SCALE,
        rope_tabs=tabs)
    x = _mm(o_c, w_o.astype(BF16), rows=N_TOK, pro="cast", x2=o_l, epi="resid", res=x, mod=mod, layer=layer,
            jmod=1, tm=512, tn=2048)
    return x, ckv_c.reshape(BATCH, SEQ, MLA_KV_LORA), kr_c.reshape(BATCH, SEQ, MLA_ROPE)


def _mixer_gla(x, mod, layer, g, w_in, w_gd, w_gu, b_g, g_norm, w_o, state):
    w_cat = jnp.concatenate([w_in, w_gd[0], w_gd[1], jnp.zeros((D, 128 - 2 * GLA_RANK), F32)], axis=1).astype(BF16)
    proj = _mm(x, w_cat, rows=N_TOK, pro="ada", g=g, mod=mod, layer=layer, jmod=1, out_dtype=F32, tn=896)
    wgu = jnp.zeros((2, 128, GLA_HEADS * GLA_DK), F32)
    wgu = wgu.at[0, :GLA_RANK].set(w_gu[0]).at[1, GLA_RANK:2 * GLA_RANK].set(w_gu[1]).astype(BF16)
    bg = b_g.reshape(2, 1, GLA_HEADS * GLA_DK)
    o_f, o_b, st_c = _gla_scan(proj, wgu, bg, jnp.swapaxes(state, -1, -2))
    x = _mm(o_f, w_o.astype(BF16), rows=N_TOK, pro="gla", x2=o_b, gin=proj, gin_col=2, g=g_norm,
            epi="resid", res=x, mod=mod, layer=layer, jmod=1, tm=256, tn=2048)
    return x, jnp.swapaxes(st_c, -1, -2)


def kernel(x_prompt, x_sample, cache_na_k, cache_na_v, cache_mla_ckv, cache_mla_krope, state_gla, c, c_ctx, norm_g, w_ada, b_ada, w_ffn_in, w_ffn_out, w_na_qkv, w_na_o, na_rpb, w_mla_down, g_mla_q, w_mla_uq, g_mla_kv, w_mla_ukv, w_mla_o, w_gla_in, w_gla_gate_down, w_gla_gate_up, b_gla_gate, g_gla_norm, w_gla_o, final_norm_g):
    x = jnp.concatenate([x_prompt.reshape(N_CTX, D), x_sample.reshape(N_LAT, D)], axis=0)
    cond = jnp.concatenate([c_ctx[None], c, jnp.zeros((N_GROUPS - 1 - DEC_BATCH, D), F32)], axis=0)
    mod = _ada_mod(cond, w_ada, b_ada)
    w_in = w_ffn_in.astype(BF16)
    w_out = w_ffn_out.astype(BF16)

    na_k, na_v, ckv, krope, gla_st = [], [], [], [], []
    for i in range(DEPTH):
        kind, slot = i % 3, i // 3
        x = _ffn(x, mod, i, 0, w_in, w_out, norm_g[i, 0])
        if kind == 0:
            x, k_c, v_c = _mixer_na(x, mod, i, norm_g[i, 1], w_na_qkv[slot], w_na_o[slot], na_rpb[slot],
                                    cache_na_k[:, slot], cache_na_v[:, slot])
            na_k.append(k_c)
            na_v.append(v_c)
        elif kind == 1:
            x, ckv_c, kr_c = _mixer_mla(x, mod, i, norm_g[i, 1], w_mla_down[slot], g_mla_q[slot], w_mla_uq[slot],
                                        g_mla_kv[slot], w_mla_ukv[slot], w_mla_o[slot],
                                        cache_mla_ckv[:, slot], cache_mla_krope[:, slot])
            ckv.append(ckv_c)
            krope.append(kr_c)
        else:
            x, st = _mixer_gla(x, mod, i, norm_g[i, 1], w_gla_in[slot], w_gla_gate_down[slot],
                               w_gla_gate_up[slot], b_gla_gate[slot], g_gla_norm[slot], w_gla_o[slot],
                               state_gla[:, slot])
            gla_st.append(st)
        x = _ffn(x, mod, i, 1, w_in, w_out, norm_g[i, 2])

    y_prompt = _rmsnorm(x, final_norm_g, rows=N_CTX).reshape(BATCH, SEQ, D)
    y_sample = _rmsnorm(x, final_norm_g, rows=N_LAT, row_off=N_CTX // 512).reshape(DEC_BATCH, DEC_SEQ, D)
    return (y_prompt, y_sample, jnp.stack(na_k, axis=1), jnp.stack(na_v, axis=1), jnp.stack(ckv, axis=1),
            jnp.stack(krope, axis=1), jnp.stack(gla_st, axis=1))
```

```python
import functools

import jax
import jax.numpy as jnp
from jax import lax
from jax.experimental import pallas as pl
from jax.experimental.pallas import tpu as pltpu

F32 = jnp.float32
BF16 = jnp.bfloat16

D = 2048
BATCH, SEQ = 16, 256
DEC_BATCH, DEC_SEQ = 4, 2048
PAST = 512
DEPTH = 4
N_MOD = 9
EPS = 1e-6
D_FF = 5632
GRID_W = 64
LOG2_GRID_W = 6
NEG_INF = -1e30
LOG2E = 1.4426950408889634

N_CTX = BATCH * SEQ
N_LAT = DEC_BATCH * DEC_SEQ
N_TOK = N_CTX + N_LAT
N_GROUPS = 8

NA_HEADS, NA_DH = 16, 128
NA_WIN_ROWS, NA_WIN_COLS = 8, 16
LAT_ROWS = DEC_SEQ // GRID_W
NA_QR = 8
NA_KR = NA_QR + NA_WIN_ROWS

MLA_HEADS = 16
MLA_Q_LORA, MLA_KV_LORA = 768, 512
MLA_NOPE, MLA_ROPE, MLA_V = 128, 64, 128
MLA_SCALE = (MLA_NOPE + MLA_ROPE) ** -0.5
ROPE_THETA = 10000.0
MLA_DOWN_N = 1536

GLA_HEADS, GLA_DK, GLA_DV = 4, 256, 512
GLA_RANK = 16
GLA_TAU = 16.0
GLA_CHUNK = 128
GLA_PROJ_N = 2 * GLA_HEADS * GLA_DK + 2 * GLA_HEADS * GLA_DV + 128
GLA_CTX_CHUNKS = SEQ // GLA_CHUNK
GLA_LAT_CHUNKS = DEC_SEQ // GLA_CHUNK
GLA_CTX_STEPS = BATCH * GLA_CTX_CHUNKS
GLA_STEPS = GLA_CTX_STEPS + DEC_BATCH * GLA_LAT_CHUNKS

VMEM_LIMIT = 56 * 1024 * 1024
ROW_CHUNK = 16
CTX_PPS = 8
FFN_TF = 512
FFN_SUB = 512


def _cparams(sem):
    return pltpu.CompilerParams(dimension_semantics=sem, vmem_limit_bytes=VMEM_LIMIT)


def _group_of_row(r0):
    return jnp.where(r0 < N_CTX, 0, 1 + (r0 - N_CTX) // DEC_SEQ)


def _silu(x):
    return x / (1.0 + jnp.exp(-x))


def _rms(x):
    return x * lax.rsqrt(jnp.mean(x * x, axis=-1, keepdims=True) + EPS)


def _for_row_chunks(n_rows, body):
    def step(c, carry):
        body(pl.ds(pl.multiple_of(c * ROW_CHUNK, ROW_CHUNK), ROW_CHUNK))
        return carry
    lax.fori_loop(0, n_rows // ROW_CHUNK, step, 0, unroll=8)


def _dot(a, b):
    return jnp.dot(a, b, preferred_element_type=F32)


def _dot_nt(a, b):
    return lax.dot_general(a, b, (((1,), (1,)), ((), ())), preferred_element_type=F32)


def _dot_tn(a, b):
    return lax.dot_general(a, b, (((0,), (0,)), ((), ())), preferred_element_type=F32)


def _ada_kernel(c_ref, w_ref, b_ref, o_ref):
    s = _silu(c_ref[...]).astype(BF16)
    o_ref[...] = _dot(s, w_ref[...].astype(BF16)) + b_ref[...]


def _ada_mod(cond, w_ada, b_ada):
    tn = 1024
    n = N_MOD * D
    out = pl.pallas_call(
        _ada_kernel,
        out_shape=jax.ShapeDtypeStruct((DEPTH, N_GROUPS, n), F32),
        grid=(DEPTH, n // tn),
        in_specs=[
            pl.BlockSpec((N_GROUPS, D), lambda l, j: (0, 0)),
            pl.BlockSpec((None, D, tn), lambda l, j: (l, 0, j)),
            pl.BlockSpec((None, 1, tn), lambda l, j: (l, 0, j)),
        ],
        out_specs=pl.BlockSpec((None, N_GROUPS, tn), lambda l, j: (l, 0, j)),
        compiler_params=_cparams(("parallel", "parallel")),
        name="ada_mod",
    )(cond, w_ada, b_ada.reshape(DEPTH, 1, n))
    return out.reshape(DEPTH, N_GROUPS, N_MOD, 1, D)


def _mod_spec(layer, j, tm, row_off):
    return pl.BlockSpec(
        (None, None, None, 1, D),
        lambda i, n: (layer, _group_of_row((i + row_off) * tm), j, 0, 0))


def _ffn_kernel(x_ref, g_ref, m_ref, win_hbm, wout_hbm, o_ref, h_ref, gm_ref, wg_buf, wu_buf, wo_buf, sem,
                *, layer, half):
    i = pl.program_id(0)
    tm = x_ref.shape[0]
    nf = D_FF // FFN_TF

    def tile_copies(f, slot):
        col = pl.multiple_of(f * FFN_TF, FFN_TF)
        return (
            pltpu.make_async_copy(win_hbm.at[layer, half, :, pl.ds(col, FFN_TF)], wg_buf.at[slot], sem.at[0, slot]),
            pltpu.make_async_copy(win_hbm.at[layer, half, :, pl.ds(D_FF + col, FFN_TF)], wu_buf.at[slot],
                                  sem.at[1, slot]),
            pltpu.make_async_copy(wout_hbm.at[layer, half, pl.ds(col, FFN_TF), :], wo_buf.at[slot], sem.at[2, slot]),
        )

    @pl.when(i == 0)
    def _():
        for c in tile_copies(0, 0):
            c.start()

    gm_ref[...] = g_ref[...] * (1.0 + m_ref[1])

    def rows_fn(rows):
        h_ref[rows, :] = (_rms(x_ref[rows, :]) * gm_ref[...] + m_ref[0]).astype(BF16)
        o_ref[rows, :] = jnp.zeros((ROW_CHUNK, D), F32)

    _for_row_chunks(tm, rows_fn)

    def hidden_tile(f, carry):
        slot = (i * nf + f) % 2
        @pl.when(jnp.logical_or(f + 1 < nf, i + 1 < pl.num_programs(0)))
        def _():
            for c in tile_copies(jnp.where(f + 1 < nf, f + 1, 0), 1 - slot):
                c.start()

        for c in tile_copies(f, slot):
            c.wait()
        for r in range(0, tm, FFN_SUB):
            h = h_ref[r:r + FFN_SUB, :]
            a = _silu(_dot(h, wg_buf[slot])) * _dot(h, wu_buf[slot])
            o_ref[r:r + FFN_SUB, :] += _dot(a.astype(BF16), wo_buf[slot])
        return carry

    lax.fori_loop(0, nf, hidden_tile, 0)
    o_ref[...] = x_ref[...] + 0.5 * m_ref[2] * o_ref[...]


def _ffn(x, mod, layer, half, w_in, w_out, g, *, tm=1024):
    tf = FFN_TF
    return pl.pallas_call(
        functools.partial(_ffn_kernel, layer=layer, half=half),
        out_shape=jax.ShapeDtypeStruct((N_TOK, D), F32),
        grid=(N_TOK // tm,),
        in_specs=[
            pl.BlockSpec((tm, D), lambda i: (i, 0)),
            pl.BlockSpec((1, D), lambda i: (0, 0)),
            pl.BlockSpec((None, None, 3, 1, D), lambda i: (layer, _group_of_row(i * tm), 2 * half, 0, 0)),
            pl.BlockSpec(memory_space=pl.ANY),
            pl.BlockSpec(memory_space=pl.ANY),
        ],
        out_specs=pl.BlockSpec((tm, D), lambda i: (i, 0)),
        scratch_shapes=[pltpu.VMEM((tm, D), BF16), pltpu.VMEM((1, D), F32),
                        pltpu.VMEM((2, D, tf), BF16), pltpu.VMEM((2, D, tf), BF16), pltpu.VMEM((2, tf, D), BF16),
                        pltpu.SemaphoreType.DMA((3, 2))],
        compiler_params=_cparams(("arbitrary",)),
        name="ffn_half",
    )(x, g.reshape(1, D), mod, w_in, w_out)


def _mm_kernel(*refs, pro, epi, split, n_outs, tiles_per_out):
    it = iter(refs)
    x_ref = next(it)
    if pro == "cast" and split is not None:
        x2_ref = next(it)
    if pro == "gla":
        x2_ref, gin_ref = next(it), next(it)
    if pro in ("ada", "rms", "gla"):
        g_ref = next(it)
    if pro == "ada":
        sh_ref, sc_ref = next(it), next(it)
    w_ref = next(it)
    if epi == "resid":
        res_ref, gt_ref = next(it), next(it)
    o_refs = [next(it) for _ in range(n_outs)]
    if pro != "cast":
        h_ref = next(it)
    if pro == "ada":
        gm_ref = next(it)

    if pro != "cast":
        @pl.when(pl.program_id(1) == 0)
        def _():
            if pro == "ada":
                gm_ref[...] = g_ref[...] * (1.0 + sc_ref[...])

            def rows_fn(rows):
                if pro == "ada":
                    h = _rms(x_ref[rows, :]) * gm_ref[...] + sh_ref[...]
                elif pro == "rms":
                    h = _rms(x_ref[rows, :]) * g_ref[...]
                else:
                    o = x_ref[rows, :] + x2_ref[rows, :]
                    parts = [_rms(o[:, k * GLA_DV:(k + 1) * GLA_DV]) * g_ref[...] for k in range(GLA_HEADS)]
                    h = jnp.concatenate(parts, axis=-1) * _silu(gin_ref[rows, :])
                h_ref[rows, :] = h.astype(BF16)

            _for_row_chunks(x_ref.shape[0], rows_fn)

    def finish(lhs_ref):
        y = _dot(lhs_ref[...], w_ref[...])
        if epi == "resid":
            y = res_ref[...] + gt_ref[...] * y
        if n_outs == 1:
            o_refs[0][...] = y.astype(o_refs[0].dtype)
        else:
            for k, o_ref in enumerate(o_refs):
                @pl.when(pl.program_id(1) // tiles_per_out == k)
                def _(o_ref=o_ref):
                    o_ref[...] = y.astype(o_ref.dtype)

    if pro != "cast":
        finish(h_ref)
    elif split is None:
        finish(x_ref)
    else:
        pl.when(pl.program_id(0) < split)(lambda: finish(x_ref))
        pl.when(pl.program_id(0) >= split)(lambda: finish(x2_ref))


def _mm(x, w, *, rows, row_off=0, xcol=0, n_outs=1, pro, epi="plain", out_dtype=F32, tm=1024, tn=1024,
        g=None, mod=None, layer=None, jmod=None, x2=None, gin=None, gin_col=0, res=None):
    kdim, n = w.shape
    tn = min(tn, n // n_outs)
    tiles_per_out = n // n_outs // tn
    assert rows % tm == 0 and n % (n_outs * tn) == 0 and (n_outs == 1 or epi == "plain")
    split = None
    if pro == "cast" and x2 is not None:
        split = x.shape[0] // tm
        xspec = pl.BlockSpec((tm, kdim), lambda i, j: (jnp.minimum(i, split - 1), 0))
        args, specs = [x, x2], [xspec, pl.BlockSpec((tm, kdim), lambda i, j: (jnp.maximum(i - split, 0), 0))]
    else:
        xspec = pl.BlockSpec((tm, kdim), lambda i, j: (i + row_off, xcol))
        args, specs = [x], [xspec]
    if pro == "gla":
        args += [x2, gin]
        specs += [xspec, pl.BlockSpec((tm, kdim), lambda i, j: (i + row_off, gin_col))]
    if pro in ("ada", "rms", "gla"):
        args.append(g.reshape(1, -1))
        specs.append(pl.BlockSpec((1, g.shape[-1]), lambda i, j: (0, 0)))
    if pro == "ada":
        args += [mod, mod]
        specs += [_mod_spec(layer, 3 * jmod, tm, row_off), _mod_spec(layer, 3 * jmod + 1, tm, row_off)]
    args.append(w)
    specs.append(pl.BlockSpec((kdim, tn), lambda i, j: (0, j)))
    if epi == "resid":
        args += [res, mod]
        specs += [
            pl.BlockSpec((tm, tn), lambda i, j: (i + row_off, j)),
            pl.BlockSpec((None, None, None, 1, tn),
                         lambda i, j: (layer, _group_of_row((i + row_off) * tm), 3 * jmod + 2, 0, j)),
        ]
    scratch = []
    if pro != "cast":
        scratch.append(pltpu.VMEM((tm, kdim), BF16))
    if pro == "ada":
        scratch.append(pltpu.VMEM((1, kdim), F32))
    out_specs = [pl.BlockSpec((tm, tn), functools.partial(
        lambda i, j, k: (i, jnp.clip(j - k * tiles_per_out, 0, tiles_per_out - 1)), k=k)) for k in range(n_outs)]
    outs = pl.pallas_call(
        functools.partial(_mm_kernel, pro=pro, epi=epi, split=split, n_outs=n_outs, tiles_per_out=tiles_per_out),
        out_shape=[jax.ShapeDtypeStruct((rows, n // n_outs), out_dtype)] * n_outs,
        grid=(rows // tm, n // tn),
        in_specs=specs,
        out_specs=out_specs,
        scratch_shapes=scratch,
        compiler_params=_cparams(("parallel", "arbitrary")),
        name="proj_" + pro + "_" + epi,
    )(*args)
    return outs[0] if n_outs == 1 else outs


def _rmsnorm_kernel(x_ref, g_ref, o_ref):
    o_ref[...] = _rms(x_ref[...]) * g_ref[...]


def _rmsnorm(x, g, *, rows, row_off=0, xcol=0, tm=512):
    width = g.shape[-1]
    return pl.pallas_call(
        _rmsnorm_kernel,
        out_shape=jax.ShapeDtypeStruct((rows, width), F32),
        grid=(rows // tm,),
        in_specs=[pl.BlockSpec((tm, width), lambda i: (i + row_off, xcol)),
                  pl.BlockSpec((1, width), lambda i: (0, 0))],
        out_specs=pl.BlockSpec((tm, width), lambda i: (i, 0)),
        compiler_params=_cparams(("parallel",)),
        name="rmsnorm",
    )(x, g.reshape(1, width))


def _rope(x, cos, sin):
    width = x.shape[-1]
    lane = lax.broadcasted_iota(jnp.int32, x.shape, 1)
    up = pltpu.roll(x, width - 16, 1)
    down = pltpu.roll(x, 16, 1)
    swapped = jnp.where((lane & 31) < 16, up, down)
    return x * cos + swapped * sin


def _rope_keys_kernel(x_ref, cos_ref, sin_ref, o_ref):
    o_ref[...] = _rope(x_ref[...], cos_ref[...], sin_ref[...]).astype(o_ref.dtype)


def _rope_keys(down, rope_tabs, *, tm=1024):
    cos, sin = rope_tabs
    per_seq = DEC_SEQ // tm
    return pl.pallas_call(
        _rope_keys_kernel,
        out_shape=jax.ShapeDtypeStruct((N_LAT, 128), BF16),
        grid=(N_LAT // tm,),
        in_specs=[pl.BlockSpec((tm, 128), lambda i: (N_CTX // tm + i, MLA_Q_LORA // 128)),
                  pl.BlockSpec((tm, 128), lambda i: (i % per_seq, 0)),
                  pl.BlockSpec((tm, 128), lambda i: (i % per_seq, 0))],
        out_specs=pl.BlockSpec((tm, 128), lambda i: (i, 0)),
        compiler_params=_cparams(("parallel",)),
        name="rope_keys",
    )(down, cos, sin)


def _attn_kernel(*refs, nseg, has_r, rope, scale, dh, dv, pps):
    it = iter(refs)
    q_ref = next(it)
    qr_ref = next(it) if has_r else None
    segs = []
    for _ in range(nseg):
        k_ref = next(it)
        kr_ref = next(it) if has_r else None
        v_ref = next(it)
        segs.append((k_ref, kr_ref, v_ref))
    if rope:
        cq_ref, sq_ref = next(it), next(it)
    o_ref = next(it)

    if has_r:
        qr = qr_ref[...].astype(F32)
        if rope:
            qr = _rope(qr, cq_ref[...], sq_ref[...])
        qrs = [qr[:, h * MLA_ROPE:(h + 1) * MLA_ROPE].astype(BF16) for h in range(2 * pps)]
        krs = [kr_ref[:, :MLA_ROPE].astype(BF16) for _, kr_ref, _ in segs]

    for pair in range(pps):
        scores = []
        for h in (2 * pair, 2 * pair + 1):
            q = q_ref[:, h * dh:(h + 1) * dh].astype(BF16)
            ss = []
            for si, (k_ref, _, _) in enumerate(segs):
                s = _dot_nt(q, k_ref[:, h * dh:(h + 1) * dh].astype(BF16))
                if has_r:
                    s = s + _dot_nt(qrs[h], krs[si])
                ss.append(s * (scale * LOG2E))
            scores.append(ss)
        for h, ss in zip((2 * pair, 2 * pair + 1), scores):
            m = ss[0].max(axis=-1, keepdims=True)
            for s in ss[1:]:
                m = jnp.maximum(m, s.max(axis=-1, keepdims=True))
            acc = None
            den = None
            for s, (_, _, v_ref) in zip(ss, segs):
                e = jnp.exp2(s - m)
                d = e.sum(axis=-1, keepdims=True)
                pv = _dot(e.astype(BF16), v_ref[:, h * dv:(h + 1) * dv].astype(BF16))
                acc = pv if acc is None else acc + pv
                den = d if den is None else den + d
            o_ref[:, h * dv:(h + 1) * dv] = (acc / den).astype(o_ref.dtype)


def _attention(q, segs, *, nb, lq, tq, q_row0, q_col, scale, qr=None, rope_tabs=None, pps=1, dh=128, dv=128):
    npairs = 8
    ng = npairs // pps
    nq = lq // tq
    has_r = qr is not None
    q_blk0 = q_row0 // tq
    assert q_col % pps == 0 and (rope_tabs is None or pps == 1)
    args = [q]
    specs = [pl.BlockSpec((tq, 2 * dh * pps), lambda b, p, t: (q_blk0 + b * nq + t, q_col // pps + p))]
    if has_r:
        qr_arr, qr_col = qr
        assert qr_col % pps == 0
        args.append(qr_arr)
        specs.append(pl.BlockSpec((tq, 2 * MLA_ROPE * pps),
                                  lambda b, p, t: (q_blk0 + b * nq + t, qr_col // pps + p)))
    for sg in segs:
        lk = sg["lk"]
        k_arr, k_row0, k_col = sg["k"]
        assert k_col % pps == 0
        args.append(k_arr)
        specs.append(pl.BlockSpec((lk, 2 * dh * pps), functools.partial(
            lambda b, p, t, r0, c0: (r0 + b, c0 + p), r0=k_row0 // lk, c0=k_col // pps)))
        if has_r:
            kr_arr, kr_row0, kr_col, kr_w = sg["kr"]
            args.append(kr_arr)
            specs.append(pl.BlockSpec((lk, kr_w), functools.partial(
                lambda b, p, t, r0, c0: (r0 + b, c0), r0=kr_row0 // lk, c0=kr_col)))
        v_arr, v_row0, v_col = sg["v"]
        assert v_col % pps == 0
        args.append(v_arr)
        specs.append(pl.BlockSpec((lk, 2 * dv * pps), functools.partial(
            lambda b, p, t, r0, c0: (r0 + b, c0 + p), r0=v_row0 // lk, c0=v_col // pps)))
    if rope_tabs is not None:
        cos, sin = rope_tabs
        args += [cos, sin]
        specs += [pl.BlockSpec((tq, 128), lambda b, p, t: (t, 0)),
                  pl.BlockSpec((tq, 128), lambda b, p, t: (t, 0))]
    return pl.pallas_call(
        functools.partial(_attn_kernel, nseg=len(segs), has_r=has_r, rope=rope_tabs is not None,
                          scale=scale, dh=dh, dv=dv, pps=pps),
        out_shape=jax.ShapeDtypeStruct((nb * lq, npairs * 2 * dv), BF16),
        grid=(nb, ng, nq),
        in_specs=specs,
        out_specs=pl.BlockSpec((tq, 2 * dv * pps), lambda b, p, t: (b * nq + t, p)),
        compiler_params=_cparams(("parallel", "parallel", "arbitrary")),
        name="attention",
    )(*args)


def _na_kernel(q_ref, k_ref, v_ref, kc_ref, vc_ref, t2_ref, o_ref, bias_ref, cap_ref):
    r0 = pl.program_id(1) * NA_QR
    ks = jnp.clip(r0 - NA_WIN_ROWS // 2, 0, LAT_ROWS - NA_KR)
    nq, nk = NA_QR * GRID_W, NA_KR * GRID_W
    scale = NA_DH ** -0.5 * LOG2E

    @pl.when(pl.program_id(2) == 0)
    def _():
        row = lax.broadcasted_iota(jnp.int32, (nq, nk), 0)
        lane = lax.broadcasted_iota(jnp.int32, (nq, nk), 1)
        qc = row & (GRID_W - 1)
        kc = lane & (GRID_W - 1)
        rs = jnp.clip(r0 + (row >> LOG2_GRID_W) - NA_WIN_ROWS // 2, 0, LAT_ROWS - NA_WIN_ROWS)
        kr = ks + (lane >> LOG2_GRID_W)
        cs = jnp.clip(qc - NA_WIN_COLS // 2, 0, GRID_W - NA_WIN_COLS)
        ok = (kr >= rs) & (kr < rs + NA_WIN_ROWS) & (kc >= cs) & (kc < cs + NA_WIN_COLS)
        cap_ref[...] = jnp.where(ok, jnp.inf, NEG_INF)
        for hh in range(2):
            for i in range(NA_QR):
                for jp in range(NA_KR // 2):
                    e = jnp.clip(ks - r0 + 2 * jp - i + NA_WIN_ROWS, 0, 2 * NA_WIN_ROWS - 1)
                    bias_ref[hh, i * GRID_W:(i + 1) * GRID_W, jp * 128:(jp + 1) * 128] = t2_ref[hh, e] * LOG2E

    start = pl.multiple_of(ks * GRID_W, GRID_W)
    kw = k_ref[pl.ds(start, nk), :]
    vw = v_ref[pl.ds(start, nk), :]
    cap = cap_ref[...]
    for hh in range(2):
        sl = slice(hh * NA_DH, (hh + 1) * NA_DH)
        q = q_ref[:, sl]
        s_loc = jnp.minimum(_dot_nt(q, kw[:, sl]) * scale + bias_ref[hh], cap)
        s_ctx = _dot_nt(q, kc_ref[:, sl]) * scale
        m = jnp.maximum(s_loc.max(axis=-1, keepdims=True), s_ctx.max(axis=-1, keepdims=True))
        e_loc = jnp.exp2(s_loc - m)
        e_ctx = jnp.exp2(s_ctx - m)
        den = e_loc.sum(axis=-1, keepdims=True) + e_ctx.sum(axis=-1, keepdims=True)
        acc = _dot(e_loc.astype(BF16), vw[:, sl]) + _dot(e_ctx.astype(BF16), vc_ref[:, sl])
        o_ref[:, sl] = (acc / den).astype(o_ref.dtype)


def _na_bias_table(rpb):
    col = jnp.arange(GRID_W)
    col_off = jnp.clip(col[None, :] - col[:, None] + NA_WIN_COLS - 1, 0, 2 * NA_WIN_COLS - 2)
    e = jnp.arange(2 * NA_WIN_ROWS)
    dr = jnp.clip(jnp.stack([e - 1, e], axis=1), 0, 2 * NA_WIN_ROWS - 2)
    tb = rpb[:, dr][:, :, :, col_off]
    return jnp.transpose(tb, (0, 1, 3, 2, 4)).reshape(NA_HEADS, 2 * NA_WIN_ROWS, GRID_W, 2 * GRID_W)


def _na_latent(qkv_l, kc, vc, t2):
    npairs = NA_HEADS // 2
    w2 = 2 * NA_DH
    nrb = LAT_ROWS // NA_QR
    nq = NA_QR * GRID_W
    return pl.pallas_call(
        _na_kernel,
        out_shape=jax.ShapeDtypeStruct((N_LAT, D), BF16),
        grid=(npairs, nrb, DEC_BATCH),
        in_specs=[
            pl.BlockSpec((nq, w2), lambda p, r, b: (b * nrb + r, p)),
            pl.BlockSpec((DEC_SEQ, w2), lambda p, r, b: (b, npairs + p)),
            pl.BlockSpec((DEC_SEQ, w2), lambda p, r, b: (b, 2 * npairs + p)),
            pl.BlockSpec((PAST, w2), lambda p, r, b: (b, p)),
            pl.BlockSpec((PAST, w2), lambda p, r, b: (b, p)),
            pl.BlockSpec((2, 2 * NA_WIN_ROWS, GRID_W, 2 * GRID_W), lambda p, r, b: (p, 0, 0, 0)),
        ],
        out_specs=pl.BlockSpec((nq, w2), lambda p, r, b: (b * nrb + r, p)),
        scratch_shapes=[pltpu.VMEM((2, nq, NA_KR * GRID_W), F32), pltpu.VMEM((nq, NA_KR * GRID_W), F32)],
        compiler_params=_cparams(("parallel", "parallel", "arbitrary")),
        name="na_latent",
    )(qkv_l, qkv_l, qkv_l, kc, vc, t2)


def _gla_pos(t):
    is_ctx = t < GLA_CTX_STEPS
    u = jnp.maximum(t - GLA_CTX_STEPS, 0)
    seq = jnp.where(is_ctx, t // GLA_CTX_CHUNKS, u // GLA_LAT_CHUNKS)
    n = jnp.where(is_ctx, t % GLA_CTX_CHUNKS, u % GLA_LAT_CHUNKS)
    return is_ctx, seq, n


def _gla_bwd_chunk(t):
    is_ctx, _, n = _gla_pos(t)
    return t + jnp.where(is_ctx, GLA_CTX_CHUNKS, GLA_LAT_CHUNKS) - 1 - 2 * n


def _gla_kernel(qf_ref, kf_ref, vf_ref, gdf_ref, qb_ref, kb_ref, vb_ref, gdb_ref, wgu_ref, bg_ref, s0_ref,
                of_ref, ob_ref, fin_ref, st_ref):
    is_ctx, _, n = _gla_pos(pl.program_id(0))
    is_lat = jnp.logical_not(is_ctx)

    @pl.when(jnp.logical_and(n == 0, is_ctx))
    def _():
        st_ref[...] = jnp.zeros_like(st_ref)

    @pl.when(jnp.logical_and(n == 0, is_lat))
    def _():
        st_ref[...] = s0_ref[...]

    ri = lax.broadcasted_iota(jnp.int32, (GLA_CHUNK, GLA_CHUNK), 0)
    ci = lax.broadcasted_iota(jnp.int32, (GLA_CHUNK, GLA_CHUNK), 1)
    streams = ((qf_ref, kf_ref, vf_ref, gdf_ref, of_ref), (qb_ref, kb_ref, vb_ref, gdb_ref, ob_ref))
    for d, (q_ref, k_ref, v_ref, gd_ref, o_ref) in enumerate(streams):
        keep = (ci <= ri) if d == 0 else (ci >= ri)
        tri = jnp.where(keep, 1.0, 0.0).astype(BF16)
        pre = _dot(gd_ref[...].astype(BF16), wgu_ref[d]) + bg_ref[d]
        la = (jnp.minimum(pre, 0.0) - jnp.log1p(jnp.exp(-jnp.abs(pre)))) / GLA_TAU
        hi = la.astype(BF16)
        r1 = la - hi.astype(F32)
        mid = r1.astype(BF16)
        lo = (r1 - mid.astype(F32)).astype(BF16)
        cum = _dot(tri, hi) + _dot(tri, mid) + _dot(tri, lo)
        tot = jnp.sum(la, axis=0, keepdims=True)
        k = k_ref[...]
        q_dec = (q_ref[...] * (GLA_DK ** -0.5) * jnp.exp(cum)).astype(BF16)
        k_inv = (k * jnp.exp(-cum)).astype(BF16)
        k_end = (k * jnp.exp(tot - cum)).astype(BF16)
        dec = jnp.exp(tot)
        v = v_ref[...].astype(BF16)
        for h in range(GLA_HEADS):
            ks = slice(h * GLA_DK, (h + 1) * GLA_DK)
            vs = slice(h * GLA_DV, (h + 1) * GLA_DV)
            att = jnp.where(keep, _dot_nt(q_dec[:, ks], k_inv[:, ks]), 0.0)
            st = st_ref[d, h]
            o_ref[:, vs] = _dot(att.astype(BF16), v[:, vs]) + _dot_nt(q_dec[:, ks], st.astype(BF16))
            st_ref[d, h] = st * dec[:, ks] + _dot_tn(v[:, vs], k_end[:, ks])

    @pl.when(jnp.logical_and(is_ctx, n == GLA_CTX_CHUNKS - 1))
    def _():
        fin_ref[...] = st_ref[...]


def _gla_scan(proj, wgu, bg, s0t):
    hk = GLA_HEADS * GLA_DK
    hv = GLA_HEADS * GLA_DV
    gd_col = (GLA_PROJ_N - 128) // 128

    def fwd(c):
        return lambda t: (t, c)

    def bwd(c):
        return lambda t: (_gla_bwd_chunk(t), c)

    in_specs = []
    for ix in (fwd, bwd):
        in_specs += [pl.BlockSpec((GLA_CHUNK, hk), ix(0)), pl.BlockSpec((GLA_CHUNK, hk), ix(1)),
                     pl.BlockSpec((GLA_CHUNK, hv), ix(1)), pl.BlockSpec((GLA_CHUNK, 128), ix(gd_col))]
    st_block = (None, 2, GLA_HEADS, GLA_DV, GLA_DK)
    in_specs += [
        pl.BlockSpec((2, 128, hk), lambda t: (0, 0, 0)),
        pl.BlockSpec((2, 1, hk), lambda t: (0, 0, 0)),
        pl.BlockSpec(st_block, lambda t: (jnp.where(_gla_pos(t)[0], 0, _gla_pos(t)[1]), 0, 0, 0, 0)),
    ]
    o_shape = jax.ShapeDtypeStruct((N_TOK, hv), F32)
    return pl.pallas_call(
        _gla_kernel,
        out_shape=(o_shape, o_shape, jax.ShapeDtypeStruct((BATCH, 2, GLA_HEADS, GLA_DV, GLA_DK), F32)),
        grid=(GLA_STEPS,),
        in_specs=in_specs,
        out_specs=(
            pl.BlockSpec((GLA_CHUNK, hv), lambda t: (t, 0)),
            pl.BlockSpec((GLA_CHUNK, hv), lambda t: (_gla_bwd_chunk(t), 0)),
            pl.BlockSpec(st_block, lambda t: (jnp.where(_gla_pos(t)[0], _gla_pos(t)[1], BATCH - 1), 0, 0, 0, 0)),
        ),
        scratch_shapes=[pltpu.VMEM((2, GLA_HEADS, GLA_DV, GLA_DK), F32)],
        compiler_params=_cparams(("arbitrary",)),
        name="gla_scan",
    )(*([proj] * 8), wgu, bg, s0t)


def _rope_tables():
    t = jnp.arange(DEC_SEQ)
    d = MLA_ROPE // 2
    inv = ROPE_THETA ** (-jnp.arange(0, d, 2, dtype=F32) / d)
    ang_r = (t // GRID_W).astype(F32)[:, None] * inv[None]
    ang_c = (t % GRID_W).astype(F32)[:, None] * inv[None]
    cos = jnp.concatenate([jnp.cos(ang_r)] * 2 + [jnp.cos(ang_c)] * 2, axis=-1)
    sin = jnp.concatenate([-jnp.sin(ang_r), jnp.sin(ang_r), -jnp.sin(ang_c), jnp.sin(ang_c)], axis=-1)
    return jnp.concatenate([cos, cos], axis=-1), jnp.concatenate([sin, sin], axis=-1)


def _mixer_na(x, mod, layer, g, w_qkv, w_o, rpb, cache_k, cache_v):
    w_qkv = w_qkv.astype(BF16)
    ada = dict(pro="ada", g=g, mod=mod, layer=layer, jmod=1)
    q_c, k_c, v_c = _mm(x, w_qkv, rows=N_CTX, n_outs=3, out_dtype=F32, tm=512, **ada)
    qkv_l = _mm(x, w_qkv, rows=N_LAT, row_off=N_CTX // 1024, out_dtype=BF16, tn=1536, **ada)
    o_c = _attention(q_c, [dict(k=(k_c, 0, 0), v=(v_c, 0, 0), lk=SEQ)],
                     nb=BATCH, lq=SEQ, tq=SEQ, q_row0=0, q_col=0, scale=NA_DH ** -0.5, pps=CTX_PPS)
    kc = cache_k.reshape(DEC_BATCH * PAST, D).astype(BF16)
    vc = cache_v.reshape(DEC_BATCH * PAST, D).astype(BF16)
    o_l = _na_latent(qkv_l, kc, vc, _na_bias_table(rpb))
    x = _mm(o_c, w_o.astype(BF16), rows=N_TOK, pro="cast", x2=o_l, epi="resid", res=x, mod=mod, layer=layer,
            jmod=1, tm=512, tn=2048)
    return x, k_c.reshape(BATCH, SEQ, NA_HEADS, NA_DH), v_c.reshape(BATCH, SEQ, NA_HEADS, NA_DH)


def _mixer_mla(x, mod, layer, g, w_down, g_q, w_uq, g_kv, w_ukv, w_o, cache_ckv, cache_kr):
    hq = MLA_NOPE + MLA_ROPE
    wd = jnp.concatenate([
        w_down[:, :MLA_Q_LORA],
        w_down[:, MLA_Q_LORA + MLA_KV_LORA:],
        jnp.zeros((D, 256 - MLA_ROPE), F32),
        w_down[:, MLA_Q_LORA:MLA_Q_LORA + MLA_KV_LORA]], axis=1).astype(BF16)
    wq = w_uq.reshape(MLA_Q_LORA, MLA_HEADS, hq)
    wq = jnp.concatenate([wq[:, :, :MLA_NOPE].reshape(MLA_Q_LORA, -1),
                          wq[:, :, MLA_NOPE:].reshape(MLA_Q_LORA, -1)], axis=1).astype(BF16)
    wkv = w_ukv.reshape(MLA_KV_LORA, MLA_HEADS, MLA_NOPE + MLA_V)
    wkv = jnp.concatenate([wkv[:, :, :MLA_NOPE].reshape(MLA_KV_LORA, -1),
                           wkv[:, :, MLA_NOPE:].reshape(MLA_KV_LORA, -1)], axis=1).astype(BF16)

    down = _mm(x, wd, rows=N_TOK, pro="ada", g=g, mod=mod, layer=layer, jmod=1, out_dtype=F32, tn=MLA_DOWN_N)
    q = _mm(down, wq, rows=N_TOK, pro="rms", g=g_q, out_dtype=F32, tn=1536)
    kv = _mm(down, wkv, rows=N_TOK, xcol=2, pro="rms", g=g_kv, out_dtype=BF16, tn=2048)
    kv_cache = _mm(cache_ckv.reshape(DEC_BATCH * PAST, MLA_KV_LORA).astype(BF16), wkv,
                   rows=DEC_BATCH * PAST, pro="cast", out_dtype=BF16, tn=2048)
    ckv_c = _rmsnorm(down, g_kv, rows=N_CTX, xcol=2)
    kr_c = down[:N_CTX, MLA_Q_LORA:MLA_Q_LORA + MLA_ROPE]

    npairs = MLA_HEADS // 2
    kr_col = MLA_Q_LORA // 128
    o_c = _attention(q, [dict(k=(kv, 0, 0), v=(kv, 0, npairs), kr=(down, 0, kr_col, 128), lk=SEQ)],
                     nb=BATCH, lq=SEQ, tq=SEQ, q_row0=0, q_col=0, qr=(q, 2 * npairs), scale=MLA_SCALE, pps=CTX_PPS)
    tabs = _rope_tables()
    kr_l = _rope_keys(down, tabs)
    o_l = _attention(
        q,
        [dict(k=(kv, N_CTX, 0), v=(kv, N_CTX, npairs), kr=(kr_l, 0, 0, 128), lk=DEC_SEQ),
         dict(k=(kv_cache, 0, 0), v=(kv_cache, 0, npairs),
              kr=(cache_kr.reshape(DEC_BATCH * PAST, MLA_ROPE), 0, 0, MLA_ROPE), lk=PAST)],
        nb=DEC_BATCH, lq=DEC_SEQ, tq=512, q_row0=N_CTX, q_col=0, qr=(q, 2 * npairs), scale=MLA_SCALE,
        rope_tabs=tabs)
    x = _mm(o_c, w_o.astype(BF16), rows=N_TOK, pro="cast", x2=o_l, epi="resid", res=x, mod=mod, layer=layer,
            jmod=1, tm=512, tn=2048)
    return x, ckv_c.reshape(BATCH, SEQ, MLA_KV_LORA), kr_c.reshape(BATCH, SEQ, MLA_ROPE)


def _mixer_gla(x, mod, layer, g, w_in, w_gd, w_gu, b_g, g_norm, w_o, state):
    w_cat = jnp.concatenate([w_in, w_gd[0], w_gd[1], jnp.zeros((D, 128 - 2 * GLA_RANK), F32)], axis=1).astype(BF16)
    proj = _mm(x, w_cat, rows=N_TOK, pro="ada", g=g, mod=mod, layer=layer, jmod=1, out_dtype=F32, tn=896)
    wgu = jnp.zeros((2, 128, GLA_HEADS * GLA_DK), F32)
    wgu = wgu.at[0, :GLA_RANK].set(w_gu[0]).at[1, GLA_RANK:2 * GLA_RANK].set(w_gu[1]).astype(BF16)
    bg = b_g.reshape(2, 1, GLA_HEADS * GLA_DK)
    o_f, o_b, st_c = _gla_scan(proj, wgu, bg, jnp.swapaxes(state, -1, -2))
    x = _mm(o_f, w_o.astype(BF16), rows=N_TOK, pro="gla", x2=o_b, gin=proj, gin_col=2, g=g_norm,
            epi="resid", res=x, mod=mod, layer=layer, jmod=1, tm=256, tn=2048)
    return x, jnp.swapaxes(st_c, -1, -2)


def kernel(x_prompt, x_sample, cache_na_k, cache_na_v, cache_mla_ckv, cache_mla_krope, state_gla, c, c_ctx, norm_g, w_ada, b_ada, w_ffn_in, w_ffn_out, w_na_qkv, w_na_o, na_rpb, w_mla_down, g_mla_q, w_mla_uq, g_mla_kv, w_mla_ukv, w_mla_o, w_gla_in, w_gla_gate_down, w_gla_gate_up, b_gla_gate, g_gla_norm, w_gla_o, final_norm_g):
    x = jnp.concatenate([x_prompt.reshape(N_CTX, D), x_sample.reshape(N_LAT, D)], axis=0)
    cond = jnp.concatenate([c_ctx[None], c, jnp.zeros((N_GROUPS - 1 - DEC_BATCH, D), F32)], axis=0)
    mod = _ada_mod(cond, w_ada, b_ada)
    w_in = w_ffn_in.astype(BF16)
    w_out = w_ffn_out.astype(BF16)

    na_k, na_v, ckv, krope, gla_st = [], [], [], [], []
    for i in range(DEPTH):
        kind, slot = i % 3, i // 3
        x = _ffn(x, mod, i, 0, w_in, w_out, norm_g[i, 0])
        if kind == 0:
            x, k_c, v_c = _mixer_na(x, mod, i, norm_g[i, 1], w_na_qkv[slot], w_na_o[slot], na_rpb[slot],
                                    cache_na_k[:, slot], cache_na_v[:, slot])
            na_k.append(k_c)
            na_v.append(v_c)
        elif kind == 1:
            x, ckv_c, kr_c = _mixer_mla(x, mod, i, norm_g[i, 1], w_mla_down[slot], g_mla_q[slot], w_mla_uq[slot],
                                        g_mla_kv[slot], w_mla_ukv[slot], w_mla_o[slot],
                                        cache_mla_ckv[:, slot], cache_mla_krope[:, slot])
            ckv.append(ckv_c)
            krope.append(kr_c)
        else:
            x, st = _mixer_gla(x, mod, i, norm_g[i, 1], w_gla_in[slot], w_gla_gate_down[slot],
                               w_gla_gate_up[slot], b_gla_gate[slot], g_gla_norm[slot], w_gla_o[slot],
                               state_gla[:, slot])
            gla_st.append(st)
        x = _ffn(x, mod, i, 1, w_in, w_out, norm_g[i, 2])

    y_prompt = _rmsnorm(x, final_norm_g, rows=N_CTX).reshape(BATCH, SEQ, D)
    y_sample = _rmsnorm(x, final_norm_g, rows=N_LAT, row_off=N_CTX // 512).reshape(DEC_BATCH, DEC_SEQ, D)
    return (y_prompt, y_sample, jnp.stack(na_k, axis=1), jnp.stack(na_v, axis=1), jnp.stack(ckv, axis=1),
            jnp.stack(krope, axis=1), jnp.stack(gla_st, axis=1))
```

```python
import functools

import jax
import jax.numpy as jnp
from jax import lax
from jax.experimental import pallas as pl
from jax.experimental.pallas import tpu as pltpu

F32 = jnp.float32
BF16 = jnp.bfloat16

D = 2048
BATCH, SEQ = 16, 256
DEC_BATCH, DEC_SEQ = 4, 2048
PAST = 512
DEPTH = 4
N_MOD = 9
EPS = 1e-6
D_FF = 5632
GRID_W = 64
LOG2_GRID_W = 6
NEG_INF = -1e30
LOG2E = 1.4426950408889634

N_CTX = BATCH * SEQ
N_LAT = DEC_BATCH * DEC_SEQ
N_TOK = N_CTX + N_LAT
N_GROUPS = 8

NA_HEADS, NA_DH = 16, 128
NA_WIN_ROWS, NA_WIN_COLS = 8, 16
LAT_ROWS = DEC_SEQ // GRID_W
NA_QR = 8
NA_KR = NA_QR + NA_WIN_ROWS

MLA_HEADS = 16
MLA_Q_LORA, MLA_KV_LORA = 768, 512
MLA_NOPE, MLA_ROPE, MLA_V = 128, 64, 128
MLA_SCALE = (MLA_NOPE + MLA_ROPE) ** -0.5
ROPE_THETA = 10000.0
MLA_DOWN_N = 1536

GLA_HEADS, GLA_DK, GLA_DV = 4, 256, 512
GLA_RANK = 16
GLA_TAU = 16.0
GLA_CHUNK = 128
GLA_PROJ_N = 2 * GLA_HEADS * GLA_DK + 2 * GLA_HEADS * GLA_DV + 128
GLA_CTX_CHUNKS = SEQ // GLA_CHUNK
GLA_LAT_CHUNKS = DEC_SEQ // GLA_CHUNK
GLA_CTX_STEPS = BATCH * GLA_CTX_CHUNKS
GLA_STEPS = GLA_CTX_STEPS + DEC_BATCH * GLA_LAT_CHUNKS

VMEM_LIMIT = 56 * 1024 * 1024
ROW_CHUNK = 16
CTX_PPS = 8
FFN_TF = 512
FFN_SUB = 512
FFN_CVT_IN_ROWS = 16
FFN_CVT_OUT_ROWS = 64


def _cparams(sem):
    return pltpu.CompilerParams(dimension_semantics=sem, vmem_limit_bytes=VMEM_LIMIT)


def _group_of_row(r0):
    return jnp.where(r0 < N_CTX, 0, 1 + (r0 - N_CTX) // DEC_SEQ)


def _silu(x):
    return x / (1.0 + jnp.exp(-x))


def _rms(x):
    return x * lax.rsqrt(jnp.mean(x * x, axis=-1, keepdims=True) + EPS)


def _for_row_chunks(n_rows, body):
    def step(c, carry):
        body(pl.ds(pl.multiple_of(c * ROW_CHUNK, ROW_CHUNK), ROW_CHUNK))
        return carry
    lax.fori_loop(0, n_rows // ROW_CHUNK, step, 0, unroll=8)


def _dot(a, b):
    return jnp.dot(a, b, preferred_element_type=F32)


def _dot_nt(a, b):
    return lax.dot_general(a, b, (((1,), (1,)), ((), ())), preferred_element_type=F32)


def _dot_tn(a, b):
    return lax.dot_general(a, b, (((0,), (0,)), ((), ())), preferred_element_type=F32)


def _ada_kernel(c_ref, w_ref, b_ref, o_ref):
    s = _silu(c_ref[...]).astype(BF16)
    o_ref[...] = _dot(s, w_ref[...].astype(BF16)) + b_ref[...]


def _ada_mod(cond, w_ada, b_ada):
    tn = 1024
    n = N_MOD * D
    out = pl.pallas_call(
        _ada_kernel,
        out_shape=jax.ShapeDtypeStruct((DEPTH, N_GROUPS, n), F32),
        grid=(DEPTH, n // tn),
        in_specs=[
            pl.BlockSpec((N_GROUPS, D), lambda l, j: (0, 0)),
            pl.BlockSpec((None, D, tn), lambda l, j: (l, 0, j)),
            pl.BlockSpec((None, 1, tn), lambda l, j: (l, 0, j)),
        ],
        out_specs=pl.BlockSpec((None, N_GROUPS, tn), lambda l, j: (l, 0, j)),
        compiler_params=_cparams(("parallel", "parallel")),
        name="ada_mod",
    )(cond, w_ada, b_ada.reshape(DEPTH, 1, n))
    return out.reshape(DEPTH, N_GROUPS, N_MOD, 1, D)


def _mod_spec(layer, j, tm, row_off):
    return pl.BlockSpec(
        (None, None, None, 1, D),
        lambda i, n: (layer, _group_of_row((i + row_off) * tm), j, 0, 0))


def _ffn_kernel(*refs, convert_next):
    if convert_next:
        (x_ref, g_ref, sh_ref, sc_ref, gt_ref, wg_ref, wu_ref, wo_ref, nin_ref, nout_ref,
         o_ref, cin_ref, cout_ref, h_ref, gm_ref) = refs
    else:
        x_ref, g_ref, sh_ref, sc_ref, gt_ref, wg_ref, wu_ref, wo_ref, o_ref, h_ref, gm_ref = refs
    f = pl.program_id(1)
    tm = x_ref.shape[0]

    @pl.when(f == 0)
    def _():
        gm_ref[...] = g_ref[...] * (1.0 + sc_ref[...])

        def rows_fn(rows):
            h_ref[rows, :] = (_rms(x_ref[rows, :]) * gm_ref[...] + sh_ref[...]).astype(BF16)
            o_ref[rows, :] = jnp.zeros((ROW_CHUNK, D), F32)

        _for_row_chunks(tm, rows_fn)

    for r in range(0, tm, FFN_SUB):
        h = h_ref[r:r + FFN_SUB, :]
        a = _silu(_dot(h, wg_ref[...])) * _dot(h, wu_ref[...])
        o_ref[r:r + FFN_SUB, :] += _dot(a.astype(BF16), wo_ref[...])

    @pl.when(f == pl.num_programs(1) - 1)
    def _():
        o_ref[...] = x_ref[...] + 0.5 * gt_ref[...] * o_ref[...]

    if convert_next:
        cin_ref[...] = nin_ref[...].astype(BF16)
        cout_ref[...] = nout_ref[...].astype(BF16)


def _ffn(x, mod, layer, half, w_in, w_out, g, nxt=None, *, tm=1024):
    tf = FFN_TF
    nf = D_FF // tf
    j = 2 * half
    in_specs = [
        pl.BlockSpec((tm, D), lambda i, f: (i, 0)),
        pl.BlockSpec((1, D), lambda i, f: (0, 0)),
        _mod_spec(layer, 3 * j, tm, 0),
        _mod_spec(layer, 3 * j + 1, tm, 0),
        _mod_spec(layer, 3 * j + 2, tm, 0),
        pl.BlockSpec((D, tf), lambda i, f: (0, f)),
        pl.BlockSpec((D, tf), lambda i, f: (0, nf + f)),
        pl.BlockSpec((tf, D), lambda i, f: (f, 0)),
    ]
    args = [x, g.reshape(1, D), mod, mod, mod, w_in, w_in, w_out]
    out_shape = [jax.ShapeDtypeStruct((N_TOK, D), F32)]
    out_specs = [pl.BlockSpec((tm, D), lambda i, f: (i, 0))]
    if nxt is not None:
        nw_in, nw_out, nl, nh = nxt
        steps = (N_TOK // tm) * nf
        n_in, n_out = D // FFN_CVT_IN_ROWS, D_FF // FFN_CVT_OUT_ROWS
        assert n_in <= steps and n_out <= steps

        def slab(last):
            return lambda i, f: jnp.minimum(i * nf + f, last)

        s_in, s_out = slab(n_in - 1), slab(n_out - 1)
        in_specs += [pl.BlockSpec((None, None, FFN_CVT_IN_ROWS, 2 * D_FF), lambda i, f: (nl, nh, s_in(i, f), 0)),
                     pl.BlockSpec((None, None, FFN_CVT_OUT_ROWS, D), lambda i, f: (nl, nh, s_out(i, f), 0))]
        args += [nw_in, nw_out]
        out_shape += [jax.ShapeDtypeStruct((D, 2 * D_FF), BF16), jax.ShapeDtypeStruct((D_FF, D), BF16)]
        out_specs += [pl.BlockSpec((FFN_CVT_IN_ROWS, 2 * D_FF), lambda i, f: (s_in(i, f), 0)),
                      pl.BlockSpec((FFN_CVT_OUT_ROWS, D), lambda i, f: (s_out(i, f), 0))]
    outs = pl.pallas_call(
        functools.partial(_ffn_kernel, convert_next=nxt is not None),
        out_shape=out_shape,
        grid=(N_TOK // tm, nf),
        in_specs=in_specs,
        out_specs=out_specs,
        scratch_shapes=[pltpu.VMEM((tm, D), BF16), pltpu.VMEM((1, D), F32)],
        compiler_params=_cparams(("arbitrary", "arbitrary")),
        name="ffn_half",
    )(*args)
    return outs[0] if nxt is None else (outs[0], (outs[1], outs[2]))


def _mm_kernel(*refs, pro, epi, split, n_outs, tiles_per_out):
    it = iter(refs)
    x_ref = next(it)
    if pro == "cast" and split is not None:
        x2_ref = next(it)
    if pro == "gla":
        x2_ref, gin_ref = next(it), next(it)
    if pro in ("ada", "rms", "gla"):
        g_ref = next(it)
    if pro == "ada":
        sh_ref, sc_ref = next(it), next(it)
    w_ref = next(it)
    if epi == "resid":
        res_ref, gt_ref = next(it), next(it)
    o_refs = [next(it) for _ in range(n_outs)]
    if pro != "cast":
        h_ref = next(it)
    if pro == "ada":
        gm_ref = next(it)

    if pro != "cast":
        @pl.when(pl.program_id(1) == 0)
        def _():
            if pro == "ada":
                gm_ref[...] = g_ref[...] * (1.0 + sc_ref[...])

            def rows_fn(rows):
                if pro == "ada":
                    h = _rms(x_ref[rows, :]) * gm_ref[...] + sh_ref[...]
                elif pro == "rms":
                    h = _rms(x_ref[rows, :]) * g_ref[...]
                else:
                    o = x_ref[rows, :] + x2_ref[rows, :]
                    parts = [_rms(o[:, k * GLA_DV:(k + 1) * GLA_DV]) * g_ref[...] for k in range(GLA_HEADS)]
                    h = jnp.concatenate(parts, axis=-1) * _silu(gin_ref[rows, :])
                h_ref[rows, :] = h.astype(BF16)

            _for_row_chunks(x_ref.shape[0], rows_fn)

    def finish(lhs_ref):
        y = _dot(lhs_ref[...], w_ref[...])
        if epi == "resid":
            y = res_ref[...] + gt_ref[...] * y
        if n_outs == 1:
            o_refs[0][...] = y.astype(o_refs[0].dtype)
        else:
            for k, o_ref in enumerate(o_refs):
                @pl.when(pl.program_id(1) // tiles_per_out == k)
                def _(o_ref=o_ref):
                    o_ref[...] = y.astype(o_ref.dtype)

    if pro != "cast":
        finish(h_ref)
    elif split is None:
        finish(x_ref)
    else:
        pl.when(pl.program_id(0) < split)(lambda: finish(x_ref))
        pl.when(pl.program_id(0) >= split)(lambda: finish(x2_ref))


def _mm(x, w, *, rows, row_off=0, xcol=0, n_outs=1, pro, epi="plain", out_dtype=F32, tm=1024, tn=1024,
        g=None, mod=None, layer=None, jmod=None, x2=None, gin=None, gin_col=0, res=None):
    kdim, n = w.shape
    tn = min(tn, n // n_outs)
    tiles_per_out = n // n_outs // tn
    assert rows % tm == 0 and n % (n_outs * tn) == 0 and (n_outs == 1 or epi == "plain")
    split = None
    if pro == "cast" and x2 is not None:
        split = x.shape[0] // tm
        xspec = pl.BlockSpec((tm, kdim), lambda i, j: (jnp.minimum(i, split - 1), 0))
        args, specs = [x, x2], [xspec, pl.BlockSpec((tm, kdim), lambda i, j: (jnp.maximum(i - split, 0), 0))]
    else:
        xspec = pl.BlockSpec((tm, kdim), lambda i, j: (i + row_off, xcol))
        args, specs = [x], [xspec]
    if pro == "gla":
        args += [x2, gin]
        specs += [xspec, pl.BlockSpec((tm, kdim), lambda i, j: (i + row_off, gin_col))]
    if pro in ("ada", "rms", "gla"):
        args.append(g.reshape(1, -1))
        specs.append(pl.BlockSpec((1, g.shape[-1]), lambda i, j: (0, 0)))
    if pro == "ada":
        args += [mod, mod]
        specs += [_mod_spec(layer, 3 * jmod, tm, row_off), _mod_spec(layer, 3 * jmod + 1, tm, row_off)]
    args.append(w)
    specs.append(pl.BlockSpec((kdim, tn), lambda i, j: (0, j)))
    if epi == "resid":
        args += [res, mod]
        specs += [
            pl.BlockSpec((tm, tn), lambda i, j: (i + row_off, j)),
            pl.BlockSpec((None, None, None, 1, tn),
                         lambda i, j: (layer, _group_of_row((i + row_off) * tm), 3 * jmod + 2, 0, j)),
        ]
    scratch = []
    if pro != "cast":
        scratch.append(pltpu.VMEM((tm, kdim), BF16))
    if pro == "ada":
        scratch.append(pltpu.VMEM((1, kdim), F32))
    out_specs = [pl.BlockSpec((tm, tn), functools.partial(
        lambda i, j, k: (i, jnp.clip(j - k * tiles_per_out, 0, tiles_per_out - 1)), k=k)) for k in range(n_outs)]
    outs = pl.pallas_call(
        functools.partial(_mm_kernel, pro=pro, epi=epi, split=split, n_outs=n_outs, tiles_per_out=tiles_per_out),
        out_shape=[jax.ShapeDtypeStruct((rows, n // n_outs), out_dtype)] * n_outs,
        grid=(rows // tm, n // tn),
        in_specs=specs,
        out_specs=out_specs,
        scratch_shapes=scratch,
        compiler_params=_cparams(("parallel", "arbitrary")),
        name="proj_" + pro + "_" + epi,
    )(*args)
    return outs[0] if n_outs == 1 else outs


def _rmsnorm_kernel(x_ref, g_ref, o_ref):
    o_ref[...] = _rms(x_ref[...]) * g_ref[...]


def _rmsnorm(x, g, *, rows, row_off=0, xcol=0, tm=512):
    width = g.shape[-1]
    return pl.pallas_call(
        _rmsnorm_kernel,
        out_shape=jax.ShapeDtypeStruct((rows, width), F32),
        grid=(rows // tm,),
        in_specs=[pl.BlockSpec((tm, width), lambda i: (i + row_off, xcol)),
                  pl.BlockSpec((1, width), lambda i: (0, 0))],
        out_specs=pl.BlockSpec((tm, width), lambda i: (i, 0)),
        compiler_params=_cparams(("parallel",)),
        name="rmsnorm",
    )(x, g.reshape(1, width))


def _rope(x, cos, sin):
    width = x.shape[-1]
    lane = lax.broadcasted_iota(jnp.int32, x.shape, 1)
    up = pltpu.roll(x, width - 16, 1)
    down = pltpu.roll(x, 16, 1)
    swapped = jnp.where((lane & 31) < 16, up, down)
    return x * cos + swapped * sin


def _rope_keys_kernel(x_ref, cos_ref, sin_ref, o_ref):
    o_ref[...] = _rope(x_ref[...], cos_ref[...], sin_ref[...]).astype(o_ref.dtype)


def _rope_keys(down, rope_tabs, *, tm=1024):
    cos, sin = rope_tabs
    per_seq = DEC_SEQ // tm
    return pl.pallas_call(
        _rope_keys_kernel,
        out_shape=jax.ShapeDtypeStruct((N_LAT, 128), BF16),
        grid=(N_LAT // tm,),
        in_specs=[pl.BlockSpec((tm, 128), lambda i: (N_CTX // tm + i, MLA_Q_LORA // 128)),
                  pl.BlockSpec((tm, 128), lambda i: (i % per_seq, 0)),
                  pl.BlockSpec((tm, 128), lambda i: (i % per_seq, 0))],
        out_specs=pl.BlockSpec((tm, 128), lambda i: (i, 0)),
        compiler_params=_cparams(("parallel",)),
        name="rope_keys",
    )(down, cos, sin)


def _attn_kernel(*refs, nseg, has_r, rope, scale, dh, dv, pps):
    it = iter(refs)
    q_ref = next(it)
    qr_ref = next(it) if has_r else None
    segs = []
    for _ in range(nseg):
        k_ref = next(it)
        kr_ref = next(it) if has_r else None
        v_ref = next(it)
        segs.append((k_ref, kr_ref, v_ref))
    if rope:
        cq_ref, sq_ref = next(it), next(it)
    o_ref = next(it)

    if has_r:
        qr = qr_ref[...].astype(F32)
        if rope:
            qr = _rope(qr, cq_ref[...], sq_ref[...])
        qrs = [qr[:, h * MLA_ROPE:(h + 1) * MLA_ROPE].astype(BF16) for h in range(2 * pps)]
        krs = [kr_ref[:, :MLA_ROPE].astype(BF16) for _, kr_ref, _ in segs]

    for pair in range(pps):
        scores = []
        for h in (2 * pair, 2 * pair + 1):
            q = q_ref[:, h * dh:(h + 1) * dh].astype(BF16)
            ss = []
            for si, (k_ref, _, _) in enumerate(segs):
                s = _dot_nt(q, k_ref[:, h * dh:(h + 1) * dh].astype(BF16))
                if has_r:
                    s = s + _dot_nt(qrs[h], krs[si])
                ss.append(s * (scale * LOG2E))
            scores.append(ss)
        for h, ss in zip((2 * pair, 2 * pair + 1), scores):
            m = ss[0].max(axis=-1, keepdims=True)
            for s in ss[1:]:
                m = jnp.maximum(m, s.max(axis=-1, keepdims=True))
            acc = None
            den = None
            for s, (_, _, v_ref) in zip(ss, segs):
                e = jnp.exp2(s - m)
                d = e.sum(axis=-1, keepdims=True)
                pv = _dot(e.astype(BF16), v_ref[:, h * dv:(h + 1) * dv].astype(BF16))
                acc = pv if acc is None else acc + pv
                den = d if den is None else den + d
            o_ref[:, h * dv:(h + 1) * dv] = (acc / den).astype(o_ref.dtype)


def _attention(q, segs, *, nb, lq, tq, q_row0, q_col, scale, qr=None, rope_tabs=None, pps=1, dh=128, dv=128):
    npairs = 8
    ng = npairs // pps
    nq = lq // tq
    has_r = qr is not None
    q_blk0 = q_row0 // tq
    assert q_col % pps == 0 and (rope_tabs is None or pps == 1)
    args = [q]
    specs = [pl.BlockSpec((tq, 2 * dh * pps), lambda b, p, t: (q_blk0 + b * nq + t, q_col // pps + p))]
    if has_r:
        qr_arr, qr_col = qr
        assert qr_col % pps == 0
        args.append(qr_arr)
        specs.append(pl.BlockSpec((tq, 2 * MLA_ROPE * pps),
                                  lambda b, p, t: (q_blk0 + b * nq + t, qr_col // pps + p)))
    for sg in segs:
        lk = sg["lk"]
        k_arr, k_row0, k_col = sg["k"]
        assert k_col % pps == 0
        args.append(k_arr)
        specs.append(pl.BlockSpec((lk, 2 * dh * pps), functools.partial(
            lambda b, p, t, r0, c0: (r0 + b, c0 + p), r0=k_row0 // lk, c0=k_col // pps)))
        if has_r:
            kr_arr, kr_row0, kr_col, kr_w = sg["kr"]
            args.append(kr_arr)
            specs.append(pl.BlockSpec((lk, kr_w), functools.partial(
                lambda b, p, t, r0, c0: (r0 + b, c0), r0=kr_row0 // lk, c0=kr_col)))
        v_arr, v_row0, v_col = sg["v"]
        assert v_col % pps == 0
        args.append(v_arr)
        specs.append(pl.BlockSpec((lk, 2 * dv * pps), functools.partial(
            lambda b, p, t, r0, c0: (r0 + b, c0 + p), r0=v_row0 // lk, c0=v_col // pps)))
    if rope_tabs is not None:
        cos, sin = rope_tabs
        args += [cos, sin]
        specs += [pl.BlockSpec((tq, 128), lambda b, p, t: (t, 0)),
                  pl.BlockSpec((tq, 128), lambda b, p, t: (t, 0))]
    return pl.pallas_call(
        functools.partial(_attn_kernel, nseg=len(segs), has_r=has_r, rope=rope_tabs is not None,
                          scale=scale, dh=dh, dv=dv, pps=pps),
        out_shape=jax.ShapeDtypeStruct((nb * lq, npairs * 2 * dv), BF16),
        grid=(nb, ng, nq),
        in_specs=specs,
        out_specs=pl.BlockSpec((tq, 2 * dv * pps), lambda b, p, t: (b * nq + t, p)),
        compiler_params=_cparams(("parallel", "parallel", "arbitrary")),
        name="attention",
    )(*args)


def _na_kernel(q_ref, k_ref, v_ref, kc_ref, vc_ref, t2_ref, o_ref, bias_ref, cap_ref):
    r0 = pl.program_id(1) * NA_QR
    ks = jnp.clip(r0 - NA_WIN_ROWS // 2, 0, LAT_ROWS - NA_KR)
    nq, nk = NA_QR * GRID_W, NA_KR * GRID_W
    scale = NA_DH ** -0.5 * LOG2E

    @pl.when(pl.program_id(2) == 0)
    def _():
        row = lax.broadcasted_iota(jnp.int32, (nq, nk), 0)
        lane = lax.broadcasted_iota(jnp.int32, (nq, nk), 1)
        qc = row & (GRID_W - 1)
        kc = lane & (GRID_W - 1)
        rs = jnp.clip(r0 + (row >> LOG2_GRID_W) - NA_WIN_ROWS // 2, 0, LAT_ROWS - NA_WIN_ROWS)
        kr = ks + (lane >> LOG2_GRID_W)
        cs = jnp.clip(qc - NA_WIN_COLS // 2, 0, GRID_W - NA_WIN_COLS)
        ok = (kr >= rs) & (kr < rs + NA_WIN_ROWS) & (kc >= cs) & (kc < cs + NA_WIN_COLS)
        cap_ref[...] = jnp.where(ok, jnp.inf, NEG_INF)
        for hh in range(2):
            for i in range(NA_QR):
                for jp in range(NA_KR // 2):
                    e = jnp.clip(ks - r0 + 2 * jp - i + NA_WIN_ROWS, 0, 2 * NA_WIN_ROWS - 1)
                    bias_ref[hh, i * GRID_W:(i + 1) * GRID_W, jp * 128:(jp + 1) * 128] = t2_ref[hh, e] * LOG2E

    start = pl.multiple_of(ks * GRID_W, GRID_W)
    kw = k_ref[pl.ds(start, nk), :]
    vw = v_ref[pl.ds(start, nk), :]
    cap = cap_ref[...]
    for hh in range(2):
        sl = slice(hh * NA_DH, (hh + 1) * NA_DH)
        q = q_ref[:, sl]
        s_loc = jnp.minimum(_dot_nt(q, kw[:, sl]) * scale + bias_ref[hh], cap)
        s_ctx = _dot_nt(q, kc_ref[:, sl]) * scale
        m = jnp.maximum(s_loc.max(axis=-1, keepdims=True), s_ctx.max(axis=-1, keepdims=True))
        e_loc = jnp.exp2(s_loc - m)
        e_ctx = jnp.exp2(s_ctx - m)
        den = e_loc.sum(axis=-1, keepdims=True) + e_ctx.sum(axis=-1, keepdims=True)
        acc = _dot(e_loc.astype(BF16), vw[:, sl]) + _dot(e_ctx.astype(BF16), vc_ref[:, sl])
        o_ref[:, sl] = (acc / den).astype(o_ref.dtype)


def _na_bias_table(rpb):
    col = jnp.arange(GRID_W)
    col_off = jnp.clip(col[None, :] - col[:, None] + NA_WIN_COLS - 1, 0, 2 * NA_WIN_COLS - 2)
    e = jnp.arange(2 * NA_WIN_ROWS)
    dr = jnp.clip(jnp.stack([e - 1, e], axis=1), 0, 2 * NA_WIN_ROWS - 2)
    tb = rpb[:, dr][:, :, :, col_off]
    return jnp.transpose(tb, (0, 1, 3, 2, 4)).reshape(NA_HEADS, 2 * NA_WIN_ROWS, GRID_W, 2 * GRID_W)


def _na_latent(qkv_l, kc, vc, t2):
    npairs = NA_HEADS // 2
    w2 = 2 * NA_DH
    nrb = LAT_ROWS // NA_QR
    nq = NA_QR * GRID_W
    return pl.pallas_call(
        _na_kernel,
        out_shape=jax.ShapeDtypeStruct((N_LAT, D), BF16),
        grid=(npairs, nrb, DEC_BATCH),
        in_specs=[
            pl.BlockSpec((nq, w2), lambda p, r, b: (b * nrb + r, p)),
            pl.BlockSpec((DEC_SEQ, w2), lambda p, r, b: (b, npairs + p)),
            pl.BlockSpec((DEC_SEQ, w2), lambda p, r, b: (b, 2 * npairs + p)),
            pl.BlockSpec((PAST, w2), lambda p, r, b: (b, p)),
            pl.BlockSpec((PAST, w2), lambda p, r, b: (b, p)),
            pl.BlockSpec((2, 2 * NA_WIN_ROWS, GRID_W, 2 * GRID_W), lambda p, r, b: (p, 0, 0, 0)),
        ],
        out_specs=pl.BlockSpec((nq, w2), lambda p, r, b: (b * nrb + r, p)),
        scratch_shapes=[pltpu.VMEM((2, nq, NA_KR * GRID_W), F32), pltpu.VMEM((nq, NA_KR * GRID_W), F32)],
        compiler_params=_cparams(("parallel", "parallel", "arbitrary")),
        name="na_latent",
    )(qkv_l, qkv_l, qkv_l, kc, vc, t2)


def _gla_pos(t):
    is_ctx = t < GLA_CTX_STEPS
    u = jnp.maximum(t - GLA_CTX_STEPS, 0)
    seq = jnp.where(is_ctx, t // GLA_CTX_CHUNKS, u // GLA_LAT_CHUNKS)
    n = jnp.where(is_ctx, t % GLA_CTX_CHUNKS, u % GLA_LAT_CHUNKS)
    return is_ctx, seq, n


def _gla_bwd_chunk(t):
    is_ctx, _, n = _gla_pos(t)
    return t + jnp.where(is_ctx, GLA_CTX_CHUNKS, GLA_LAT_CHUNKS) - 1 - 2 * n


def _gla_kernel(qf_ref, kf_ref, vf_ref, gdf_ref, qb_ref, kb_ref, vb_ref, gdb_ref, wgu_ref, bg_ref, s0_ref,
                of_ref, ob_ref, fin_ref, st_ref):
    is_ctx, _, n = _gla_pos(pl.program_id(0))
    is_lat = jnp.logical_not(is_ctx)

    @pl.when(jnp.logical_and(n == 0, is_ctx))
    def _():
        st_ref[...] = jnp.zeros_like(st_ref)

    @pl.when(jnp.logical_and(n == 0, is_lat))
    def _():
        st_ref[...] = s0_ref[...]

    ri = lax.broadcasted_iota(jnp.int32, (GLA_CHUNK, GLA_CHUNK), 0)
    ci = lax.broadcasted_iota(jnp.int32, (GLA_CHUNK, GLA_CHUNK), 1)
    streams = ((qf_ref, kf_ref, vf_ref, gdf_ref, of_ref), (qb_ref, kb_ref, vb_ref, gdb_ref, ob_ref))
    for d, (q_ref, k_ref, v_ref, gd_ref, o_ref) in enumerate(streams):
        keep = (ci <= ri) if d == 0 else (ci >= ri)
        tri = jnp.where(keep, 1.0, 0.0).astype(BF16)
        pre = _dot(gd_ref[...].astype(BF16), wgu_ref[d]) + bg_ref[d]
        la = (jnp.minimum(pre, 0.0) - jnp.log1p(jnp.exp(-jnp.abs(pre)))) / GLA_TAU
        hi = la.astype(BF16)
        r1 = la - hi.astype(F32)
        mid = r1.astype(BF16)
        lo = (r1 - mid.astype(F32)).astype(BF16)
        cum = _dot(tri, hi) + _dot(tri, mid) + _dot(tri, lo)
        tot = jnp.sum(la, axis=0, keepdims=True)
        k = k_ref[...]
        q_dec = (q_ref[...] * (GLA_DK ** -0.5) * jnp.exp(cum)).astype(BF16)
        k_inv = (k * jnp.exp(-cum)).astype(BF16)
        k_end = (k * jnp.exp(tot - cum)).astype(BF16)
        dec = jnp.exp(tot)
        v = v_ref[...].astype(BF16)
        for h in range(GLA_HEADS):
            ks = slice(h * GLA_DK, (h + 1) * GLA_DK)
            vs = slice(h * GLA_DV, (h + 1) * GLA_DV)
            att = jnp.where(keep, _dot_nt(q_dec[:, ks], k_inv[:, ks]), 0.0)
            st = st_ref[d, h]
            o_ref[:, vs] = _dot(att.astype(BF16), v[:, vs]) + _dot_nt(q_dec[:, ks], st.astype(BF16))
            st_ref[d, h] = st * dec[:, ks] + _dot_tn(v[:, vs], k_end[:, ks])

    @pl.when(jnp.logical_and(is_ctx, n == GLA_CTX_CHUNKS - 1))
    def _():
        fin_ref[...] = st_ref[...]


def _gla_scan(proj, wgu, bg, s0t):
    hk = GLA_HEADS * GLA_DK
    hv = GLA_HEADS * GLA_DV
    gd_col = (GLA_PROJ_N - 128) // 128

    def fwd(c):
        return lambda t: (t, c)

    def bwd(c):
        return lambda t: (_gla_bwd_chunk(t), c)

    in_specs = []
    for ix in (fwd, bwd):
        in_specs += [pl.BlockSpec((GLA_CHUNK, hk), ix(0)), pl.BlockSpec((GLA_CHUNK, hk), ix(1)),
                     pl.BlockSpec((GLA_CHUNK, hv), ix(1)), pl.BlockSpec((GLA_CHUNK, 128), ix(gd_col))]
    st_block = (None, 2, GLA_HEADS, GLA_DV, GLA_DK)
    in_specs += [
        pl.BlockSpec((2, 128, hk), lambda t: (0, 0, 0)),
        pl.BlockSpec((2, 1, hk), lambda t: (0, 0, 0)),
        pl.BlockSpec(st_block, lambda t: (jnp.where(_gla_pos(t)[0], 0, _gla_pos(t)[1]), 0, 0, 0, 0)),
    ]
    o_shape = jax.ShapeDtypeStruct((N_TOK, hv), F32)
    return pl.pallas_call(
        _gla_kernel,
        out_shape=(o_shape, o_shape, jax.ShapeDtypeStruct((BATCH, 2, GLA_HEADS, GLA_DV, GLA_DK), F32)),
        grid=(GLA_STEPS,),
        in_specs=in_specs,
        out_specs=(
            pl.BlockSpec((GLA_CHUNK, hv), lambda t: (t, 0)),
            pl.BlockSpec((GLA_CHUNK, hv), lambda t: (_gla_bwd_chunk(t), 0)),
            pl.BlockSpec(st_block, lambda t: (jnp.where(_gla_pos(t)[0], _gla_pos(t)[1], BATCH - 1), 0, 0, 0, 0)),
        ),
        scratch_shapes=[pltpu.VMEM((2, GLA_HEADS, GLA_DV, GLA_DK), F32)],
        compiler_params=_cparams(("arbitrary",)),
        name="gla_scan",
    )(*([proj] * 8), wgu, bg, s0t)


def _rope_tables():
    t = jnp.arange(DEC_SEQ)
    d = MLA_ROPE // 2
    inv = ROPE_THETA ** (-jnp.arange(0, d, 2, dtype=F32) / d)
    ang_r = (t // GRID_W).astype(F32)[:, None] * inv[None]
    ang_c = (t % GRID_W).astype(F32)[:, None] * inv[None]
    cos = jnp.concatenate([jnp.cos(ang_r)] * 2 + [jnp.cos(ang_c)] * 2, axis=-1)
    sin = jnp.concatenate([-jnp.sin(ang_r), jnp.sin(ang_r), -jnp.sin(ang_c), jnp.sin(ang_c)], axis=-1)
    return jnp.concatenate([cos, cos], axis=-1), jnp.concatenate([sin, sin], axis=-1)


def _mixer_na(x, mod, layer, g, w_qkv, w_o, rpb, cache_k, cache_v):
    w_qkv = w_qkv.astype(BF16)
    ada = dict(pro="ada", g=g, mod=mod, layer=layer, jmod=1)
    q_c, k_c, v_c = _mm(x, w_qkv, rows=N_CTX, n_outs=3, out_dtype=F32, tm=512, **ada)
    qkv_l = _mm(x, w_qkv, rows=N_LAT, row_off=N_CTX // 1024, out_dtype=BF16, tn=1536, **ada)
    o_c = _attention(q_c, [dict(k=(k_c, 0, 0), v=(v_c, 0, 0), lk=SEQ)],
                     nb=BATCH, lq=SEQ, tq=SEQ, q_row0=0, q_col=0, scale=NA_DH ** -0.5, pps=CTX_PPS)
    kc = cache_k.reshape(DEC_BATCH * PAST, D).astype(BF16)
    vc = cache_v.reshape(DEC_BATCH * PAST, D).astype(BF16)
    o_l = _na_latent(qkv_l, kc, vc, _na_bias_table(rpb))
    x = _mm(o_c, w_o.astype(BF16), rows=N_TOK, pro="cast", x2=o_l, epi="resid", res=x, mod=mod, layer=layer,
            jmod=1, tm=512, tn=2048)
    return x, k_c.reshape(BATCH, SEQ, NA_HEADS, NA_DH), v_c.reshape(BATCH, SEQ, NA_HEADS, NA_DH)


def _mixer_mla(x, mod, layer, g, w_down, g_q, w_uq, g_kv, w_ukv, w_o, cache_ckv, cache_kr):
    hq = MLA_NOPE + MLA_ROPE
    wd = jnp.concatenate([
        w_down[:, :MLA_Q_LORA],
        w_down[:, MLA_Q_LORA + MLA_KV_LORA:],
        jnp.zeros((D, 256 - MLA_ROPE), F32),
        w_down[:, MLA_Q_LORA:MLA_Q_LORA + MLA_KV_LORA]], axis=1).astype(BF16)
    wq = w_uq.reshape(MLA_Q_LORA, MLA_HEADS, hq)
    wq = jnp.concatenate([wq[:, :, :MLA_NOPE].reshape(MLA_Q_LORA, -1),
                          wq[:, :, MLA_NOPE:].reshape(MLA_Q_LORA, -1)], axis=1).astype(BF16)
    wkv = w_ukv.reshape(MLA_KV_LORA, MLA_HEADS, MLA_NOPE + MLA_V)
    wkv = jnp.concatenate([wkv[:, :, :MLA_NOPE].reshape(MLA_KV_LORA, -1),
                           wkv[:, :, MLA_NOPE:].reshape(MLA_KV_LORA, -1)], axis=1).astype(BF16)

    down = _mm(x, wd, rows=N_TOK, pro="ada", g=g, mod=mod, layer=layer, jmod=1, out_dtype=F32, tn=MLA_DOWN_N)
    q = _mm(down, wq, rows=N_TOK, pro="rms", g=g_q, out_dtype=F32, tn=1536)
    kv = _mm(down, wkv, rows=N_TOK, xcol=2, pro="rms", g=g_kv, out_dtype=BF16, tn=2048)
    kv_cache = _mm(cache_ckv.reshape(DEC_BATCH * PAST, MLA_KV_LORA).astype(BF16), wkv,
                   rows=DEC_BATCH * PAST, pro="cast", out_dtype=BF16, tn=2048)
    ckv_c = _rmsnorm(down, g_kv, rows=N_CTX, xcol=2)
    kr_c = down[:N_CTX, MLA_Q_LORA:MLA_Q_LORA + MLA_ROPE]

    npairs = MLA_HEADS // 2
    kr_col = MLA_Q_LORA // 128
    o_c = _attention(q, [dict(k=(kv, 0, 0), v=(kv, 0, npairs), kr=(down, 0, kr_col, 128), lk=SEQ)],
                     nb=BATCH, lq=SEQ, tq=SEQ, q_row0=0, q_col=0, qr=(q, 2 * npairs), scale=MLA_SCALE, pps=CTX_PPS)
    tabs = _rope_tables()
    kr_l = _rope_keys(down, tabs)
    o_l = _attention(
        q,
        [dict(k=(kv, N_CTX, 0), v=(kv, N_CTX, npairs), kr=(kr_l, 0, 0, 128), lk=DEC_SEQ),
         dict(k=(kv_cache, 0, 0), v=(kv_cache, 0, npairs),
              kr=(cache_kr.reshape(DEC_BATCH * PAST, MLA_ROPE), 0, 0, MLA_ROPE), lk=PAST)],
        nb=DEC_BATCH, lq=DEC_SEQ, tq=512, q_row0=N_CTX, q_col=0, qr=(q, 2 * npairs), scale=MLA_SCALE,
        rope_tabs=tabs)
    x = _mm(o_c, w_o.astype(BF16), rows=N_TOK, pro="cast", x2=o_l, epi="resid", res=x, mod=mod, layer=layer,
            jmod=1, tm=512, tn=2048)
    return x, ckv_c.reshape(BATCH, SEQ, MLA_KV_LORA), kr_c.reshape(BATCH, SEQ, MLA_ROPE)


def _mixer_gla(x, mod, layer, g, w_in, w_gd, w_gu, b_g, g_norm, w_o, state):
    w_cat = jnp.concatenate([w_in, w_gd[0], w_gd[1], jnp.zeros((D, 128 - 2 * GLA_RANK), F32)], axis=1).astype(BF16)
    proj = _mm(x, w_cat, rows=N_TOK, pro="ada", g=g, mod=mod, layer=layer, jmod=1, out_dtype=F32, tn=896)
    wgu = jnp.zeros((2, 128, GLA_HEADS * GLA_DK), F32)
    wgu = wgu.at[0, :GLA_RANK].set(w_gu[0]).at[1, GLA_RANK:2 * GLA_RANK].set(w_gu[1]).astype(BF16)
    bg = b_g.reshape(2, 1, GLA_HEADS * GLA_DK)
    o_f, o_b, st_c = _gla_scan(proj, wgu, bg, jnp.swapaxes(state, -1, -2))
    x = _mm(o_f, w_o.astype(BF16), rows=N_TOK, pro="gla", x2=o_b, gin=proj, gin_col=2, g=g_norm,
            epi="resid", res=x, mod=mod, layer=layer, jmod=1, tm=256, tn=2048)
    return x, jnp.swapaxes(st_c, -1, -2)


def kernel(x_prompt, x_sample, cache_na_k, cache_na_v, cache_mla_ckv, cache_mla_krope, state_gla, c, c_ctx, norm_g, w_ada, b_ada, w_ffn_in, w_ffn_out, w_na_qkv, w_na_o, na_rpb, w_mla_down, g_mla_q, w_mla_uq, g_mla_kv, w_mla_ukv, w_mla_o, w_gla_in, w_gla_gate_down, w_gla_gate_up, b_gla_gate, g_gla_norm, w_gla_o, final_norm_g):
    x = jnp.concatenate([x_prompt.reshape(N_CTX, D), x_sample.reshape(N_LAT, D)], axis=0)
    cond = jnp.concatenate([c_ctx[None], c, jnp.zeros((N_GROUPS - 1 - DEC_BATCH, D), F32)], axis=0)
    mod = _ada_mod(cond, w_ada, b_ada)
    w_ffn = (w_ffn_in[0, 0].astype(BF16), w_ffn_out[0, 0].astype(BF16))

    def ffn(x, w_ffn, layer, half):
        last = layer == DEPTH - 1 and half == 1
        nxt = None if last else (w_ffn_in, w_ffn_out, layer + half, 1 - half)
        out = _ffn(x, mod, layer, half, w_ffn[0], w_ffn[1], norm_g[layer, 2 * half], nxt)
        return (out, None) if last else out

    na_k, na_v, ckv, krope, gla_st = [], [], [], [], []
    for i in range(DEPTH):
        kind, slot = i % 3, i // 3
        x, w_ffn = ffn(x, w_ffn, i, 0)
        if kind == 0:
            x, k_c, v_c = _mixer_na(x, mod, i, norm_g[i, 1], w_na_qkv[slot], w_na_o[slot], na_rpb[slot],
                                    cache_na_k[:, slot], cache_na_v[:, slot])
            na_k.append(k_c)
            na_v.append(v_c)
        elif kind == 1:
            x, ckv_c, kr_c = _mixer_mla(x, mod, i, norm_g[i, 1], w_mla_down[slot], g_mla_q[slot], w_mla_uq[slot],
                                        g_mla_kv[slot], w_mla_ukv[slot], w_mla_o[slot],
                                        cache_mla_ckv[:, slot], cache_mla_krope[:, slot])
            ckv.append(ckv_c)
            krope.append(kr_c)
        else:
            x, st = _mixer_gla(x, mod, i, norm_g[i, 1], w_gla_in[slot], w_gla_gate_down[slot],
                               w_gla_gate_up[slot], b_gla_gate[slot], g_gla_norm[slot], w_gla_o[slot],
                               state_gla[:, slot])
            gla_st.append(st)
        x, w_ffn = ffn(x, w_ffn, i, 1)

    y_prompt = _rmsnorm(x, final_norm_g, rows=N_CTX).reshape(BATCH, SEQ, D)
    y_sample = _rmsnorm(x, final_norm_g, rows=N_LAT, row_off=N_CTX // 512).reshape(DEC_BATCH, DEC_SEQ, D)
    return (y_prompt, y_sample, jnp.stack(na_k, axis=1), jnp.stack(na_v, axis=1), jnp.stack(ckv, axis=1),
            jnp.stack(krope, axis=1), jnp.stack(gla_st, axis=1))
```

```python
import functools

import jax
import jax.numpy as jnp
from jax import lax
from jax.experimental import pallas as pl
from jax.experimental.pallas import tpu as pltpu

F32 = jnp.float32
BF16 = jnp.bfloat16

D = 2048
BATCH, SEQ = 16, 256
DEC_BATCH, DEC_SEQ = 4, 2048
PAST = 512
DEPTH = 4
N_MOD = 9
EPS = 1e-6
D_FF = 5632
GRID_W = 64
LOG2_GRID_W = 6
NEG_INF = -1e30
LOG2E = 1.4426950408889634

N_CTX = BATCH * SEQ
N_LAT = DEC_BATCH * DEC_SEQ
N_TOK = N_CTX + N_LAT
N_GROUPS = 8

NA_HEADS, NA_DH = 16, 128
NA_WIN_ROWS, NA_WIN_COLS = 8, 16
LAT_ROWS = DEC_SEQ // GRID_W
NA_QR = 8
NA_KR = NA_QR + NA_WIN_ROWS

MLA_HEADS = 16
MLA_Q_LORA, MLA_KV_LORA = 768, 512
MLA_NOPE, MLA_ROPE, MLA_V = 128, 64, 128
MLA_SCALE = (MLA_NOPE + MLA_ROPE) ** -0.5
ROPE_THETA = 10000.0
MLA_DOWN_N = 1536

GLA_HEADS, GLA_DK, GLA_DV = 4, 256, 512
GLA_RANK = 16
GLA_TAU = 16.0
GLA_CHUNK = 128
GLA_QKVG_N = 2 * GLA_HEADS * GLA_DK + 2 * GLA_HEADS * GLA_DV
GLA_PROJ_N = GLA_QKVG_N + 256
GLA_CTX_CHUNKS = SEQ // GLA_CHUNK
GLA_LAT_CHUNKS = DEC_SEQ // GLA_CHUNK
GLA_CTX_STEPS = BATCH * GLA_CTX_CHUNKS
GLA_STEPS = GLA_CTX_STEPS + DEC_BATCH * GLA_LAT_CHUNKS

VMEM_LIMIT = 56 * 1024 * 1024
ROW_CHUNK = 16
CTX_PPS = 8
FFN_TF = 512
FFN_SUB = 512
FFN_CVT_IN_ROWS = 16
FFN_CVT_OUT_ROWS = 64


def _cparams(sem):
    return pltpu.CompilerParams(dimension_semantics=sem, vmem_limit_bytes=VMEM_LIMIT)


def _group_of_row(r0):
    return jnp.where(r0 < N_CTX, 0, 1 + (r0 - N_CTX) // DEC_SEQ)


def _silu(x):
    return x / (1.0 + jnp.exp(-x))


def _rms(x):
    return x * lax.rsqrt(jnp.mean(x * x, axis=-1, keepdims=True) + EPS)


def _for_row_chunks(n_rows, body):
    def step(c, carry):
        body(pl.ds(pl.multiple_of(c * ROW_CHUNK, ROW_CHUNK), ROW_CHUNK))
        return carry
    lax.fori_loop(0, n_rows // ROW_CHUNK, step, 0, unroll=8)


def _dot(a, b):
    return jnp.dot(a, b, preferred_element_type=F32)


def _dot_nt(a, b):
    return lax.dot_general(a, b, (((1,), (1,)), ((), ())), preferred_element_type=F32)


def _dot_tn(a, b):
    return lax.dot_general(a, b, (((0,), (0,)), ((), ())), preferred_element_type=F32)


def _ada_kernel(c_ref, w_ref, b_ref, o_ref):
    s = _silu(c_ref[...]).astype(BF16)
    o_ref[...] = _dot(s, w_ref[...].astype(BF16)) + b_ref[...]


def _ada_mod(cond, w_ada, b_ada):
    tn = 1024
    n = N_MOD * D
    out = pl.pallas_call(
        _ada_kernel,
        out_shape=jax.ShapeDtypeStruct((DEPTH, N_GROUPS, n), F32),
        grid=(DEPTH, n // tn),
        in_specs=[
            pl.BlockSpec((N_GROUPS, D), lambda l, j: (0, 0)),
            pl.BlockSpec((None, D, tn), lambda l, j: (l, 0, j)),
            pl.BlockSpec((None, 1, tn), lambda l, j: (l, 0, j)),
        ],
        out_specs=pl.BlockSpec((None, N_GROUPS, tn), lambda l, j: (l, 0, j)),
        compiler_params=_cparams(("parallel", "parallel")),
        name="ada_mod",
    )(cond, w_ada, b_ada.reshape(DEPTH, 1, n))
    return out.reshape(DEPTH, N_GROUPS, N_MOD, 1, D)


def _mod_spec(layer, j, tm, row_off):
    return pl.BlockSpec(
        (None, None, None, 1, D),
        lambda i, n: (layer, _group_of_row((i + row_off) * tm), j, 0, 0))


def _ffn_kernel(*refs, convert_next):
    if convert_next:
        (x_ref, g_ref, sh_ref, sc_ref, gt_ref, wg_ref, wu_ref, wo_ref, nin_ref, nout_ref,
         o_ref, cin_ref, cout_ref, h_ref, gm_ref) = refs
    else:
        x_ref, g_ref, sh_ref, sc_ref, gt_ref, wg_ref, wu_ref, wo_ref, o_ref, h_ref, gm_ref = refs
    f = pl.program_id(1)
    tm = x_ref.shape[0]

    @pl.when(f == 0)
    def _():
        gm_ref[...] = g_ref[...] * (1.0 + sc_ref[...])

        def rows_fn(rows):
            h_ref[rows, :] = (_rms(x_ref[rows, :]) * gm_ref[...] + sh_ref[...]).astype(BF16)
            o_ref[rows, :] = jnp.zeros((ROW_CHUNK, D), F32)

        _for_row_chunks(tm, rows_fn)

    if convert_next:
        cin_ref[...] = nin_ref[...].astype(BF16)
        cout_ref[...] = nout_ref[...].astype(BF16)

    for r in range(0, tm, FFN_SUB):
        h = h_ref[r:r + FFN_SUB, :]
        a = _silu(_dot(h, wg_ref[...])) * _dot(h, wu_ref[...])
        o_ref[r:r + FFN_SUB, :] += _dot(a.astype(BF16), wo_ref[...])

    @pl.when(f == pl.num_programs(1) - 1)
    def _():
        o_ref[...] = x_ref[...] + 0.5 * gt_ref[...] * o_ref[...]


def _ffn(x, mod, layer, half, w_in, w_out, g, nxt=None, *, tm=1024):
    tf = FFN_TF
    nf = D_FF // tf
    j = 2 * half
    in_specs = [
        pl.BlockSpec((tm, D), lambda i, f: (i, 0)),
        pl.BlockSpec((1, D), lambda i, f: (0, 0)),
        _mod_spec(layer, 3 * j, tm, 0),
        _mod_spec(layer, 3 * j + 1, tm, 0),
        _mod_spec(layer, 3 * j + 2, tm, 0),
        pl.BlockSpec((D, tf), lambda i, f: (0, f)),
        pl.BlockSpec((D, tf), lambda i, f: (0, nf + f)),
        pl.BlockSpec((tf, D), lambda i, f: (f, 0)),
    ]
    args = [x, g.reshape(1, D), mod, mod, mod, w_in, w_in, w_out]
    out_shape = [jax.ShapeDtypeStruct((N_TOK, D), F32)]
    out_specs = [pl.BlockSpec((tm, D), lambda i, f: (i, 0))]
    if nxt is not None:
        nw_in, nw_out, nl, nh = nxt
        steps = (N_TOK // tm) * nf
        n_in, n_out = D // FFN_CVT_IN_ROWS, D_FF // FFN_CVT_OUT_ROWS
        assert n_in <= steps and n_out <= steps

        def slab(last):
            return lambda i, f: jnp.minimum(i * nf + f, last)

        s_in, s_out = slab(n_in - 1), slab(n_out - 1)
        in_specs += [pl.BlockSpec((None, None, FFN_CVT_IN_ROWS, 2 * D_FF), lambda i, f: (nl, nh, s_in(i, f), 0)),
                     pl.BlockSpec((None, None, FFN_CVT_OUT_ROWS, D), lambda i, f: (nl, nh, s_out(i, f), 0))]
        args += [nw_in, nw_out]
        out_shape += [jax.ShapeDtypeStruct((D, 2 * D_FF), BF16), jax.ShapeDtypeStruct((D_FF, D), BF16)]
        out_specs += [pl.BlockSpec((FFN_CVT_IN_ROWS, 2 * D_FF), lambda i, f: (s_in(i, f), 0)),
                      pl.BlockSpec((FFN_CVT_OUT_ROWS, D), lambda i, f: (s_out(i, f), 0))]
    outs = pl.pallas_call(
        functools.partial(_ffn_kernel, convert_next=nxt is not None),
        out_shape=out_shape,
        grid=(N_TOK // tm, nf),
        in_specs=in_specs,
        out_specs=out_specs,
        scratch_shapes=[pltpu.VMEM((tm, D), BF16), pltpu.VMEM((1, D), F32)],
        compiler_params=_cparams(("arbitrary", "arbitrary")),
        name="ffn_half",
    )(*args)
    return outs[0] if nxt is None else (outs[0], (outs[1], outs[2]))


def _mm_kernel(*refs, pro, epi, split, n_outs, tiles_per_out):
    it = iter(refs)
    x_ref = next(it)
    if pro == "cast" and split is not None:
        x2_ref = next(it)
    if pro == "gla":
        x2_ref, gin_ref = next(it), next(it)
    if pro in ("ada", "rms", "gla"):
        g_ref = next(it)
    if pro == "ada":
        sh_ref, sc_ref = next(it), next(it)
    w_ref = next(it)
    if epi == "resid":
        res_ref, gt_ref = next(it), next(it)
    o_refs = [next(it) for _ in range(n_outs)]
    if pro != "cast":
        h_ref = next(it)
    if pro == "ada":
        gm_ref = next(it)

    if pro != "cast":
        @pl.when(pl.program_id(1) == 0)
        def _():
            if pro == "ada":
                gm_ref[...] = g_ref[...] * (1.0 + sc_ref[...])

            def rows_fn(rows):
                if pro == "ada":
                    h = _rms(x_ref[rows, :]) * gm_ref[...] + sh_ref[...]
                elif pro == "rms":
                    h = _rms(x_ref[rows, :]) * g_ref[...]
                else:
                    o = x_ref[rows, :] + x2_ref[rows, :]
                    parts = [_rms(o[:, k * GLA_DV:(k + 1) * GLA_DV]) * g_ref[...] for k in range(GLA_HEADS)]
                    h = jnp.concatenate(parts, axis=-1) * _silu(gin_ref[rows, :])
                h_ref[rows, :] = h.astype(BF16)

            _for_row_chunks(x_ref.shape[0], rows_fn)

    def finish(lhs_ref):
        y = _dot(lhs_ref[...], w_ref[...])
        if epi == "resid":
            y = res_ref[...] + gt_ref[...] * y
        if n_outs == 1:
            o_refs[0][...] = y.astype(o_refs[0].dtype)
        else:
            for k, o_ref in enumerate(o_refs):
                @pl.when(pl.program_id(1) // tiles_per_out == k)
                def _(o_ref=o_ref):
                    o_ref[...] = y.astype(o_ref.dtype)

    if pro != "cast":
        finish(h_ref)
    elif split is None:
        finish(x_ref)
    else:
        pl.when(pl.program_id(0) < split)(lambda: finish(x_ref))
        pl.when(pl.program_id(0) >= split)(lambda: finish(x2_ref))


def _mm(x, w, *, rows, row_off=0, xcol=0, n_outs=1, pro, epi="plain", out_dtype=F32, tm=1024, tn=1024,
        g=None, mod=None, layer=None, jmod=None, x2=None, gin=None, gin_col=0, res=None):
    kdim, n = w.shape
    tn = min(tn, n // n_outs)
    tiles_per_out = n // n_outs // tn
    assert rows % tm == 0 and n % (n_outs * tn) == 0 and (n_outs == 1 or epi == "plain")
    split = None
    if pro == "cast" and x2 is not None:
        split = x.shape[0] // tm
        xspec = pl.BlockSpec((tm, kdim), lambda i, j: (jnp.minimum(i, split - 1), 0))
        args, specs = [x, x2], [xspec, pl.BlockSpec((tm, kdim), lambda i, j: (jnp.maximum(i - split, 0), 0))]
    else:
        xspec = pl.BlockSpec((tm, kdim), lambda i, j: (i + row_off, xcol))
        args, specs = [x], [xspec]
    if pro == "gla":
        args += [x2, gin]
        specs += [xspec, pl.BlockSpec((tm, kdim), lambda i, j: (i + row_off, gin_col))]
    if pro in ("ada", "rms", "gla"):
        args.append(g.reshape(1, -1))
        specs.append(pl.BlockSpec((1, g.shape[-1]), lambda i, j: (0, 0)))
    if pro == "ada":
        args += [mod, mod]
        specs += [_mod_spec(layer, 3 * jmod, tm, row_off), _mod_spec(layer, 3 * jmod + 1, tm, row_off)]
    args.append(w)
    specs.append(pl.BlockSpec((kdim, tn), lambda i, j: (0, j)))
    if epi == "resid":
        args += [res, mod]
        specs += [
            pl.BlockSpec((tm, tn), lambda i, j: (i + row_off, j)),
            pl.BlockSpec((None, None, None, 1, tn),
                         lambda i, j: (layer, _group_of_row((i + row_off) * tm), 3 * jmod + 2, 0, j)),
        ]
    scratch = []
    if pro != "cast":
        scratch.append(pltpu.VMEM((tm, kdim), BF16))
    if pro == "ada":
        scratch.append(pltpu.VMEM((1, kdim), F32))
    out_specs = [pl.BlockSpec((tm, tn), functools.partial(
        lambda i, j, k: (i, jnp.clip(j - k * tiles_per_out, 0, tiles_per_out - 1)), k=k)) for k in range(n_outs)]
    outs = pl.pallas_call(
        functools.partial(_mm_kernel, pro=pro, epi=epi, split=split, n_outs=n_outs, tiles_per_out=tiles_per_out),
        out_shape=[jax.ShapeDtypeStruct((rows, n // n_outs), out_dtype)] * n_outs,
        grid=(rows // tm, n // tn),
        in_specs=specs,
        out_specs=out_specs,
        scratch_shapes=scratch,
        compiler_params=_cparams(("parallel", "arbitrary")),
        name="proj_" + pro + "_" + epi,
    )(*args)
    return outs[0] if n_outs == 1 else outs


def _rmsnorm_kernel(x_ref, g_ref, o_ref):
    o_ref[...] = _rms(x_ref[...]) * g_ref[...]


def _rmsnorm(x, g, *, rows, row_off=0, xcol=0, tm=512):
    width = g.shape[-1]
    return pl.pallas_call(
        _rmsnorm_kernel,
        out_shape=jax.ShapeDtypeStruct((rows, width), F32),
        grid=(rows // tm,),
        in_specs=[pl.BlockSpec((tm, width), lambda i: (i + row_off, xcol)),
                  pl.BlockSpec((1, width), lambda i: (0, 0))],
        out_specs=pl.BlockSpec((tm, width), lambda i: (i, 0)),
        compiler_params=_cparams(("parallel",)),
        name="rmsnorm",
    )(x, g.reshape(1, width))


def _rope(x, cos, sin):
    width = x.shape[-1]
    lane = lax.broadcasted_iota(jnp.int32, x.shape, 1)
    up = pltpu.roll(x, width - 16, 1)
    down = pltpu.roll(x, 16, 1)
    swapped = jnp.where((lane & 31) < 16, up, down)
    return x * cos + swapped * sin


def _rope_keys_kernel(x_ref, cos_ref, sin_ref, o_ref):
    o_ref[...] = _rope(x_ref[...], cos_ref[...], sin_ref[...]).astype(o_ref.dtype)


def _rope_keys(down, rope_tabs, *, tm=1024):
    cos, sin = rope_tabs
    per_seq = DEC_SEQ // tm
    return pl.pallas_call(
        _rope_keys_kernel,
        out_shape=jax.ShapeDtypeStruct((N_LAT, 128), BF16),
        grid=(N_LAT // tm,),
        in_specs=[pl.BlockSpec((tm, 128), lambda i: (N_CTX // tm + i, MLA_Q_LORA // 128)),
                  pl.BlockSpec((tm, 128), lambda i: (i % per_seq, 0)),
                  pl.BlockSpec((tm, 128), lambda i: (i % per_seq, 0))],
        out_specs=pl.BlockSpec((tm, 128), lambda i: (i, 0)),
        compiler_params=_cparams(("parallel",)),
        name="rope_keys",
    )(down, cos, sin)


def _attn_kernel(*refs, nseg, has_r, rope, scale, dh, dv, pps):
    it = iter(refs)
    q_ref = next(it)
    qr_ref = next(it) if has_r else None
    segs = []
    for _ in range(nseg):
        k_ref = next(it)
        kr_ref = next(it) if has_r else None
        v_ref = next(it)
        segs.append((k_ref, kr_ref, v_ref))
    if rope:
        cq_ref, sq_ref = next(it), next(it)
    o_ref = next(it)

    if has_r:
        qr = qr_ref[...].astype(F32)
        if rope:
            qr = _rope(qr, cq_ref[...], sq_ref[...])
        qrs = [qr[:, h * MLA_ROPE:(h + 1) * MLA_ROPE].astype(BF16) for h in range(2 * pps)]
        krs = [kr_ref[:, :MLA_ROPE].astype(BF16) for _, kr_ref, _ in segs]

    for pair in range(pps):
        scores = []
        for h in (2 * pair, 2 * pair + 1):
            q = q_ref[:, h * dh:(h + 1) * dh].astype(BF16)
            ss = []
            for si, (k_ref, _, _) in enumerate(segs):
                s = _dot_nt(q, k_ref[:, h * dh:(h + 1) * dh].astype(BF16))
                if has_r:
                    s = s + _dot_nt(qrs[h], krs[si])
                ss.append(s * (scale * LOG2E))
            scores.append(ss)
        for h, ss in zip((2 * pair, 2 * pair + 1), scores):
            m = ss[0].max(axis=-1, keepdims=True)
            for s in ss[1:]:
                m = jnp.maximum(m, s.max(axis=-1, keepdims=True))
            acc = None
            den = None
            for s, (_, _, v_ref) in zip(ss, segs):
                e = jnp.exp2(s - m)
                d = e.sum(axis=-1, keepdims=True)
                pv = _dot(e.astype(BF16), v_ref[:, h * dv:(h + 1) * dv].astype(BF16))
                acc = pv if acc is None else acc + pv
                den = d if den is None else den + d
            o_ref[:, h * dv:(h + 1) * dv] = (acc / den).astype(o_ref.dtype)


def _attention(q, segs, *, nb, lq, tq, q_row0, q_col, scale, qr=None, rope_tabs=None, pps=1, dh=128, dv=128):
    npairs = 8
    ng = npairs // pps
    nq = lq // tq
    has_r = qr is not None
    q_blk0 = q_row0 // tq
    assert q_col % pps == 0 and (rope_tabs is None or pps == 1)
    args = [q]
    specs = [pl.BlockSpec((tq, 2 * dh * pps), lambda b, p, t: (q_blk0 + b * nq + t, q_col // pps + p))]
    if has_r:
        qr_arr, qr_col = qr
        assert qr_col % pps == 0
        args.append(qr_arr)
        specs.append(pl.BlockSpec((tq, 2 * MLA_ROPE * pps),
                                  lambda b, p, t: (q_blk0 + b * nq + t, qr_col // pps + p)))
    for sg in segs:
        lk = sg["lk"]
        k_arr, k_row0, k_col = sg["k"]
        assert k_col % pps == 0
        args.append(k_arr)
        specs.append(pl.BlockSpec((lk, 2 * dh * pps), functools.partial(
            lambda b, p, t, r0, c0: (r0 + b, c0 + p), r0=k_row0 // lk, c0=k_col // pps)))
        if has_r:
            kr_arr, kr_row0, kr_col, kr_w = sg["kr"]
            args.append(kr_arr)
            specs.append(pl.BlockSpec((lk, kr_w), functools.partial(
                lambda b, p, t, r0, c0: (r0 + b, c0), r0=kr_row0 // lk, c0=kr_col)))
        v_arr, v_row0, v_col = sg["v"]
        assert v_col % pps == 0
        args.append(v_arr)
        specs.append(pl.BlockSpec((lk, 2 * dv * pps), functools.partial(
            lambda b, p, t, r0, c0: (r0 + b, c0 + p), r0=v_row0 // lk, c0=v_col // pps)))
    if rope_tabs is not None:
        cos, sin = rope_tabs
        args += [cos, sin]
        specs += [pl.BlockSpec((tq, 128), lambda b, p, t: (t, 0)),
                  pl.BlockSpec((tq, 128), lambda b, p, t: (t, 0))]
    return pl.pallas_call(
        functools.partial(_attn_kernel, nseg=len(segs), has_r=has_r, rope=rope_tabs is not None,
                          scale=scale, dh=dh, dv=dv, pps=pps),
        out_shape=jax.ShapeDtypeStruct((nb * lq, npairs * 2 * dv), BF16),
        grid=(nb, ng, nq),
        in_specs=specs,
        out_specs=pl.BlockSpec((tq, 2 * dv * pps), lambda b, p, t: (b * nq + t, p)),
        compiler_params=_cparams(("parallel", "parallel", "arbitrary")),
        name="attention",
    )(*args)


def _na_kernel(q_ref, k_ref, v_ref, kc_ref, vc_ref, t2_ref, o_ref, bias_ref, cap_ref):
    r0 = pl.program_id(1) * NA_QR
    ks = jnp.clip(r0 - NA_WIN_ROWS // 2, 0, LAT_ROWS - NA_KR)
    nq, nk = NA_QR * GRID_W, NA_KR * GRID_W
    scale = NA_DH ** -0.5 * LOG2E

    @pl.when(pl.program_id(2) == 0)
    def _():
        row = lax.broadcasted_iota(jnp.int32, (nq, nk), 0)
        lane = lax.broadcasted_iota(jnp.int32, (nq, nk), 1)
        qc = row & (GRID_W - 1)
        kc = lane & (GRID_W - 1)
        rs = jnp.clip(r0 + (row >> LOG2_GRID_W) - NA_WIN_ROWS // 2, 0, LAT_ROWS - NA_WIN_ROWS)
        kr = ks + (lane >> LOG2_GRID_W)
        cs = jnp.clip(qc - NA_WIN_COLS // 2, 0, GRID_W - NA_WIN_COLS)
        ok = (kr >= rs) & (kr < rs + NA_WIN_ROWS) & (kc >= cs) & (kc < cs + NA_WIN_COLS)
        cap_ref[...] = jnp.where(ok, jnp.inf, NEG_INF)
        for hh in range(2):
            for i in range(NA_QR):
                for jp in range(NA_KR // 2):
                    e = jnp.clip(ks - r0 + 2 * jp - i + NA_WIN_ROWS, 0, 2 * NA_WIN_ROWS - 1)
                    bias_ref[hh, i * GRID_W:(i + 1) * GRID_W, jp * 128:(jp + 1) * 128] = t2_ref[hh, e] * LOG2E

    start = pl.multiple_of(ks * GRID_W, GRID_W)
    kw = k_ref[pl.ds(start, nk), :]
    vw = v_ref[pl.ds(start, nk), :]
    cap = cap_ref[...]
    for hh in range(2):
        sl = slice(hh * NA_DH, (hh + 1) * NA_DH)
        q = q_ref[:, sl]
        s_loc = jnp.minimum(_dot_nt(q, kw[:, sl]) * scale + bias_ref[hh], cap)
        s_ctx = _dot_nt(q, kc_ref[:, sl]) * scale
        m = jnp.maximum(s_loc.max(axis=-1, keepdims=True), s_ctx.max(axis=-1, keepdims=True))
        e_loc = jnp.exp2(s_loc - m)
        e_ctx = jnp.exp2(s_ctx - m)
        den = e_loc.sum(axis=-1, keepdims=True) + e_ctx.sum(axis=-1, keepdims=True)
        acc = _dot(e_loc.astype(BF16), vw[:, sl]) + _dot(e_ctx.astype(BF16), vc_ref[:, sl])
        o_ref[:, sl] = (acc / den).astype(o_ref.dtype)


def _na_bias_table(rpb):
    col = jnp.arange(GRID_W)
    col_off = jnp.clip(col[None, :] - col[:, None] + NA_WIN_COLS - 1, 0, 2 * NA_WIN_COLS - 2)
    e = jnp.arange(2 * NA_WIN_ROWS)
    dr = jnp.clip(jnp.stack([e - 1, e], axis=1), 0, 2 * NA_WIN_ROWS - 2)
    tb = rpb[:, dr][:, :, :, col_off]
    return jnp.transpose(tb, (0, 1, 3, 2, 4)).reshape(NA_HEADS, 2 * NA_WIN_ROWS, GRID_W, 2 * GRID_W)


def _na_latent(qkv_l, kc, vc, t2):
    npairs = NA_HEADS // 2
    w2 = 2 * NA_DH
    nrb = LAT_ROWS // NA_QR
    nq = NA_QR * GRID_W
    return pl.pallas_call(
        _na_kernel,
        out_shape=jax.ShapeDtypeStruct((N_LAT, D), BF16),
        grid=(npairs, nrb, DEC_BATCH),
        in_specs=[
            pl.BlockSpec((nq, w2), lambda p, r, b: (b * nrb + r, p)),
            pl.BlockSpec((DEC_SEQ, w2), lambda p, r, b: (b, npairs + p)),
            pl.BlockSpec((DEC_SEQ, w2), lambda p, r, b: (b, 2 * npairs + p)),
            pl.BlockSpec((PAST, w2), lambda p, r, b: (b, p)),
            pl.BlockSpec((PAST, w2), lambda p, r, b: (b, p)),
            pl.BlockSpec((2, 2 * NA_WIN_ROWS, GRID_W, 2 * GRID_W), lambda p, r, b: (p, 0, 0, 0)),
        ],
        out_specs=pl.BlockSpec((nq, w2), lambda p, r, b: (b * nrb + r, p)),
        scratch_shapes=[pltpu.VMEM((2, nq, NA_KR * GRID_W), F32), pltpu.VMEM((nq, NA_KR * GRID_W), F32)],
        compiler_params=_cparams(("parallel", "parallel", "arbitrary")),
        name="na_latent",
    )(qkv_l, qkv_l, qkv_l, kc, vc, t2)


def _gla_pos(t):
    is_ctx = t < GLA_CTX_STEPS
    u = jnp.maximum(t - GLA_CTX_STEPS, 0)
    seq = jnp.where(is_ctx, t // GLA_CTX_CHUNKS, u // GLA_LAT_CHUNKS)
    n = jnp.where(is_ctx, t % GLA_CTX_CHUNKS, u % GLA_LAT_CHUNKS)
    return is_ctx, seq, n


def _gla_bwd_chunk(t):
    is_ctx, _, n = _gla_pos(t)
    return t + jnp.where(is_ctx, GLA_CTX_CHUNKS, GLA_LAT_CHUNKS) - 1 - 2 * n


def _gla_kernel(qf_ref, kf_ref, vf_ref, gdf_ref, qb_ref, kb_ref, vb_ref, gdb_ref, wgu_ref, bg_ref, s0_ref,
                of_ref, ob_ref, fin_ref, st_ref):
    is_ctx, _, n = _gla_pos(pl.program_id(0))
    is_lat = jnp.logical_not(is_ctx)

    @pl.when(jnp.logical_and(n == 0, is_ctx))
    def _():
        st_ref[...] = jnp.zeros_like(st_ref)

    @pl.when(jnp.logical_and(n == 0, is_lat))
    def _():
        st_ref[...] = s0_ref[...]

    ri = lax.broadcasted_iota(jnp.int32, (GLA_CHUNK, GLA_CHUNK), 0)
    ci = lax.broadcasted_iota(jnp.int32, (GLA_CHUNK, GLA_CHUNK), 1)
    streams = ((qf_ref, kf_ref, vf_ref, gdf_ref, of_ref), (qb_ref, kb_ref, vb_ref, gdb_ref, ob_ref))
    for d, (q_ref, k_ref, v_ref, gd_ref, o_ref) in enumerate(streams):
        keep = (ci <= ri) if d == 0 else (ci >= ri)
        tri = jnp.where(keep, 1.0, 0.0).astype(BF16)
        pre = _dot(gd_ref[...].astype(BF16), wgu_ref[d]) + bg_ref[d]
        la = (jnp.minimum(pre, 0.0) - jnp.log1p(jnp.exp(-jnp.abs(pre)))) / GLA_TAU
        hi = la.astype(BF16)
        r1 = la - hi.astype(F32)
        mid = r1.astype(BF16)
        lo = (r1 - mid.astype(F32)).astype(BF16)
        cum = _dot(tri, hi) + _dot(tri, mid) + _dot(tri, lo)
        tot = jnp.sum(la, axis=0, keepdims=True)
        k = k_ref[...]
        q_dec = (q_ref[...] * (GLA_DK ** -0.5) * jnp.exp(cum)).astype(BF16)
        k_inv = (k * jnp.exp(-cum)).astype(BF16)
        k_end = (k * jnp.exp(tot - cum)).astype(BF16)
        dec = jnp.exp(tot)
        v = v_ref[...].astype(BF16)
        for h in range(GLA_HEADS):
            ks = slice(h * GLA_DK, (h + 1) * GLA_DK)
            vs = slice(h * GLA_DV, (h + 1) * GLA_DV)
            att = jnp.where(keep, _dot_nt(q_dec[:, ks], k_inv[:, ks]), 0.0)
            st = st_ref[d, h]
            o_ref[:, vs] = _dot(att.astype(BF16), v[:, vs]) + _dot_nt(q_dec[:, ks], st.astype(BF16))
            st_ref[d, h] = st * dec[:, ks] + _dot_tn(v[:, vs], k_end[:, ks])

    @pl.when(jnp.logical_and(is_ctx, n == GLA_CTX_CHUNKS - 1))
    def _():
        fin_ref[...] = st_ref[...]


def _gla_scan(proj, wgu, bg, s0t):
    hk = GLA_HEADS * GLA_DK
    hv = GLA_HEADS * GLA_DV
    gd_col = GLA_QKVG_N // 128

    def fwd(c):
        return lambda t: (t, c)

    def bwd(c):
        return lambda t: (_gla_bwd_chunk(t), c)

    in_specs = []
    for ix in (fwd, bwd):
        in_specs += [pl.BlockSpec((GLA_CHUNK, hk), ix(0)), pl.BlockSpec((GLA_CHUNK, hk), ix(1)),
                     pl.BlockSpec((GLA_CHUNK, hv), ix(1)), pl.BlockSpec((GLA_CHUNK, 128), ix(gd_col))]
    st_block = (None, 2, GLA_HEADS, GLA_DV, GLA_DK)
    in_specs += [
        pl.BlockSpec((2, 128, hk), lambda t: (0, 0, 0)),
        pl.BlockSpec((2, 1, hk), lambda t: (0, 0, 0)),
        pl.BlockSpec(st_block, lambda t: (jnp.where(_gla_pos(t)[0], 0, _gla_pos(t)[1]), 0, 0, 0, 0)),
    ]
    o_shape = jax.ShapeDtypeStruct((N_TOK, hv), F32)
    return pl.pallas_call(
        _gla_kernel,
        out_shape=(o_shape, o_shape, jax.ShapeDtypeStruct((BATCH, 2, GLA_HEADS, GLA_DV, GLA_DK), F32)),
        grid=(GLA_STEPS,),
        in_specs=in_specs,
        out_specs=(
            pl.BlockSpec((GLA_CHUNK, hv), lambda t: (t, 0)),
            pl.BlockSpec((GLA_CHUNK, hv), lambda t: (_gla_bwd_chunk(t), 0)),
            pl.BlockSpec(st_block, lambda t: (jnp.where(_gla_pos(t)[0], _gla_pos(t)[1], BATCH - 1), 0, 0, 0, 0)),
        ),
        scratch_shapes=[pltpu.VMEM((2, GLA_HEADS, GLA_DV, GLA_DK), F32)],
        compiler_params=_cparams(("arbitrary",)),
        name="gla_scan",
    )(*([proj] * 8), wgu, bg, s0t)


def _rope_tables():
    t = jnp.arange(DEC_SEQ)
    d = MLA_ROPE // 2
    inv = ROPE_THETA ** (-jnp.arange(0, d, 2, dtype=F32) / d)
    ang_r = (t // GRID_W).astype(F32)[:, None] * inv[None]
    ang_c = (t % GRID_W).astype(F32)[:, None] * inv[None]
    cos = jnp.concatenate([jnp.cos(ang_r)] * 2 + [jnp.cos(ang_c)] * 2, axis=-1)
    sin = jnp.concatenate([-jnp.sin(ang_r), jnp.sin(ang_r), -jnp.sin(ang_c), jnp.sin(ang_c)], axis=-1)
    return jnp.concatenate([cos, cos], axis=-1), jnp.concatenate([sin, sin], axis=-1)


def _mixer_na(x, mod, layer, g, w_qkv, w_o, rpb, cache_k, cache_v):
    w_qkv = w_qkv.astype(BF16)
    ada = dict(pro="ada", g=g, mod=mod, layer=layer, jmod=1)
    q_c, k_c, v_c = _mm(x, w_qkv, rows=N_CTX, n_outs=3, out_dtype=F32, tm=512, **ada)
    qkv_l = _mm(x, w_qkv, rows=N_LAT, row_off=N_CTX // 1024, out_dtype=BF16, tn=1536, **ada)
    o_c = _attention(q_c, [dict(k=(k_c, 0, 0), v=(v_c, 0, 0), lk=SEQ)],
                     nb=BATCH, lq=SEQ, tq=SEQ, q_row0=0, q_col=0, scale=NA_DH ** -0.5, pps=CTX_PPS)
    kc = cache_k.reshape(DEC_BATCH * PAST, D).astype(BF16)
    vc = cache_v.reshape(DEC_BATCH * PAST, D).astype(BF16)
    o_l = _na_latent(qkv_l, kc, vc, _na_bias_table(rpb))
    x = _mm(o_c, w_o.astype(BF16), rows=N_TOK, pro="cast", x2=o_l, epi="resid", res=x, mod=mod, layer=layer,
            jmod=1, tm=512, tn=2048)
    return x, k_c.reshape(BATCH, SEQ, NA_HEADS, NA_DH), v_c.reshape(BATCH, SEQ, NA_HEADS, NA_DH)


def _mixer_mla(x, mod, layer, g, w_down, g_q, w_uq, g_kv, w_ukv, w_o, cache_ckv, cache_kr):
    hq = MLA_NOPE + MLA_ROPE
    wd = jnp.concatenate([
        w_down[:, :MLA_Q_LORA],
        w_down[:, MLA_Q_LORA + MLA_KV_LORA:],
        jnp.zeros((D, 256 - MLA_ROPE), F32),
        w_down[:, MLA_Q_LORA:MLA_Q_LORA + MLA_KV_LORA]], axis=1).astype(BF16)
    wq = w_uq.reshape(MLA_Q_LORA, MLA_HEADS, hq)
    wq = jnp.concatenate([wq[:, :, :MLA_NOPE].reshape(MLA_Q_LORA, -1),
                          wq[:, :, MLA_NOPE:].reshape(MLA_Q_LORA, -1)], axis=1).astype(BF16)
    wkv = w_ukv.reshape(MLA_KV_LORA, MLA_HEADS, MLA_NOPE + MLA_V)
    wkv = jnp.concatenate([wkv[:, :, :MLA_NOPE].reshape(MLA_KV_LORA, -1),
                           wkv[:, :, MLA_NOPE:].reshape(MLA_KV_LORA, -1)], axis=1).astype(BF16)

    down = _mm(x, wd, rows=N_TOK, pro="ada", g=g, mod=mod, layer=layer, jmod=1, out_dtype=F32, tn=MLA_DOWN_N)
    q = _mm(down, wq, rows=N_TOK, pro="rms", g=g_q, out_dtype=F32, tn=1536)
    kv = _mm(down, wkv, rows=N_TOK, xcol=2, pro="rms", g=g_kv, out_dtype=BF16, tn=2048)
    kv_cache = _mm(cache_ckv.reshape(DEC_BATCH * PAST, MLA_KV_LORA).astype(BF16), wkv,
                   rows=DEC_BATCH * PAST, pro="cast", out_dtype=BF16, tn=2048)
    ckv_c = _rmsnorm(down, g_kv, rows=N_CTX, xcol=2)
    kr_c = down[:N_CTX, MLA_Q_LORA:MLA_Q_LORA + MLA_ROPE]

    npairs = MLA_HEADS // 2
    kr_col = MLA_Q_LORA // 128
    o_c = _attention(q, [dict(k=(kv, 0, 0), v=(kv, 0, npairs), kr=(down, 0, kr_col, 128), lk=SEQ)],
                     nb=BATCH, lq=SEQ, tq=SEQ, q_row0=0, q_col=0, qr=(q, 2 * npairs), scale=MLA_SCALE, pps=CTX_PPS)
    tabs = _rope_tables()
    kr_l = _rope_keys(down, tabs)
    o_l = _attention(
        q,
        [dict(k=(kv, N_CTX, 0), v=(kv, N_CTX, npairs), kr=(kr_l, 0, 0, 128), lk=DEC_SEQ),
         dict(k=(kv_cache, 0, 0), v=(kv_cache, 0, npairs),
              kr=(cache_kr.reshape(DEC_BATCH * PAST, MLA_ROPE), 0, 0, MLA_ROPE), lk=PAST)],
        nb=DEC_BATCH, lq=DEC_SEQ, tq=512, q_row0=N_CTX, q_col=0, qr=(q, 2 * npairs), scale=MLA_SCALE,
        rope_tabs=tabs)
    x = _mm(o_c, w_o.astype(BF16), rows=N_TOK, pro="cast", x2=o_l, epi="resid", res=x, mod=mod, layer=layer,
            jmod=1, tm=512, tn=2048)
    return x, ckv_c.reshape(BATCH, SEQ, MLA_KV_LORA), kr_c.reshape(BATCH, SEQ, MLA_ROPE)


def _mixer_gla(x, mod, layer, g, w_in, w_gd, w_gu, b_g, g_norm, w_o, state):
    pad = jnp.zeros((D, GLA_PROJ_N - GLA_QKVG_N - 2 * GLA_RANK), F32)
    w_cat = jnp.concatenate([w_in, w_gd[0], w_gd[1], pad], axis=1).astype(BF16)
    proj = _mm(x, w_cat, rows=N_TOK, pro="ada", g=g, mod=mod, layer=layer, jmod=1, out_dtype=F32, tn=1280)
    wgu = jnp.zeros((2, 128, GLA_HEADS * GLA_DK), F32)
    wgu = wgu.at[0, :GLA_RANK].set(w_gu[0]).at[1, GLA_RANK:2 * GLA_RANK].set(w_gu[1]).astype(BF16)
    bg = b_g.reshape(2, 1, GLA_HEADS * GLA_DK)
    o_f, o_b, st_c = _gla_scan(proj, wgu, bg, jnp.swapaxes(state, -1, -2))
    x = _mm(o_f, w_o.astype(BF16), rows=N_TOK, pro="gla", x2=o_b, gin=proj, gin_col=2, g=g_norm,
            epi="resid", res=x, mod=mod, layer=layer, jmod=1, tm=256, tn=2048)
    return x, jnp.swapaxes(st_c, -1, -2)


def kernel(x_prompt, x_sample, cache_na_k, cache_na_v, cache_mla_ckv, cache_mla_krope, state_gla, c, c_ctx, norm_g, w_ada, b_ada, w_ffn_in, w_ffn_out, w_na_qkv, w_na_o, na_rpb, w_mla_down, g_mla_q, w_mla_uq, g_mla_kv, w_mla_ukv, w_mla_o, w_gla_in, w_gla_gate_down, w_gla_gate_up, b_gla_gate, g_gla_norm, w_gla_o, final_norm_g):
    x = jnp.concatenate([x_prompt.reshape(N_CTX, D), x_sample.reshape(N_LAT, D)], axis=0)
    cond = jnp.concatenate([c_ctx[None], c, jnp.zeros((N_GROUPS - 1 - DEC_BATCH, D), F32)], axis=0)
    mod = _ada_mod(cond, w_ada, b_ada)
    w_ffn = (w_ffn_in[0, 0].astype(BF16), w_ffn_out[0, 0].astype(BF16))

    def ffn(x, w_ffn, layer, half):
        last = layer == DEPTH - 1 and half == 1
        nxt = None if last else (w_ffn_in, w_ffn_out, layer + half, 1 - half)
        out = _ffn(x, mod, layer, half, w_ffn[0], w_ffn[1], norm_g[layer, 2 * half], nxt)
        return (out, None) if last else out

    na_k, na_v, ckv, krope, gla_st = [], [], [], [], []
    for i in range(DEPTH):
        kind, slot = i % 3, i // 3
        x, w_ffn = ffn(x, w_ffn, i, 0)
        if kind == 0:
            x, k_c, v_c = _mixer_na(x, mod, i, norm_g[i, 1], w_na_qkv[slot], w_na_o[slot], na_rpb[slot],
                                    cache_na_k[:, slot], cache_na_v[:, slot])
            na_k.append(k_c)
            na_v.append(v_c)
        elif kind == 1:
            x, ckv_c, kr_c = _mixer_mla(x, mod, i, norm_g[i, 1], w_mla_down[slot], g_mla_q[slot], w_mla_uq[slot],
                                        g_mla_kv[slot], w_mla_ukv[slot], w_mla_o[slot],
                                        cache_mla_ckv[:, slot], cache_mla_krope[:, slot])
            ckv.append(ckv_c)
            krope.append(kr_c)
        else:
            x, st = _mixer_gla(x, mod, i, norm_g[i, 1], w_gla_in[slot], w_gla_gate_down[slot],
                               w_gla_gate_up[slot], b_gla_gate[slot], g_gla_norm[slot], w_gla_o[slot],
                               state_gla[:, slot])
            gla_st.append(st)
        x, w_ffn = ffn(x, w_ffn, i, 1)

    y_prompt = _rmsnorm(x, final_norm_g, rows=N_CTX).reshape(BATCH, SEQ, D)
    y_sample = _rmsnorm(x, final_norm_g, rows=N_LAT, row_off=N_CTX // 512).reshape(DEC_BATCH, DEC_SEQ, D)
    return (y_prompt, y_sample, jnp.stack(na_k, axis=1), jnp.stack(na_v, axis=1), jnp.stack(ckv, axis=1),
            jnp.stack(krope, axis=1), jnp.stack(gla_st, axis=1))
```

```python
import functools

import jax
import jax.numpy as jnp
from jax import lax
from jax.experimental import pallas as pl
from jax.experimental.pallas import tpu as pltpu

F32 = jnp.float32
BF16 = jnp.bfloat16

D = 2048
BATCH, SEQ = 16, 256
DEC_BATCH, DEC_SEQ = 4, 2048
PAST = 512
DEPTH = 4
N_MOD = 9
EPS = 1e-6
D_FF = 5632
GRID_W = 64
LOG2_GRID_W = 6
NEG_INF = -1e30
LOG2E = 1.4426950408889634

N_CTX = BATCH * SEQ
N_LAT = DEC_BATCH * DEC_SEQ
N_TOK = N_CTX + N_LAT
N_GROUPS = 8

NA_HEADS, NA_DH = 16, 128
NA_WIN_ROWS, NA_WIN_COLS = 8, 16
LAT_ROWS = DEC_SEQ // GRID_W
NA_QR = 8
NA_KR = NA_QR + NA_WIN_ROWS

MLA_HEADS = 16
MLA_Q_LORA, MLA_KV_LORA = 768, 512
MLA_NOPE, MLA_ROPE, MLA_V = 128, 64, 128
MLA_SCALE = (MLA_NOPE + MLA_ROPE) ** -0.5
ROPE_THETA = 10000.0
MLA_DOWN_N = 1536

GLA_HEADS, GLA_DK, GLA_DV = 4, 256, 512
GLA_RANK = 16
GLA_TAU = 16.0
GLA_CHUNK = 128
GLA_QKVG_N = 2 * GLA_HEADS * GLA_DK + 2 * GLA_HEADS * GLA_DV
GLA_PROJ_N = GLA_QKVG_N + 256
GLA_CTX_CHUNKS = SEQ // GLA_CHUNK
GLA_LAT_CHUNKS = DEC_SEQ // GLA_CHUNK
GLA_CTX_STEPS = BATCH * GLA_CTX_CHUNKS
GLA_STEPS = GLA_CTX_STEPS + DEC_BATCH * GLA_LAT_CHUNKS

VMEM_LIMIT = 56 * 1024 * 1024
ROW_CHUNK = 16
CTX_PPS = 8
FFN_TF = 512
FFN_SUB = 512
FFN_CVT_IN_ROWS = 16
FFN_CVT_OUT_ROWS = 64


def _cparams(sem):
    return pltpu.CompilerParams(dimension_semantics=sem, vmem_limit_bytes=VMEM_LIMIT)


def _group_of_row(r0):
    return jnp.where(r0 < N_CTX, 0, 1 + (r0 - N_CTX) // DEC_SEQ)


def _silu(x):
    return x / (1.0 + jnp.exp(-x))


def _rms(x):
    return x * lax.rsqrt(jnp.mean(x * x, axis=-1, keepdims=True) + EPS)


def _for_row_chunks(n_rows, body):
    def step(c, carry):
        body(pl.ds(pl.multiple_of(c * ROW_CHUNK, ROW_CHUNK), ROW_CHUNK))
        return carry
    lax.fori_loop(0, n_rows // ROW_CHUNK, step, 0, unroll=8)


def _dot(a, b):
    return jnp.dot(a, b, preferred_element_type=F32)


def _dot_nt(a, b):
    return lax.dot_general(a, b, (((1,), (1,)), ((), ())), preferred_element_type=F32)


def _dot_tn(a, b):
    return lax.dot_general(a, b, (((0,), (0,)), ((), ())), preferred_element_type=F32)


def _ada_kernel(c_ref, w_ref, b_ref, o_ref):
    s = _silu(c_ref[...]).astype(BF16)
    o_ref[...] = _dot(s, w_ref[...].astype(BF16)) + b_ref[...]


def _ada_mod(cond, w_ada, b_ada):
    tn = 1024
    n = N_MOD * D
    out = pl.pallas_call(
        _ada_kernel,
        out_shape=jax.ShapeDtypeStruct((DEPTH, N_GROUPS, n), F32),
        grid=(DEPTH, n // tn),
        in_specs=[
            pl.BlockSpec((N_GROUPS, D), lambda l, j: (0, 0)),
            pl.BlockSpec((None, D, tn), lambda l, j: (l, 0, j)),
            pl.BlockSpec((None, 1, tn), lambda l, j: (l, 0, j)),
        ],
        out_specs=pl.BlockSpec((None, N_GROUPS, tn), lambda l, j: (l, 0, j)),
        compiler_params=_cparams(("parallel", "parallel")),
        name="ada_mod",
    )(cond, w_ada, b_ada.reshape(DEPTH, 1, n))
    return out.reshape(DEPTH, N_GROUPS, N_MOD, 1, D)


def _mod_spec(layer, j, tm, row_off):
    return pl.BlockSpec(
        (None, None, None, 1, D),
        lambda i, n: (layer, _group_of_row((i + row_off) * tm), j, 0, 0))


def _ffn_kernel(*refs, convert_next):
    if convert_next:
        (x_ref, g_ref, sh_ref, sc_ref, gt_ref, wg_ref, wu_ref, wo_ref, nin_ref, nout_ref,
         o_ref, cin_ref, cout_ref, h_ref, gm_ref) = refs
    else:
        x_ref, g_ref, sh_ref, sc_ref, gt_ref, wg_ref, wu_ref, wo_ref, o_ref, h_ref, gm_ref = refs
    f = pl.program_id(1)
    tm = x_ref.shape[0]

    @pl.when(f == 0)
    def _():
        gm_ref[...] = g_ref[...] * (1.0 + sc_ref[...])

        def rows_fn(rows):
            h_ref[rows, :] = (_rms(x_ref[rows, :]) * gm_ref[...] + sh_ref[...]).astype(BF16)
            o_ref[rows, :] = jnp.zeros((ROW_CHUNK, D), F32)

        _for_row_chunks(tm, rows_fn)

    for r in range(0, tm, FFN_SUB):
        h = h_ref[r:r + FFN_SUB, :]
        a = _silu(_dot(h, wg_ref[...])) * _dot(h, wu_ref[...])
        o_ref[r:r + FFN_SUB, :] += _dot(a.astype(BF16), wo_ref[...])

    @pl.when(f == pl.num_programs(1) - 1)
    def _():
        o_ref[...] = x_ref[...] + 0.5 * gt_ref[...] * o_ref[...]

    if convert_next:
        cin_ref[...] = nin_ref[...].astype(BF16)
        cout_ref[...] = nout_ref[...].astype(BF16)


def _ffn(x, mod, layer, half, w_in, w_out, g, nxt=None, *, tm=1024):
    tf = FFN_TF
    nf = D_FF // tf
    j = 2 * half
    in_specs = [
        pl.BlockSpec((tm, D), lambda i, f: (i, 0)),
        pl.BlockSpec((1, D), lambda i, f: (0, 0)),
        _mod_spec(layer, 3 * j, tm, 0),
        _mod_spec(layer, 3 * j + 1, tm, 0),
        _mod_spec(layer, 3 * j + 2, tm, 0),
        pl.BlockSpec((D, tf), lambda i, f: (0, f)),
        pl.BlockSpec((D, tf), lambda i, f: (0, nf + f)),
        pl.BlockSpec((tf, D), lambda i, f: (f, 0)),
    ]
    args = [x, g.reshape(1, D), mod, mod, mod, w_in, w_in, w_out]
    out_shape = [jax.ShapeDtypeStruct((N_TOK, D), F32)]
    out_specs = [pl.BlockSpec((tm, D), lambda i, f: (i, 0))]
    if nxt is not None:
        nw_in, nw_out, nl, nh = nxt
        steps = (N_TOK // tm) * nf
        n_in, n_out = D // FFN_CVT_IN_ROWS, D_FF // FFN_CVT_OUT_ROWS
        assert n_in <= steps and n_out <= steps

        def slab(last):
            return lambda i, f: jnp.minimum(i * nf + f, last)

        s_in, s_out = slab(n_in - 1), slab(n_out - 1)
        in_specs += [pl.BlockSpec((None, None, FFN_CVT_IN_ROWS, 2 * D_FF), lambda i, f: (nl, nh, s_in(i, f), 0)),
                     pl.BlockSpec((None, None, FFN_CVT_OUT_ROWS, D), lambda i, f: (nl, nh, s_out(i, f), 0))]
        args += [nw_in, nw_out]
        out_shape += [jax.ShapeDtypeStruct((D, 2 * D_FF), BF16), jax.ShapeDtypeStruct((D_FF, D), BF16)]
        out_specs += [pl.BlockSpec((FFN_CVT_IN_ROWS, 2 * D_FF), lambda i, f: (s_in(i, f), 0)),
                      pl.BlockSpec((FFN_CVT_OUT_ROWS, D), lambda i, f: (s_out(i, f), 0))]
    outs = pl.pallas_call(
        functools.partial(_ffn_kernel, convert_next=nxt is not None),
        out_shape=out_shape,
        grid=(N_TOK // tm, nf),
        in_specs=in_specs,
        out_specs=out_specs,
        scratch_shapes=[pltpu.VMEM((tm, D), BF16), pltpu.VMEM((1, D), F32)],
        compiler_params=_cparams(("arbitrary", "arbitrary")),
        name="ffn_half",
    )(*args)
    return outs[0] if nxt is None else (outs[0], (outs[1], outs[2]))


def _mm_kernel(*refs, pro, epi, split, n_outs, tiles_per_out):
    it = iter(refs)
    x_ref = next(it)
    if pro == "cast" and split is not None:
        x2_ref = next(it)
    if pro == "gla":
        x2_ref, gin_ref = next(it), next(it)
    if pro in ("ada", "rms", "gla"):
        g_ref = next(it)
    if pro == "ada":
        sh_ref, sc_ref = next(it), next(it)
    w_ref = next(it)
    if epi == "resid":
        res_ref, gt_ref = next(it), next(it)
    o_refs = [next(it) for _ in range(n_outs)]
    if pro != "cast":
        h_ref = next(it)
    if pro == "ada":
        gm_ref = next(it)

    if pro != "cast":
        @pl.when(pl.program_id(1) == 0)
        def _():
            if pro == "ada":
                gm_ref[...] = g_ref[...] * (1.0 + sc_ref[...])

            def rows_fn(rows):
                if pro == "ada":
                    h = _rms(x_ref[rows, :]) * gm_ref[...] + sh_ref[...]
                elif pro == "rms":
                    h = _rms(x_ref[rows, :]) * g_ref[...]
                else:
                    o = x_ref[rows, :] + x2_ref[rows, :]
                    parts = [_rms(o[:, k * GLA_DV:(k + 1) * GLA_DV]) * g_ref[...] for k in range(GLA_HEADS)]
                    h = jnp.concatenate(parts, axis=-1) * _silu(gin_ref[rows, :])
                h_ref[rows, :] = h.astype(BF16)

            _for_row_chunks(x_ref.shape[0], rows_fn)

    def finish(lhs_ref):
        y = _dot(lhs_ref[...], w_ref[...])
        if epi == "resid":
            y = res_ref[...] + gt_ref[...] * y
        if n_outs == 1:
            o_refs[0][...] = y.astype(o_refs[0].dtype)
        else:
            for k, o_ref in enumerate(o_refs):
                @pl.when(pl.program_id(1) // tiles_per_out == k)
                def _(o_ref=o_ref):
                    o_ref[...] = y.astype(o_ref.dtype)

    if pro != "cast":
        finish(h_ref)
    elif split is None:
        finish(x_ref)
    else:
        pl.when(pl.program_id(0) < split)(lambda: finish(x_ref))
        pl.when(pl.program_id(0) >= split)(lambda: finish(x2_ref))


def _mm(x, w, *, rows, row_off=0, xcol=0, n_outs=1, pro, epi="plain", out_dtype=F32, tm=1024, tn=1024,
        g=None, mod=None, layer=None, jmod=None, x2=None, gin=None, gin_col=0, res=None):
    kdim, n = w.shape
    tn = min(tn, n // n_outs)
    tiles_per_out = n // n_outs // tn
    assert rows % tm == 0 and n % (n_outs * tn) == 0 and (n_outs == 1 or epi == "plain")
    split = None
    if pro == "cast" and x2 is not None:
        split = x.shape[0] // tm
        xspec = pl.BlockSpec((tm, kdim), lambda i, j: (jnp.minimum(i, split - 1), 0))
        args, specs = [x, x2], [xspec, pl.BlockSpec((tm, kdim), lambda i, j: (jnp.maximum(i - split, 0), 0))]
    else:
        xspec = pl.BlockSpec((tm, kdim), lambda i, j: (i + row_off, xcol))
        args, specs = [x], [xspec]
    if pro == "gla":
        args += [x2, gin]
        specs += [xspec, pl.BlockSpec((tm, kdim), lambda i, j: (i + row_off, gin_col))]
    if pro in ("ada", "rms", "gla"):
        args.append(g.reshape(1, -1))
        specs.append(pl.BlockSpec((1, g.shape[-1]), lambda i, j: (0, 0)))
    if pro == "ada":
        args += [mod, mod]
        specs += [_mod_spec(layer, 3 * jmod, tm, row_off), _mod_spec(layer, 3 * jmod + 1, tm, row_off)]
    args.append(w)
    specs.append(pl.BlockSpec((kdim, tn), lambda i, j: (0, j)))
    if epi == "resid":
        args += [res, mod]
        specs += [
            pl.BlockSpec((tm, tn), lambda i, j: (i + row_off, j)),
            pl.BlockSpec((None, None, None, 1, tn),
                         lambda i, j: (layer, _group_of_row((i + row_off) * tm), 3 * jmod + 2, 0, j)),
        ]
    scratch = []
    if pro != "cast":
        scratch.append(pltpu.VMEM((tm, kdim), BF16))
    if pro == "ada":
        scratch.append(pltpu.VMEM((1, kdim), F32))
    out_specs = [pl.BlockSpec((tm, tn), functools.partial(
        lambda i, j, k: (i, jnp.clip(j - k * tiles_per_out, 0, tiles_per_out - 1)), k=k)) for k in range(n_outs)]
    outs = pl.pallas_call(
        functools.partial(_mm_kernel, pro=pro, epi=epi, split=split, n_outs=n_outs, tiles_per_out=tiles_per_out),
        out_shape=[jax.ShapeDtypeStruct((rows, n // n_outs), out_dtype)] * n_outs,
        grid=(rows // tm, n // tn),
        in_specs=specs,
        out_specs=out_specs,
        scratch_shapes=scratch,
        compiler_params=_cparams(("parallel", "arbitrary")),
        name="proj_" + pro + "_" + epi,
    )(*args)
    return outs[0] if n_outs == 1 else outs


def _rmsnorm_kernel(x_ref, g_ref, o_ref):
    o_ref[...] = _rms(x_ref[...]) * g_ref[...]


def _rmsnorm(x, g, *, rows, row_off=0, xcol=0, tm=512):
    width = g.shape[-1]
    return pl.pallas_call(
        _rmsnorm_kernel,
        out_shape=jax.ShapeDtypeStruct((rows, width), F32),
        grid=(rows // tm,),
        in_specs=[pl.BlockSpec((tm, width), lambda i: (i + row_off, xcol)),
                  pl.BlockSpec((1, width), lambda i: (0, 0))],
        out_specs=pl.BlockSpec((tm, width), lambda i: (i, 0)),
        compiler_params=_cparams(("parallel",)),
        name="rmsnorm",
    )(x, g.reshape(1, width))


def _rope(x, cos, sin):
    width = x.shape[-1]
    lane = lax.broadcasted_iota(jnp.int32, x.shape, 1)
    up = pltpu.roll(x, width - 16, 1)
    down = pltpu.roll(x, 16, 1)
    swapped = jnp.where((lane & 31) < 16, up, down)
    return x * cos + swapped * sin


def _rope_keys_kernel(x_ref, cos_ref, sin_ref, o_ref):
    o_ref[...] = _rope(x_ref[...], cos_ref[...], sin_ref[...]).astype(o_ref.dtype)


def _rope_keys(down, rope_tabs, *, tm=1024):
    cos, sin = rope_tabs
    per_seq = DEC_SEQ // tm
    return pl.pallas_call(
        _rope_keys_kernel,
        out_shape=jax.ShapeDtypeStruct((N_LAT, 128), BF16),
        grid=(N_LAT // tm,),
        in_specs=[pl.BlockSpec((tm, 128), lambda i: (N_CTX // tm + i, MLA_Q_LORA // 128)),
                  pl.BlockSpec((tm, 128), lambda i: (i % per_seq, 0)),
                  pl.BlockSpec((tm, 128), lambda i: (i % per_seq, 0))],
        out_specs=pl.BlockSpec((tm, 128), lambda i: (i, 0)),
        compiler_params=_cparams(("parallel",)),
        name="rope_keys",
    )(down, cos, sin)


def _attn_kernel(*refs, nseg, has_r, rope, scale, dh, dv, pps):
    it = iter(refs)
    q_ref = next(it)
    qr_ref = next(it) if has_r else None
    segs = []
    for _ in range(nseg):
        k_ref = next(it)
        kr_ref = next(it) if has_r else None
        v_ref = next(it)
        segs.append((k_ref, kr_ref, v_ref))
    if rope:
        cq_ref, sq_ref = next(it), next(it)
    o_ref = next(it)

    if has_r:
        qr = qr_ref[...].astype(F32)
        if rope:
            qr = _rope(qr, cq_ref[...], sq_ref[...])
        qrs = [qr[:, h * MLA_ROPE:(h + 1) * MLA_ROPE].astype(BF16) for h in range(2 * pps)]
        krs = [kr_ref[:, :MLA_ROPE].astype(BF16) for _, kr_ref, _ in segs]

    for pair in range(pps):
        scores = []
        for h in (2 * pair, 2 * pair + 1):
            q = q_ref[:, h * dh:(h + 1) * dh].astype(BF16)
            ss = []
            for si, (k_ref, _, _) in enumerate(segs):
                s = _dot_nt(q, k_ref[:, h * dh:(h + 1) * dh].astype(BF16))
                if has_r:
                    s = s + _dot_nt(qrs[h], krs[si])
                ss.append(s * (scale * LOG2E))
            scores.append(ss)
        for h, ss in zip((2 * pair, 2 * pair + 1), scores):
            m = ss[0].max(axis=-1, keepdims=True)
            for s in ss[1:]:
                m = jnp.maximum(m, s.max(axis=-1, keepdims=True))
            acc = None
            den = None
            for s, (_, _, v_ref) in zip(ss, segs):
                e = jnp.exp2(s - m)
                d = e.sum(axis=-1, keepdims=True)
                pv = _dot(e.astype(BF16), v_ref[:, h * dv:(h + 1) * dv].astype(BF16))
                acc = pv if acc is None else acc + pv
                den = d if den is None else den + d
            o_ref[:, h * dv:(h + 1) * dv] = (acc / den).astype(o_ref.dtype)


def _attention(q, segs, *, nb, lq, tq, q_row0, q_col, scale, qr=None, rope_tabs=None, pps=1, dh=128, dv=128):
    npairs = 8
    ng = npairs // pps
    nq = lq // tq
    has_r = qr is not None
    q_blk0 = q_row0 // tq
    assert q_col % pps == 0 and (rope_tabs is None or pps == 1)
    args = [q]
    specs = [pl.BlockSpec((tq, 2 * dh * pps), lambda b, p, t: (q_blk0 + b * nq + t, q_col // pps + p))]
    if has_r:
        qr_arr, qr_col = qr
        assert qr_col % pps == 0
        args.append(qr_arr)
        specs.append(pl.BlockSpec((tq, 2 * MLA_ROPE * pps),
                                  lambda b, p, t: (q_blk0 + b * nq + t, qr_col // pps + p)))
    for sg in segs:
        lk = sg["lk"]
        k_arr, k_row0, k_col = sg["k"]
        assert k_col % pps == 0
        args.append(k_arr)
        specs.append(pl.BlockSpec((lk, 2 * dh * pps), functools.partial(
            lambda b, p, t, r0, c0: (r0 + b, c0 + p), r0=k_row0 // lk, c0=k_col // pps)))
        if has_r:
            kr_arr, kr_row0, kr_col, kr_w = sg["kr"]
            args.append(kr_arr)
            specs.append(pl.BlockSpec((lk, kr_w), functools.partial(
                lambda b, p, t, r0, c0: (r0 + b, c0), r0=kr_row0 // lk, c0=kr_col)))
        v_arr, v_row0, v_col = sg["v"]
        assert v_col % pps == 0
        args.append(v_arr)
        specs.append(pl.BlockSpec((lk, 2 * dv * pps), functools.partial(
            lambda b, p, t, r0, c0: (r0 + b, c0 + p), r0=v_row0 // lk, c0=v_col // pps)))
    if rope_tabs is not None:
        cos, sin = rope_tabs
        args += [cos, sin]
        specs += [pl.BlockSpec((tq, 128), lambda b, p, t: (t, 0)),
                  pl.BlockSpec((tq, 128), lambda b, p, t: (t, 0))]
    return pl.pallas_call(
        functools.partial(_attn_kernel, nseg=len(segs), has_r=has_r, rope=rope_tabs is not None,
                          scale=scale, dh=dh, dv=dv, pps=pps),
        out_shape=jax.ShapeDtypeStruct((nb * lq, npairs * 2 * dv), BF16),
        grid=(nb, ng, nq),
        in_specs=specs,
        out_specs=pl.BlockSpec((tq, 2 * dv * pps), lambda b, p, t: (b * nq + t, p)),
        compiler_params=_cparams(("parallel", "parallel", "arbitrary")),
        name="attention",
    )(*args)


def _na_kernel(q_ref, k_ref, v_ref, kc_ref, vc_ref, t2_ref, o_ref, bias_ref, cap_ref):
    r0 = pl.program_id(1) * NA_QR
    ks = jnp.clip(r0 - NA_WIN_ROWS // 2, 0, LAT_ROWS - NA_KR)
    nq, nk = NA_QR * GRID_W, NA_KR * GRID_W
    scale = NA_DH ** -0.5 * LOG2E

    @pl.when(pl.program_id(2) == 0)
    def _():
        row = lax.broadcasted_iota(jnp.int32, (nq, nk), 0)
        lane = lax.broadcasted_iota(jnp.int32, (nq, nk), 1)
        qc = row & (GRID_W - 1)
        kc = lane & (GRID_W - 1)
        rs = jnp.clip(r0 + (row >> LOG2_GRID_W) - NA_WIN_ROWS // 2, 0, LAT_ROWS - NA_WIN_ROWS)
        kr = ks + (lane >> LOG2_GRID_W)
        cs = jnp.clip(qc - NA_WIN_COLS // 2, 0, GRID_W - NA_WIN_COLS)
        ok = (kr >= rs) & (kr < rs + NA_WIN_ROWS) & (kc >= cs) & (kc < cs + NA_WIN_COLS)
        cap_ref[...] = jnp.where(ok, jnp.inf, NEG_INF)
        for hh in range(2):
            for i in range(NA_QR):
                for jp in range(NA_KR // 2):
                    e = jnp.clip(ks - r0 + 2 * jp - i + NA_WIN_ROWS, 0, 2 * NA_WIN_ROWS - 1)
                    bias_ref[hh, i * GRID_W:(i + 1) * GRID_W, jp * 128:(jp + 1) * 128] = t2_ref[hh, e] * LOG2E

    start = pl.multiple_of(ks * GRID_W, GRID_W)
    kw = k_ref[pl.ds(start, nk), :]
    vw = v_ref[pl.ds(start, nk), :]
    cap = cap_ref[...]
    scores = []
    for hh in range(2):
        sl = slice(hh * NA_DH, (hh + 1) * NA_DH)
        q = q_ref[:, sl]
        scores.append((jnp.minimum(_dot_nt(q, kw[:, sl]) * scale + bias_ref[hh], cap),
                       _dot_nt(q, kc_ref[:, sl]) * scale))
    for hh, (s_loc, s_ctx) in enumerate(scores):
        sl = slice(hh * NA_DH, (hh + 1) * NA_DH)
        m = jnp.maximum(s_loc.max(axis=-1, keepdims=True), s_ctx.max(axis=-1, keepdims=True))
        e_loc = jnp.exp2(s_loc - m)
        e_ctx = jnp.exp2(s_ctx - m)
        den = e_loc.sum(axis=-1, keepdims=True) + e_ctx.sum(axis=-1, keepdims=True)
        acc = _dot(e_loc.astype(BF16), vw[:, sl]) + _dot(e_ctx.astype(BF16), vc_ref[:, sl])
        o_ref[:, sl] = (acc / den).astype(o_ref.dtype)


def _na_bias_table(rpb):
    col = jnp.arange(GRID_W)
    col_off = jnp.clip(col[None, :] - col[:, None] + NA_WIN_COLS - 1, 0, 2 * NA_WIN_COLS - 2)
    e = jnp.arange(2 * NA_WIN_ROWS)
    dr = jnp.clip(jnp.stack([e - 1, e], axis=1), 0, 2 * NA_WIN_ROWS - 2)
    tb = rpb[:, dr][:, :, :, col_off]
    return jnp.transpose(tb, (0, 1, 3, 2, 4)).reshape(NA_HEADS, 2 * NA_WIN_ROWS, GRID_W, 2 * GRID_W)


def _na_latent(qkv_l, kc, vc, t2):
    npairs = NA_HEADS // 2
    w2 = 2 * NA_DH
    nrb = LAT_ROWS // NA_QR
    nq = NA_QR * GRID_W
    return pl.pallas_call(
        _na_kernel,
        out_shape=jax.ShapeDtypeStruct((N_LAT, D), BF16),
        grid=(npairs, nrb, DEC_BATCH),
        in_specs=[
            pl.BlockSpec((nq, w2), lambda p, r, b: (b * nrb + r, p)),
            pl.BlockSpec((DEC_SEQ, w2), lambda p, r, b: (b, npairs + p)),
            pl.BlockSpec((DEC_SEQ, w2), lambda p, r, b: (b, 2 * npairs + p)),
            pl.BlockSpec((PAST, w2), lambda p, r, b: (b, p)),
            pl.BlockSpec((PAST, w2), lambda p, r, b: (b, p)),
            pl.BlockSpec((2, 2 * NA_WIN_ROWS, GRID_W, 2 * GRID_W), lambda p, r, b: (p, 0, 0, 0)),
        ],
        out_specs=pl.BlockSpec((nq, w2), lambda p, r, b: (b * nrb + r, p)),
        scratch_shapes=[pltpu.VMEM((2, nq, NA_KR * GRID_W), F32), pltpu.VMEM((nq, NA_KR * GRID_W), F32)],
        compiler_params=_cparams(("parallel", "parallel", "arbitrary")),
        name="na_latent",
    )(qkv_l, qkv_l, qkv_l, kc, vc, t2)


def _gla_pos(t):
    is_ctx = t < GLA_CTX_STEPS
    u = jnp.maximum(t - GLA_CTX_STEPS, 0)
    seq = jnp.where(is_ctx, t // GLA_CTX_CHUNKS, u // GLA_LAT_CHUNKS)
    n = jnp.where(is_ctx, t % GLA_CTX_CHUNKS, u % GLA_LAT_CHUNKS)
    return is_ctx, seq, n


def _gla_bwd_chunk(t):
    is_ctx, _, n = _gla_pos(t)
    return t + jnp.where(is_ctx, GLA_CTX_CHUNKS, GLA_LAT_CHUNKS) - 1 - 2 * n


def _gla_kernel(qf_ref, kf_ref, vf_ref, gdf_ref, qb_ref, kb_ref, vb_ref, gdb_ref, wgu_ref, bg_ref, s0_ref,
                of_ref, ob_ref, fin_ref, st_ref):
    is_ctx, _, n = _gla_pos(pl.program_id(0))
    is_lat = jnp.logical_not(is_ctx)

    @pl.when(jnp.logical_and(n == 0, is_ctx))
    def _():
        st_ref[...] = jnp.zeros_like(st_ref)

    @pl.when(jnp.logical_and(n == 0, is_lat))
    def _():
        st_ref[...] = s0_ref[...]

    ri = lax.broadcasted_iota(jnp.int32, (GLA_CHUNK, GLA_CHUNK), 0)
    ci = lax.broadcasted_iota(jnp.int32, (GLA_CHUNK, GLA_CHUNK), 1)
    streams = ((qf_ref, kf_ref, vf_ref, gdf_ref, of_ref), (qb_ref, kb_ref, vb_ref, gdb_ref, ob_ref))
    for d, (q_ref, k_ref, v_ref, gd_ref, o_ref) in enumerate(streams):
        keep = (ci <= ri) if d == 0 else (ci >= ri)
        tri = jnp.where(keep, 1.0, 0.0).astype(BF16)
        pre = _dot(gd_ref[...].astype(BF16), wgu_ref[d]) + bg_ref[d]
        la = (jnp.minimum(pre, 0.0) - jnp.log1p(jnp.exp(-jnp.abs(pre)))) / GLA_TAU
        hi = la.astype(BF16)
        r1 = la - hi.astype(F32)
        mid = r1.astype(BF16)
        lo = (r1 - mid.astype(F32)).astype(BF16)
        cum = _dot(tri, hi) + _dot(tri, mid) + _dot(tri, lo)
        tot = jnp.sum(la, axis=0, keepdims=True)
        k = k_ref[...]
        q_dec = (q_ref[...] * (GLA_DK ** -0.5) * jnp.exp(cum)).astype(BF16)
        k_inv = (k * jnp.exp(-cum)).astype(BF16)
        k_end = (k * jnp.exp(tot - cum)).astype(BF16)
        dec = jnp.exp(tot)
        v = v_ref[...].astype(BF16)
        for h in range(GLA_HEADS):
            ks = slice(h * GLA_DK, (h + 1) * GLA_DK)
            vs = slice(h * GLA_DV, (h + 1) * GLA_DV)
            att = jnp.where(keep, _dot_nt(q_dec[:, ks], k_inv[:, ks]), 0.0)
            st = st_ref[d, h]
            o_ref[:, vs] = _dot(att.astype(BF16), v[:, vs]) + _dot_nt(q_dec[:, ks], st.astype(BF16))
            st_ref[d, h] = st * dec[:, ks] + _dot_tn(v[:, vs], k_end[:, ks])

    @pl.when(jnp.logical_and(is_ctx, n == GLA_CTX_CHUNKS - 1))
    def _():
        fin_ref[...] = st_ref[...]


def _gla_scan(proj, wgu, bg, s0t):
    hk = GLA_HEADS * GLA_DK
    hv = GLA_HEADS * GLA_DV
    gd_col = GLA_QKVG_N // 128

    def fwd(c):
        return lambda t: (t, c)

    def bwd(c):
        return lambda t: (_gla_bwd_chunk(t), c)

    in_specs = []
    for ix in (fwd, bwd):
        in_specs += [pl.BlockSpec((GLA_CHUNK, hk), ix(0)), pl.BlockSpec((GLA_CHUNK, hk), ix(1)),
                     pl.BlockSpec((GLA_CHUNK, hv), ix(1)), pl.BlockSpec((GLA_CHUNK, 128), ix(gd_col))]
    st_block = (None, 2, GLA_HEADS, GLA_DV, GLA_DK)
    in_specs += [
        pl.BlockSpec((2, 128, hk), lambda t: (0, 0, 0)),
        pl.BlockSpec((2, 1, hk), lambda t: (0, 0, 0)),
        pl.BlockSpec(st_block, lambda t: (jnp.where(_gla_pos(t)[0], 0, _gla_pos(t)[1]), 0, 0, 0, 0)),
    ]
    o_shape = jax.ShapeDtypeStruct((N_TOK, hv), F32)
    return pl.pallas_call(
        _gla_kernel,
        out_shape=(o_shape, o_shape, jax.ShapeDtypeStruct((BATCH, 2, GLA_HEADS, GLA_DV, GLA_DK), F32)),
        grid=(GLA_STEPS,),
        in_specs=in_specs,
        out_specs=(
            pl.BlockSpec((GLA_CHUNK, hv), lambda t: (t, 0)),
            pl.BlockSpec((GLA_CHUNK, hv), lambda t: (_gla_bwd_chunk(t), 0)),
            pl.BlockSpec(st_block, lambda t: (jnp.where(_gla_pos(t)[0], _gla_pos(t)[1], BATCH - 1), 0, 0, 0, 0)),
        ),
        scratch_shapes=[pltpu.VMEM((2, GLA_HEADS, GLA_DV, GLA_DK), F32)],
        compiler_params=_cparams(("arbitrary",)),
        name="gla_scan",
    )(*([proj] * 8), wgu, bg, s0t)


def _rope_tables():
    t = jnp.arange(DEC_SEQ)
    d = MLA_ROPE // 2
    inv = ROPE_THETA ** (-jnp.arange(0, d, 2, dtype=F32) / d)
    ang_r = (t // GRID_W).astype(F32)[:, None] * inv[None]
    ang_c = (t % GRID_W).astype(F32)[:, None] * inv[None]
    cos = jnp.concatenate([jnp.cos(ang_r)] * 2 + [jnp.cos(ang_c)] * 2, axis=-1)
    sin = jnp.concatenate([-jnp.sin(ang_r), jnp.sin(ang_r), -jnp.sin(ang_c), jnp.sin(ang_c)], axis=-1)
    return jnp.concatenate([cos, cos], axis=-1), jnp.concatenate([sin, sin], axis=-1)


def _mixer_na(x, mod, layer, g, w_qkv, w_o, rpb, cache_k, cache_v):
    w_qkv = w_qkv.astype(BF16)
    ada = dict(pro="ada", g=g, mod=mod, layer=layer, jmod=1)
    q_c, k_c, v_c = _mm(x, w_qkv, rows=N_CTX, n_outs=3, out_dtype=F32, tm=512, **ada)
    qkv_l = _mm(x, w_qkv, rows=N_LAT, row_off=N_CTX // 1024, out_dtype=BF16, tn=1536, **ada)
    o_c = _attention(q_c, [dict(k=(k_c, 0, 0), v=(v_c, 0, 0), lk=SEQ)],
                     nb=BATCH, lq=SEQ, tq=SEQ, q_row0=0, q_col=0, scale=NA_DH ** -0.5, pps=CTX_PPS)
    kc = cache_k.reshape(DEC_BATCH * PAST, D).astype(BF16)
    vc = cache_v.reshape(DEC_BATCH * PAST, D).astype(BF16)
    o_l = _na_latent(qkv_l, kc, vc, _na_bias_table(rpb))
    x = _mm(o_c, w_o.astype(BF16), rows=N_TOK, pro="cast", x2=o_l, epi="resid", res=x, mod=mod, layer=layer,
            jmod=1, tm=512, tn=2048)
    return x, k_c.reshape(BATCH, SEQ, NA_HEADS, NA_DH), v_c.reshape(BATCH, SEQ, NA_HEADS, NA_DH)


def _mixer_mla(x, mod, layer, g, w_down, g_q, w_uq, g_kv, w_ukv, w_o, cache_ckv, cache_kr):
    hq = MLA_NOPE + MLA_ROPE
    wd = jnp.concatenate([
        w_down[:, :MLA_Q_LORA],
        w_down[:, MLA_Q_LORA + MLA_KV_LORA:],
        jnp.zeros((D, 256 - MLA_ROPE), F32),
        w_down[:, MLA_Q_LORA:MLA_Q_LORA + MLA_KV_LORA]], axis=1).astype(BF16)
    wq = w_uq.reshape(MLA_Q_LORA, MLA_HEADS, hq)
    wq = jnp.concatenate([wq[:, :, :MLA_NOPE].reshape(MLA_Q_LORA, -1),
                          wq[:, :, MLA_NOPE:].reshape(MLA_Q_LORA, -1)], axis=1).astype(BF16)
    wkv = w_ukv.reshape(MLA_KV_LORA, MLA_HEADS, MLA_NOPE + MLA_V)
    wkv = jnp.concatenate([wkv[:, :, :MLA_NOPE].reshape(MLA_KV_LORA, -1),
                           wkv[:, :, MLA_NOPE:].reshape(MLA_KV_LORA, -1)], axis=1).astype(BF16)

    down = _mm(x, wd, rows=N_TOK, pro="ada", g=g, mod=mod, layer=layer, jmod=1, out_dtype=F32, tn=MLA_DOWN_N)
    q = _mm(down, wq, rows=N_TOK, pro="rms", g=g_q, out_dtype=F32, tn=1536)
    kv = _mm(down, wkv, rows=N_TOK, xcol=2, pro="rms", g=g_kv, out_dtype=BF16, tn=2048)
    kv_cache = _mm(cache_ckv.reshape(DEC_BATCH * PAST, MLA_KV_LORA).astype(BF16), wkv,
                   rows=DEC_BATCH * PAST, pro="cast", out_dtype=BF16, tn=2048)
    ckv_c = _rmsnorm(down, g_kv, rows=N_CTX, xcol=2)
    kr_c = down[:N_CTX, MLA_Q_LORA:MLA_Q_LORA + MLA_ROPE]

    npairs = MLA_HEADS // 2
    kr_col = MLA_Q_LORA // 128
    o_c = _attention(q, [dict(k=(kv, 0, 0), v=(kv, 0, npairs), kr=(down, 0, kr_col, 128), lk=SEQ)],
                     nb=BATCH, lq=SEQ, tq=SEQ, q_row0=0, q_col=0, qr=(q, 2 * npairs), scale=MLA_SCALE, pps=CTX_PPS)
    tabs = _rope_tables()
    kr_l = _rope_keys(down, tabs)
    o_l = _attention(
        q,
        [dict(k=(kv, N_CTX, 0), v=(kv, N_CTX, npairs), kr=(kr_l, 0, 0, 128), lk=DEC_SEQ),
         dict(k=(kv_cache, 0, 0), v=(kv_cache, 0, npairs),
              kr=(cache_kr.reshape(DEC_BATCH * PAST, MLA_ROPE), 0, 0, MLA_ROPE), lk=PAST)],
        nb=DEC_BATCH, lq=DEC_SEQ, tq=1024, q_row0=N_CTX, q_col=0, qr=(q, 2 * npairs), scale=MLA_SCALE,
        rope_tabs=tabs)
    x = _mm(o_c, w_o.astype(BF16), rows=N_TOK, pro="cast", x2=o_l, epi="resid", res=x, mod=mod, layer=layer,
            jmod=1, tm=512, tn=2048)
    return x, ckv_c.reshape(BATCH, SEQ, MLA_KV_LORA), kr_c.reshape(BATCH, SEQ, MLA_ROPE)


def _mixer_gla(x, mod, layer, g, w_in, w_gd, w_gu, b_g, g_norm, w_o, state):
    pad = jnp.zeros((D, GLA_PROJ_N - GLA_QKVG_N - 2 * GLA_RANK), F32)
    w_cat = jnp.concatenate([w_in, w_gd[0], w_gd[1], pad], axis=1).astype(BF16)
    proj = _mm(x, w_cat, rows=N_TOK, pro="ada", g=g, mod=mod, layer=layer, jmod=1, out_dtype=F32, tn=1280)
    wgu = jnp.zeros((2, 128, GLA_HEADS * GLA_DK), F32)
    wgu = wgu.at[0, :GLA_RANK].set(w_gu[0]).at[1, GLA_RANK:2 * GLA_RANK].set(w_gu[1]).astype(BF16)
    bg = b_g.reshape(2, 1, GLA_HEADS * GLA_DK)
    o_f, o_b, st_c = _gla_scan(proj, wgu, bg, jnp.swapaxes(state, -1, -2))
    x = _mm(o_f, w_o.astype(BF16), rows=N_TOK, pro="gla", x2=o_b, gin=proj, gin_col=2, g=g_norm,
            epi="resid", res=x, mod=mod, layer=layer, jmod=1, tm=256, tn=2048)
    return x, jnp.swapaxes(st_c, -1, -2)


def kernel(x_prompt, x_sample, cache_na_k, cache_na_v, cache_mla_ckv, cache_mla_krope, state_gla, c, c_ctx, norm_g, w_ada, b_ada, w_ffn_in, w_ffn_out, w_na_qkv, w_na_o, na_rpb, w_mla_down, g_mla_q, w_mla_uq, g_mla_kv, w_mla_ukv, w_mla_o, w_gla_in, w_gla_gate_down, w_gla_gate_up, b_gla_gate, g_gla_norm, w_gla_o, final_norm_g):
    x = jnp.concatenate([x_prompt.reshape(N_CTX, D), x_sample.reshape(N_LAT, D)], axis=0)
    cond = jnp.concatenate([c_ctx[None], c, jnp.zeros((N_GROUPS - 1 - DEC_BATCH, D), F32)], axis=0)
    mod = _ada_mod(cond, w_ada, b_ada)
    w_ffn = (w_ffn_in[0, 0].astype(BF16), w_ffn_out[0, 0].astype(BF16))

    def ffn(x, w_ffn, layer, half):
        last = layer == DEPTH - 1 and half == 1
        nxt = None if last else (w_ffn_in, w_ffn_out, layer + half, 1 - half)
        out = _ffn(x, mod, layer, half, w_ffn[0], w_ffn[1], norm_g[layer, 2 * half], nxt)
        return (out, None) if last else out

    na_k, na_v, ckv, krope, gla_st = [], [], [], [], []
    for i in range(DEPTH):
        kind, slot = i % 3, i // 3
        x, w_ffn = ffn(x, w_ffn, i, 0)
        if kind == 0:
            x, k_c, v_c = _mixer_na(x, mod, i, norm_g[i, 1], w_na_qkv[slot], w_na_o[slot], na_rpb[slot],
                                    cache_na_k[:, slot], cache_na_v[:, slot])
            na_k.append(k_c)
            na_v.append(v_c)
        elif kind == 1:
            x, ckv_c, kr_c = _mixer_mla(x, mod, i, norm_g[i, 1], w_mla_down[slot], g_mla_q[slot], w_mla_uq[slot],
                                        g_mla_kv[slot], w_mla_ukv[slot], w_mla_o[slot],
                                        cache_mla_ckv[:, slot], cache_mla_krope[:, slot])
            ckv.append(ckv_c)
            krope.append(kr_c)
        else:
            x, st = _mixer_gla(x, mod, i, norm_g[i, 1], w_gla_in[slot], w_gla_gate_down[slot],
                               w_gla_gate_up[slot], b_gla_gate[slot], g_gla_norm[slot], w_gla_o[slot],
                               state_gla[:, slot])
            gla_st.append(st)
        x, w_ffn = ffn(x, w_ffn, i, 1)

    y_prompt = _rmsnorm(x, final_norm_g, rows=N_CTX).reshape(BATCH, SEQ, D)
    y_sample = _rmsnorm(x, final_norm_g, rows=N_LAT, row_off=N_CTX // 512).reshape(DEC_BATCH, DEC_SEQ, D)
    return (y_prompt, y_sample, jnp.stack(na_k, axis=1), jnp.stack(na_v, axis=1), jnp.stack(ckv, axis=1),
            jnp.stack(krope, axis=1), jnp.stack(gla_st, axis=1))
```

```python
import functools

import jax
import jax.numpy as jnp
from jax import lax
from jax.experimental import pallas as pl
from jax.experimental.pallas import tpu as pltpu

F32 = jnp.float32
BF16 = jnp.bfloat16

D = 2048
BATCH, SEQ = 16, 256
DEC_BATCH, DEC_SEQ = 4, 2048
PAST = 512
DEPTH = 4
N_MOD = 9
EPS = 1e-6
D_FF = 5632
GRID_W = 64
LOG2_GRID_W = 6
NEG_INF = -1e30
LOG2E = 1.4426950408889634

N_CTX = BATCH * SEQ
N_LAT = DEC_BATCH * DEC_SEQ
N_TOK = N_CTX + N_LAT
N_GROUPS = 8

NA_HEADS, NA_DH = 16, 128
NA_WIN_ROWS, NA_WIN_COLS = 8, 16
LAT_ROWS = DEC_SEQ // GRID_W
NA_QR = 8
NA_KR = NA_QR + NA_WIN_ROWS

MLA_HEADS = 16
MLA_Q_LORA, MLA_KV_LORA = 768, 512
MLA_NOPE, MLA_ROPE, MLA_V = 128, 64, 128
MLA_SCALE = (MLA_NOPE + MLA_ROPE) ** -0.5
ROPE_THETA = 10000.0
MLA_DOWN_N = 1536

GLA_HEADS, GLA_DK, GLA_DV = 4, 256, 512
GLA_RANK = 16
GLA_TAU = 16.0
GLA_CHUNK = 128
GLA_QKVG_N = 2 * GLA_HEADS * GLA_DK + 2 * GLA_HEADS * GLA_DV
GLA_PROJ_N = GLA_QKVG_N + 256
GLA_CTX_CHUNKS = SEQ // GLA_CHUNK
GLA_LAT_CHUNKS = DEC_SEQ // GLA_CHUNK
GLA_CTX_STEPS = BATCH * GLA_CTX_CHUNKS
GLA_STEPS = GLA_CTX_STEPS + DEC_BATCH * GLA_LAT_CHUNKS

VMEM_LIMIT = 56 * 1024 * 1024
ROW_CHUNK = 16
CTX_PPS = 8
FFN_TF = 512
FFN_SUB = 512
FFN_CVT_IN_ROWS = 16
FFN_CVT_OUT_ROWS = 64


def _cparams(sem):
    return pltpu.CompilerParams(dimension_semantics=sem, vmem_limit_bytes=VMEM_LIMIT)


def _group_of_row(r0):
    return jnp.where(r0 < N_CTX, 0, 1 + (r0 - N_CTX) // DEC_SEQ)


def _silu(x):
    return x / (1.0 + jnp.exp(-x))


def _rms(x):
    return x * lax.rsqrt(jnp.mean(x * x, axis=-1, keepdims=True) + EPS)


def _for_row_chunks(n_rows, body):
    def step(c, carry):
        body(pl.ds(pl.multiple_of(c * ROW_CHUNK, ROW_CHUNK), ROW_CHUNK))
        return carry
    lax.fori_loop(0, n_rows // ROW_CHUNK, step, 0, unroll=8)


def _dot(a, b):
    return jnp.dot(a, b, preferred_element_type=F32)


def _dot_nt(a, b):
    return lax.dot_general(a, b, (((1,), (1,)), ((), ())), preferred_element_type=F32)


def _dot_tn(a, b):
    return lax.dot_general(a, b, (((0,), (0,)), ((), ())), preferred_element_type=F32)


def _ada_kernel(c_ref, w_ref, b_ref, o_ref):
    s = _silu(c_ref[...]).astype(BF16)
    o_ref[...] = _dot(s, w_ref[...].astype(BF16)) + b_ref[...]


def _ada_mod(cond, w_ada, b_ada):
    tn = 1024
    n = N_MOD * D
    out = pl.pallas_call(
        _ada_kernel,
        out_shape=jax.ShapeDtypeStruct((DEPTH, N_GROUPS, n), F32),
        grid=(DEPTH, n // tn),
        in_specs=[
            pl.BlockSpec((N_GROUPS, D), lambda l, j: (0, 0)),
            pl.BlockSpec((None, D, tn), lambda l, j: (l, 0, j)),
            pl.BlockSpec((None, 1, tn), lambda l, j: (l, 0, j)),
        ],
        out_specs=pl.BlockSpec((None, N_GROUPS, tn), lambda l, j: (l, 0, j)),
        compiler_params=_cparams(("parallel", "parallel")),
        name="ada_mod",
    )(cond, w_ada, b_ada.reshape(DEPTH, 1, n))
    return out.reshape(DEPTH, N_GROUPS, N_MOD, 1, D)


def _mod_spec(layer, j, tm, row_off):
    return pl.BlockSpec(
        (None, None, None, 1, D),
        lambda i, n: (layer, _group_of_row((i + row_off) * tm), j, 0, 0))


def _ffn_kernel(*refs, convert_next):
    if convert_next:
        (x_ref, g_ref, sh_ref, sc_ref, gt_ref, wg_ref, wu_ref, wo_ref, nin_ref, nout_ref,
         o_ref, cin_ref, cout_ref, h_ref, gm_ref) = refs
    else:
        x_ref, g_ref, sh_ref, sc_ref, gt_ref, wg_ref, wu_ref, wo_ref, o_ref, h_ref, gm_ref = refs
    f = pl.program_id(1)
    tm = x_ref.shape[0]

    @pl.when(f == 0)
    def _():
        gm_ref[...] = g_ref[...] * (1.0 + sc_ref[...])

        def rows_fn(rows):
            h_ref[rows, :] = (_rms(x_ref[rows, :]) * gm_ref[...] + sh_ref[...]).astype(BF16)
            o_ref[rows, :] = jnp.zeros((ROW_CHUNK, D), F32)

        _for_row_chunks(tm, rows_fn)

    for r in range(0, tm, FFN_SUB):
        h = h_ref[r:r + FFN_SUB, :]
        a = _silu(_dot(h, wg_ref[...])) * _dot(h, wu_ref[...])
        o_ref[r:r + FFN_SUB, :] += _dot(a.astype(BF16), wo_ref[...])

    @pl.when(f == pl.num_programs(1) - 1)
    def _():
        o_ref[...] = x_ref[...] + 0.5 * gt_ref[...] * o_ref[...]

    if convert_next:
        cin_ref[...] = nin_ref[...].astype(BF16)
        cout_ref[...] = nout_ref[...].astype(BF16)


def _ffn(x, mod, layer, half, w_in, w_out, g, nxt=None, *, tm=1024):
    tf = FFN_TF
    nf = D_FF // tf
    j = 2 * half
    in_specs = [
        pl.BlockSpec((tm, D), lambda i, f: (i, 0)),
        pl.BlockSpec((1, D), lambda i, f: (0, 0)),
        _mod_spec(layer, 3 * j, tm, 0),
        _mod_spec(layer, 3 * j + 1, tm, 0),
        _mod_spec(layer, 3 * j + 2, tm, 0),
        pl.BlockSpec((D, tf), lambda i, f: (0, f)),
        pl.BlockSpec((D, tf), lambda i, f: (0, nf + f)),
        pl.BlockSpec((tf, D), lambda i, f: (f, 0)),
    ]
    args = [x, g.reshape(1, D), mod, mod, mod, w_in, w_in, w_out]
    out_shape = [jax.ShapeDtypeStruct((N_TOK, D), F32)]
    out_specs = [pl.BlockSpec((tm, D), lambda i, f: (i, 0))]
    if nxt is not None:
        nw_in, nw_out, nl, nh = nxt
        steps = (N_TOK // tm) * nf
        n_in, n_out = D // FFN_CVT_IN_ROWS, D_FF // FFN_CVT_OUT_ROWS
        assert n_in <= steps and n_out <= steps

        def slab(last):
            return lambda i, f: jnp.minimum(i * nf + f, last)

        s_in, s_out = slab(n_in - 1), slab(n_out - 1)
        in_specs += [pl.BlockSpec((None, None, FFN_CVT_IN_ROWS, 2 * D_FF), lambda i, f: (nl, nh, s_in(i, f), 0)),
                     pl.BlockSpec((None, None, FFN_CVT_OUT_ROWS, D), lambda i, f: (nl, nh, s_out(i, f), 0))]
        args += [nw_in, nw_out]
        out_shape += [jax.ShapeDtypeStruct((D, 2 * D_FF), BF16), jax.ShapeDtypeStruct((D_FF, D), BF16)]
        out_specs += [pl.BlockSpec((FFN_CVT_IN_ROWS, 2 * D_FF), lambda i, f: (s_in(i, f), 0)),
                      pl.BlockSpec((FFN_CVT_OUT_ROWS, D), lambda i, f: (s_out(i, f), 0))]
    outs = pl.pallas_call(
        functools.partial(_ffn_kernel, convert_next=nxt is not None),
        out_shape=out_shape,
        grid=(N_TOK // tm, nf),
        in_specs=in_specs,
        out_specs=out_specs,
        scratch_shapes=[pltpu.VMEM((tm, D), BF16), pltpu.VMEM((1, D), F32)],
        compiler_params=_cparams(("arbitrary", "arbitrary")),
        name="ffn_half",
    )(*args)
    return outs[0] if nxt is None else (outs[0], (outs[1], outs[2]))


def _mm_kernel(*refs, pro, epi, split, n_outs, tiles_per_out):
    it = iter(refs)
    x_ref = next(it)
    if pro == "cast" and split is not None:
        x2_ref = next(it)
    if pro == "gla":
        x2_ref, gin_ref = next(it), next(it)
    if pro in ("ada", "rms", "gla"):
        g_ref = next(it)
    if pro == "ada":
        sh_ref, sc_ref = next(it), next(it)
    w_ref = next(it)
    if epi == "resid":
        res_ref, gt_ref = next(it), next(it)
    o_refs = [next(it) for _ in range(n_outs)]
    if pro != "cast":
        h_ref = next(it)
    if pro == "ada":
        gm_ref = next(it)

    if pro != "cast":
        @pl.when(pl.program_id(1) == 0)
        def _():
            if pro == "ada":
                gm_ref[...] = g_ref[...] * (1.0 + sc_ref[...])

            def rows_fn(rows):
                if pro == "ada":
                    h = _rms(x_ref[rows, :]) * gm_ref[...] + sh_ref[...]
                elif pro == "rms":
                    h = _rms(x_ref[rows, :]) * g_ref[...]
                else:
                    o = x_ref[rows, :] + x2_ref[rows, :]
                    parts = [_rms(o[:, k * GLA_DV:(k + 1) * GLA_DV]) * g_ref[...] for k in range(GLA_HEADS)]
                    h = jnp.concatenate(parts, axis=-1) * _silu(gin_ref[rows, :])
                h_ref[rows, :] = h.astype(BF16)

            _for_row_chunks(x_ref.shape[0], rows_fn)

    def finish(lhs_ref):
        y = _dot(lhs_ref[...], w_ref[...])
        if epi == "resid":
            y = res_ref[...] + gt_ref[...] * y
        if n_outs == 1:
            o_refs[0][...] = y.astype(o_refs[0].dtype)
        else:
            for k, o_ref in enumerate(o_refs):
                @pl.when(pl.program_id(1) // tiles_per_out == k)
                def _(o_ref=o_ref):
                    o_ref[...] = y.astype(o_ref.dtype)

    if pro != "cast":
        finish(h_ref)
    elif split is None:
        finish(x_ref)
    else:
        pl.when(pl.program_id(0) < split)(lambda: finish(x_ref))
        pl.when(pl.program_id(0) >= split)(lambda: finish(x2_ref))


def _mm(x, w, *, rows, row_off=0, xcol=0, n_outs=1, pro, epi="plain", out_dtype=F32, tm=1024, tn=1024,
        g=None, mod=None, layer=None, jmod=None, x2=None, gin=None, gin_col=0, res=None):
    kdim, n = w.shape
    tn = min(tn, n // n_outs)
    tiles_per_out = n // n_outs // tn
    assert rows % tm == 0 and n % (n_outs * tn) == 0 and (n_outs == 1 or epi == "plain")
    split = None
    if pro == "cast" and x2 is not None:
        split = x.shape[0] // tm
        xspec = pl.BlockSpec((tm, kdim), lambda i, j: (jnp.minimum(i, split - 1), 0))
        args, specs = [x, x2], [xspec, pl.BlockSpec((tm, kdim), lambda i, j: (jnp.maximum(i - split, 0), 0))]
    else:
        xspec = pl.BlockSpec((tm, kdim), lambda i, j: (i + row_off, xcol))
        args, specs = [x], [xspec]
    if pro == "gla":
        args += [x2, gin]
        specs += [xspec, pl.BlockSpec((tm, kdim), lambda i, j: (i + row_off, gin_col))]
    if pro in ("ada", "rms", "gla"):
        args.append(g.reshape(1, -1))
        specs.append(pl.BlockSpec((1, g.shape[-1]), lambda i, j: (0, 0)))
    if pro == "ada":
        args += [mod, mod]
        specs += [_mod_spec(layer, 3 * jmod, tm, row_off), _mod_spec(layer, 3 * jmod + 1, tm, row_off)]
    args.append(w)
    specs.append(pl.BlockSpec((kdim, tn), lambda i, j: (0, j)))
    if epi == "resid":
        args += [res, mod]
        specs += [
            pl.BlockSpec((tm, tn), lambda i, j: (i + row_off, j)),
            pl.BlockSpec((None, None, None, 1, tn),
                         lambda i, j: (layer, _group_of_row((i + row_off) * tm), 3 * jmod + 2, 0, j)),
        ]
    scratch = []
    if pro != "cast":
        scratch.append(pltpu.VMEM((tm, kdim), BF16))
    if pro == "ada":
        scratch.append(pltpu.VMEM((1, kdim), F32))
    out_specs = [pl.BlockSpec((tm, tn), functools.partial(
        lambda i, j, k: (i, jnp.clip(j - k * tiles_per_out, 0, tiles_per_out - 1)), k=k)) for k in range(n_outs)]
    outs = pl.pallas_call(
        functools.partial(_mm_kernel, pro=pro, epi=epi, split=split, n_outs=n_outs, tiles_per_out=tiles_per_out),
        out_shape=[jax.ShapeDtypeStruct((rows, n // n_outs), out_dtype)] * n_outs,
        grid=(rows // tm, n // tn),
        in_specs=specs,
        out_specs=out_specs,
        scratch_shapes=scratch,
        compiler_params=_cparams(("parallel", "arbitrary")),
        name="proj_" + pro + "_" + epi,
    )(*args)
    return outs[0] if n_outs == 1 else outs


def _rmsnorm_kernel(x_ref, g_ref, o_ref):
    o_ref[...] = _rms(x_ref[...]) * g_ref[...]


def _rmsnorm(x, g, *, rows, row_off=0, xcol=0, tm=512):
    width = g.shape[-1]
    return pl.pallas_call(
        _rmsnorm_kernel,
        out_shape=jax.ShapeDtypeStruct((rows, width), F32),
        grid=(rows // tm,),
        in_specs=[pl.BlockSpec((tm, width), lambda i: (i + row_off, xcol)),
                  pl.BlockSpec((1, width), lambda i: (0, 0))],
        out_specs=pl.BlockSpec((tm, width), lambda i: (i, 0)),
        compiler_params=_cparams(("parallel",)),
        name="rmsnorm",
    )(x, g.reshape(1, width))


def _rope(x, cos, sin):
    width = x.shape[-1]
    lane = lax.broadcasted_iota(jnp.int32, x.shape, 1)
    up = pltpu.roll(x, width - 16, 1)
    down = pltpu.roll(x, 16, 1)
    swapped = jnp.where((lane & 31) < 16, up, down)
    return x * cos + swapped * sin


def _rope_keys_kernel(x_ref, cos_ref, sin_ref, o_ref):
    o_ref[...] = _rope(x_ref[...], cos_ref[...], sin_ref[...]).astype(o_ref.dtype)


def _rope_keys(down, rope_tabs, *, tm=1024):
    cos, sin = rope_tabs
    per_seq = DEC_SEQ // tm
    return pl.pallas_call(
        _rope_keys_kernel,
        out_shape=jax.ShapeDtypeStruct((N_LAT, 128), BF16),
        grid=(N_LAT // tm,),
        in_specs=[pl.BlockSpec((tm, 128), lambda i: (N_CTX // tm + i, MLA_Q_LORA // 128)),
                  pl.BlockSpec((tm, 128), lambda i: (i % per_seq, 0)),
                  pl.BlockSpec((tm, 128), lambda i: (i % per_seq, 0))],
        out_specs=pl.BlockSpec((tm, 128), lambda i: (i, 0)),
        compiler_params=_cparams(("parallel",)),
        name="rope_keys",
    )(down, cos, sin)


def _attn_kernel(*refs, nseg, has_r, rope, scale, dh, dv, pps):
    it = iter(refs)
    q_ref = next(it)
    qr_ref = next(it) if has_r else None
    segs = []
    for _ in range(nseg):
        k_ref = next(it)
        kr_ref = next(it) if has_r else None
        v_ref = next(it)
        segs.append((k_ref, kr_ref, v_ref))
    if rope:
        cq_ref, sq_ref = next(it), next(it)
    o_ref = next(it)

    if has_r:
        qr = qr_ref[...].astype(F32)
        if rope:
            qr = _rope(qr, cq_ref[...], sq_ref[...])
        qrs = [qr[:, h * MLA_ROPE:(h + 1) * MLA_ROPE].astype(BF16) for h in range(2 * pps)]
        krs = [kr_ref[:, :MLA_ROPE].astype(BF16) for _, kr_ref, _ in segs]

    nh = 2 * pps

    def qk(h):
        q = q_ref[:, h * dh:(h + 1) * dh].astype(BF16)
        ss = []
        for si, (k_ref, _, _) in enumerate(segs):
            s = _dot_nt(q, k_ref[:, h * dh:(h + 1) * dh].astype(BF16))
            if has_r:
                s = s + _dot_nt(qrs[h], krs[si])
            ss.append(s * (scale * LOG2E))
        return ss

    def softmax(ss):
        m = ss[0].max(axis=-1, keepdims=True)
        for s in ss[1:]:
            m = jnp.maximum(m, s.max(axis=-1, keepdims=True))
        es = [jnp.exp2(s - m) for s in ss]
        den = es[0].sum(axis=-1, keepdims=True)
        for e in es[1:]:
            den = den + e.sum(axis=-1, keepdims=True)
        return [e.astype(BF16) for e in es], den

    def pv(h, es, den):
        acc = None
        for e, (_, _, v_ref) in zip(es, segs):
            part = _dot(e, v_ref[:, h * dv:(h + 1) * dv].astype(BF16))
            acc = part if acc is None else acc + part
        o_ref[:, h * dv:(h + 1) * dv] = (acc / den).astype(o_ref.dtype)

    scores, probs = {}, {}
    for t in range(nh + 2):
        if t < nh:
            scores[t] = qk(t)
        if 0 <= t - 1 < nh:
            probs[t - 1] = softmax(scores.pop(t - 1))
        if 0 <= t - 2 < nh:
            pv(t - 2, *probs.pop(t - 2))


def _attention(q, segs, *, nb, lq, tq, q_row0, q_col, scale, qr=None, rope_tabs=None, pps=1, dh=128, dv=128):
    npairs = 8
    ng = npairs // pps
    nq = lq // tq
    has_r = qr is not None
    q_blk0 = q_row0 // tq
    assert q_col % pps == 0
    args = [q]
    specs = [pl.BlockSpec((tq, 2 * dh * pps), lambda b, p, t: (q_blk0 + b * nq + t, q_col // pps + p))]
    if has_r:
        qr_arr, qr_col = qr
        assert qr_col % pps == 0
        args.append(qr_arr)
        specs.append(pl.BlockSpec((tq, 2 * MLA_ROPE * pps),
                                  lambda b, p, t: (q_blk0 + b * nq + t, qr_col // pps + p)))
    for sg in segs:
        lk = sg["lk"]
        k_arr, k_row0, k_col = sg["k"]
        assert k_col % pps == 0
        args.append(k_arr)
        specs.append(pl.BlockSpec((lk, 2 * dh * pps), functools.partial(
            lambda b, p, t, r0, c0: (r0 + b, c0 + p), r0=k_row0 // lk, c0=k_col // pps)))
        if has_r:
            kr_arr, kr_row0, kr_col, kr_w = sg["kr"]
            args.append(kr_arr)
            specs.append(pl.BlockSpec((lk, kr_w), functools.partial(
                lambda b, p, t, r0, c0: (r0 + b, c0), r0=kr_row0 // lk, c0=kr_col)))
        v_arr, v_row0, v_col = sg["v"]
        assert v_col % pps == 0
        args.append(v_arr)
        specs.append(pl.BlockSpec((lk, 2 * dv * pps), functools.partial(
            lambda b, p, t, r0, c0: (r0 + b, c0 + p), r0=v_row0 // lk, c0=v_col // pps)))
    if rope_tabs is not None:
        cos, sin = (jnp.concatenate([tab] * pps, axis=-1) for tab in rope_tabs)
        args += [cos, sin]
        specs += [pl.BlockSpec((tq, 128 * pps), lambda b, p, t: (t, 0)),
                  pl.BlockSpec((tq, 128 * pps), lambda b, p, t: (t, 0))]
    return pl.pallas_call(
        functools.partial(_attn_kernel, nseg=len(segs), has_r=has_r, rope=rope_tabs is not None,
                          scale=scale, dh=dh, dv=dv, pps=pps),
        out_shape=jax.ShapeDtypeStruct((nb * lq, npairs * 2 * dv), BF16),
        grid=(nb, ng, nq),
        in_specs=specs,
        out_specs=pl.BlockSpec((tq, 2 * dv * pps), lambda b, p, t: (b * nq + t, p)),
        compiler_params=_cparams(("parallel", "parallel", "arbitrary")),
        name="attention",
    )(*args)


def _na_kernel(q_ref, k_ref, v_ref, kc_ref, vc_ref, t2_ref, o_ref, bias_ref, cap_ref):
    r0 = pl.program_id(1) * NA_QR
    ks = jnp.clip(r0 - NA_WIN_ROWS // 2, 0, LAT_ROWS - NA_KR)
    nq, nk = NA_QR * GRID_W, NA_KR * GRID_W
    scale = NA_DH ** -0.5 * LOG2E

    @pl.when(pl.program_id(2) == 0)
    def _():
        row = lax.broadcasted_iota(jnp.int32, (nq, nk), 0)
        lane = lax.broadcasted_iota(jnp.int32, (nq, nk), 1)
        qc = row & (GRID_W - 1)
        kc = lane & (GRID_W - 1)
        rs = jnp.clip(r0 + (row >> LOG2_GRID_W) - NA_WIN_ROWS // 2, 0, LAT_ROWS - NA_WIN_ROWS)
        kr = ks + (lane >> LOG2_GRID_W)
        cs = jnp.clip(qc - NA_WIN_COLS // 2, 0, GRID_W - NA_WIN_COLS)
        ok = (kr >= rs) & (kr < rs + NA_WIN_ROWS) & (kc >= cs) & (kc < cs + NA_WIN_COLS)
        cap_ref[...] = jnp.where(ok, jnp.inf, NEG_INF)
        for hh in range(2):
            for i in range(NA_QR):
                for jp in range(NA_KR // 2):
                    e = jnp.clip(ks - r0 + 2 * jp - i + NA_WIN_ROWS, 0, 2 * NA_WIN_ROWS - 1)
                    bias_ref[hh, i * GRID_W:(i + 1) * GRID_W, jp * 128:(jp + 1) * 128] = t2_ref[hh, e] * LOG2E

    start = pl.multiple_of(ks * GRID_W, GRID_W)
    kw = k_ref[pl.ds(start, nk), :]
    vw = v_ref[pl.ds(start, nk), :]
    cap = cap_ref[...]
    scores = []
    for hh in range(2):
        sl = slice(hh * NA_DH, (hh + 1) * NA_DH)
        q = q_ref[:, sl]
        scores.append((jnp.minimum(_dot_nt(q, kw[:, sl]) * scale + bias_ref[hh], cap),
                       _dot_nt(q, kc_ref[:, sl]) * scale))
    for hh, (s_loc, s_ctx) in enumerate(scores):
        sl = slice(hh * NA_DH, (hh + 1) * NA_DH)
        m = jnp.maximum(s_loc.max(axis=-1, keepdims=True), s_ctx.max(axis=-1, keepdims=True))
        e_loc = jnp.exp2(s_loc - m)
        e_ctx = jnp.exp2(s_ctx - m)
        den = e_loc.sum(axis=-1, keepdims=True) + e_ctx.sum(axis=-1, keepdims=True)
        acc = _dot(e_loc.astype(BF16), vw[:, sl]) + _dot(e_ctx.astype(BF16), vc_ref[:, sl])
        o_ref[:, sl] = (acc / den).astype(o_ref.dtype)


def _na_bias_table(rpb):
    col = jnp.arange(GRID_W)
    col_off = jnp.clip(col[None, :] - col[:, None] + NA_WIN_COLS - 1, 0, 2 * NA_WIN_COLS - 2)
    e = jnp.arange(2 * NA_WIN_ROWS)
    dr = jnp.clip(jnp.stack([e - 1, e], axis=1), 0, 2 * NA_WIN_ROWS - 2)
    tb = rpb[:, dr][:, :, :, col_off]
    return jnp.transpose(tb, (0, 1, 3, 2, 4)).reshape(NA_HEADS, 2 * NA_WIN_ROWS, GRID_W, 2 * GRID_W)


def _na_latent(qkv_l, kc, vc, t2):
    npairs = NA_HEADS // 2
    w2 = 2 * NA_DH
    nrb = LAT_ROWS // NA_QR
    nq = NA_QR * GRID_W
    return pl.pallas_call(
        _na_kernel,
        out_shape=jax.ShapeDtypeStruct((N_LAT, D), BF16),
        grid=(npairs, nrb, DEC_BATCH),
        in_specs=[
            pl.BlockSpec((nq, w2), lambda p, r, b: (b * nrb + r, p)),
            pl.BlockSpec((DEC_SEQ, w2), lambda p, r, b: (b, npairs + p)),
            pl.BlockSpec((DEC_SEQ, w2), lambda p, r, b: (b, 2 * npairs + p)),
            pl.BlockSpec((PAST, w2), lambda p, r, b: (b, p)),
            pl.BlockSpec((PAST, w2), lambda p, r, b: (b, p)),
            pl.BlockSpec((2, 2 * NA_WIN_ROWS, GRID_W, 2 * GRID_W), lambda p, r, b: (p, 0, 0, 0)),
        ],
        out_specs=pl.BlockSpec((nq, w2), lambda p, r, b: (b * nrb + r, p)),
        scratch_shapes=[pltpu.VMEM((2, nq, NA_KR * GRID_W), F32), pltpu.VMEM((nq, NA_KR * GRID_W), F32)],
        compiler_params=_cparams(("parallel", "parallel", "arbitrary")),
        name="na_latent",
    )(qkv_l, qkv_l, qkv_l, kc, vc, t2)


def _gla_pos(t):
    is_ctx = t < GLA_CTX_STEPS
    u = jnp.maximum(t - GLA_CTX_STEPS, 0)
    seq = jnp.where(is_ctx, t // GLA_CTX_CHUNKS, u // GLA_LAT_CHUNKS)
    n = jnp.where(is_ctx, t % GLA_CTX_CHUNKS, u % GLA_LAT_CHUNKS)
    return is_ctx, seq, n


def _gla_bwd_chunk(t):
    is_ctx, _, n = _gla_pos(t)
    return t + jnp.where(is_ctx, GLA_CTX_CHUNKS, GLA_LAT_CHUNKS) - 1 - 2 * n


def _gla_kernel(qf_ref, kf_ref, vf_ref, gdf_ref, qb_ref, kb_ref, vb_ref, gdb_ref, wgu_ref, bg_ref, s0_ref,
                of_ref, ob_ref, fin_ref, st_ref):
    is_ctx, _, n = _gla_pos(pl.program_id(0))
    is_lat = jnp.logical_not(is_ctx)

    @pl.when(jnp.logical_and(n == 0, is_ctx))
    def _():
        st_ref[...] = jnp.zeros_like(st_ref)

    @pl.when(jnp.logical_and(n == 0, is_lat))
    def _():
        st_ref[...] = s0_ref[...]

    ri = lax.broadcasted_iota(jnp.int32, (GLA_CHUNK, GLA_CHUNK), 0)
    ci = lax.broadcasted_iota(jnp.int32, (GLA_CHUNK, GLA_CHUNK), 1)
    streams = ((qf_ref, kf_ref, vf_ref, gdf_ref, of_ref), (qb_ref, kb_ref, vb_ref, gdb_ref, ob_ref))
    for d, (q_ref, k_ref, v_ref, gd_ref, o_ref) in enumerate(streams):
        keep = (ci <= ri) if d == 0 else (ci >= ri)
        tri = jnp.where(keep, 1.0, 0.0).astype(BF16)
        pre = _dot(gd_ref[...].astype(BF16), wgu_ref[d]) + bg_ref[d]
        la = (jnp.minimum(pre, 0.0) - jnp.log1p(jnp.exp(-jnp.abs(pre)))) / GLA_TAU
        hi = la.astype(BF16)
        r1 = la - hi.astype(F32)
        mid = r1.astype(BF16)
        lo = (r1 - mid.astype(F32)).astype(BF16)
        cum = _dot(tri, hi) + _dot(tri, mid) + _dot(tri, lo)
        tot = jnp.sum(la, axis=0, keepdims=True)
        k = k_ref[...]
        q_dec = (q_ref[...] * (GLA_DK ** -0.5) * jnp.exp(cum)).astype(BF16)
        k_inv = (k * jnp.exp(-cum)).astype(BF16)
        k_end = (k * jnp.exp(tot - cum)).astype(BF16)
        dec = jnp.exp(tot)
        v = v_ref[...].astype(BF16)
        for h in range(GLA_HEADS):
            ks = slice(h * GLA_DK, (h + 1) * GLA_DK)
            vs = slice(h * GLA_DV, (h + 1) * GLA_DV)
            att = jnp.where(keep, _dot_nt(q_dec[:, ks], k_inv[:, ks]), 0.0)
            st = st_ref[d, h]
            o_ref[:, vs] = _dot(att.astype(BF16), v[:, vs]) + _dot_nt(q_dec[:, ks], st.astype(BF16))
            st_ref[d, h] = st * dec[:, ks] + _dot_tn(v[:, vs], k_end[:, ks])

    @pl.when(jnp.logical_and(is_ctx, n == GLA_CTX_CHUNKS - 1))
    def _():
        fin_ref[...] = st_ref[...]


def _gla_scan(proj, wgu, bg, s0t):
    hk = GLA_HEADS * GLA_DK
    hv = GLA_HEADS * GLA_DV
    gd_col = GLA_QKVG_N // 128

    def fwd(c):
        return lambda t: (t, c)

    def bwd(c):
        return lambda t: (_gla_bwd_chunk(t), c)

    in_specs = []
    for ix in (fwd, bwd):
        in_specs += [pl.BlockSpec((GLA_CHUNK, hk), ix(0)), pl.BlockSpec((GLA_CHUNK, hk), ix(1)),
                     pl.BlockSpec((GLA_CHUNK, hv), ix(1)), pl.BlockSpec((GLA_CHUNK, 128), ix(gd_col))]
    st_block = (None, 2, GLA_HEADS, GLA_DV, GLA_DK)
    in_specs += [
        pl.BlockSpec((2, 128, hk), lambda t: (0, 0, 0)),
        pl.BlockSpec((2, 1, hk), lambda t: (0, 0, 0)),
        pl.BlockSpec(st_block, lambda t: (jnp.where(_gla_pos(t)[0], 0, _gla_pos(t)[1]), 0, 0, 0, 0)),
    ]
    o_shape = jax.ShapeDtypeStruct((N_TOK, hv), F32)
    return pl.pallas_call(
        _gla_kernel,
        out_shape=(o_shape, o_shape, jax.ShapeDtypeStruct((BATCH, 2, GLA_HEADS, GLA_DV, GLA_DK), F32)),
        grid=(GLA_STEPS,),
        in_specs=in_specs,
        out_specs=(
            pl.BlockSpec((GLA_CHUNK, hv), lambda t: (t, 0)),
            pl.BlockSpec((GLA_CHUNK, hv), lambda t: (_gla_bwd_chunk(t), 0)),
            pl.BlockSpec(st_block, lambda t: (jnp.where(_gla_pos(t)[0], _gla_pos(t)[1], BATCH - 1), 0, 0, 0, 0)),
        ),
        scratch_shapes=[pltpu.VMEM((2, GLA_HEADS, GLA_DV, GLA_DK), F32)],
        compiler_params=_cparams(("arbitrary",)),
        name="gla_scan",
    )(*([proj] * 8), wgu, bg, s0t)


def _rope_tables():
    t = jnp.arange(DEC_SEQ)
    d = MLA_ROPE // 2
    inv = ROPE_THETA ** (-jnp.arange(0, d, 2, dtype=F32) / d)
    ang_r = (t // GRID_W).astype(F32)[:, None] * inv[None]
    ang_c = (t % GRID_W).astype(F32)[:, None] * inv[None]
    cos = jnp.concatenate([jnp.cos(ang_r)] * 2 + [jnp.cos(ang_c)] * 2, axis=-1)
    sin = jnp.concatenate([-jnp.sin(ang_r), jnp.sin(ang_r), -jnp.sin(ang_c), jnp.sin(ang_c)], axis=-1)
    return jnp.concatenate([cos, cos], axis=-1), jnp.concatenate([sin, sin], axis=-1)


def _mixer_na(x, mod, layer, g, w_qkv, w_o, rpb, cache_k, cache_v):
    w_qkv = w_qkv.astype(BF16)
    ada = dict(pro="ada", g=g, mod=mod, layer=layer, jmod=1)
    q_c, k_c, v_c = _mm(x, w_qkv, rows=N_CTX, n_outs=3, out_dtype=F32, tm=512, **ada)
    qkv_l = _mm(x, w_qkv, rows=N_LAT, row_off=N_CTX // 1024, out_dtype=BF16, tn=1536, **ada)
    o_c = _attention(q_c, [dict(k=(k_c, 0, 0), v=(v_c, 0, 0), lk=SEQ)],
                     nb=BATCH, lq=SEQ, tq=SEQ, q_row0=0, q_col=0, scale=NA_DH ** -0.5, pps=CTX_PPS)
    kc = cache_k.reshape(DEC_BATCH * PAST, D).astype(BF16)
    vc = cache_v.reshape(DEC_BATCH * PAST, D).astype(BF16)
    o_l = _na_latent(qkv_l, kc, vc, _na_bias_table(rpb))
    x = _mm(o_c, w_o.astype(BF16), rows=N_TOK, pro="cast", x2=o_l, epi="resid", res=x, mod=mod, layer=layer,
            jmod=1, tm=512, tn=2048)
    return x, k_c.reshape(BATCH, SEQ, NA_HEADS, NA_DH), v_c.reshape(BATCH, SEQ, NA_HEADS, NA_DH)


def _mixer_mla(x, mod, layer, g, w_down, g_q, w_uq, g_kv, w_ukv, w_o, cache_ckv, cache_kr):
    hq = MLA_NOPE + MLA_ROPE
    wd = jnp.concatenate([
        w_down[:, :MLA_Q_LORA],
        w_down[:, MLA_Q_LORA + MLA_KV_LORA:],
        jnp.zeros((D, 256 - MLA_ROPE), F32),
        w_down[:, MLA_Q_LORA:MLA_Q_LORA + MLA_KV_LORA]], axis=1).astype(BF16)
    wq = w_uq.reshape(MLA_Q_LORA, MLA_HEADS, hq)
    wq = jnp.concatenate([wq[:, :, :MLA_NOPE].reshape(MLA_Q_LORA, -1),
                          wq[:, :, MLA_NOPE:].reshape(MLA_Q_LORA, -1)], axis=1).astype(BF16)
    wkv = w_ukv.reshape(MLA_KV_LORA, MLA_HEADS, MLA_NOPE + MLA_V)
    wkv = jnp.concatenate([wkv[:, :, :MLA_NOPE].reshape(MLA_KV_LORA, -1),
                           wkv[:, :, MLA_NOPE:].reshape(MLA_KV_LORA, -1)], axis=1).astype(BF16)

    down = _mm(x, wd, rows=N_TOK, pro="ada", g=g, mod=mod, layer=layer, jmod=1, out_dtype=F32, tn=MLA_DOWN_N)
    q = _mm(down, wq, rows=N_TOK, pro="rms", g=g_q, out_dtype=F32, tn=1536)
    kv = _mm(down, wkv, rows=N_TOK, xcol=2, pro="rms", g=g_kv, out_dtype=BF16, tn=2048)
    kv_cache = _mm(cache_ckv.reshape(DEC_BATCH * PAST, MLA_KV_LORA).astype(BF16), wkv,
                   rows=DEC_BATCH * PAST, pro="cast", out_dtype=BF16, tn=2048)
    ckv_c = _rmsnorm(down, g_kv, rows=N_CTX, xcol=2)
    kr_c = down[:N_CTX, MLA_Q_LORA:MLA_Q_LORA + MLA_ROPE]

    npairs = MLA_HEADS // 2
    kr_col = MLA_Q_LORA // 128
    o_c = _attention(q, [dict(k=(kv, 0, 0), v=(kv, 0, npairs), kr=(down, 0, kr_col, 128), lk=SEQ)],
                     nb=BATCH, lq=SEQ, tq=SEQ, q_row0=0, q_col=0, qr=(q, 2 * npairs), scale=MLA_SCALE, pps=CTX_PPS)
    tabs = _rope_tables()
    kr_l = _rope_keys(down, tabs)
    o_l = _attention(
        q,
        [dict(k=(kv, N_CTX, 0), v=(kv, N_CTX, npairs), kr=(kr_l, 0, 0, 128), lk=DEC_SEQ),
         dict(k=(kv_cache, 0, 0), v=(kv_cache, 0, npairs),
              kr=(cache_kr.reshape(DEC_BATCH * PAST, MLA_ROPE), 0, 0, MLA_ROPE), lk=PAST)],
        nb=DEC_BATCH, lq=DEC_SEQ, tq=512, q_row0=N_CTX, q_col=0, qr=(q, 2 * npairs), scale=MLA_SCALE,
        rope_tabs=tabs, pps=2)
    x = _mm(o_c, w_o.astype(BF16), rows=N_TOK, pro="cast", x2=o_l, epi="resid", res=x, mod=mod, layer=layer,
            jmod=1, tm=512, tn=2048)
    return x, ckv_c.reshape(BATCH, SEQ, MLA_KV_LORA), kr_c.reshape(BATCH, SEQ, MLA_ROPE)


def _mixer_gla(x, mod, layer, g, w_in, w_gd, w_gu, b_g, g_norm, w_o, state):
    pad = jnp.zeros((D, GLA_PROJ_N - GLA_QKVG_N - 2 * GLA_RANK), F32)
    w_cat = jnp.concatenate([w_in, w_gd[0], w_gd[1], pad], axis=1).astype(BF16)
    proj = _mm(x, w_cat, rows=N_TOK, pro="ada", g=g, mod=mod, layer=layer, jmod=1, out_dtype=F32, tn=1280)
    wgu = jnp.zeros((2, 128, GLA_HEADS * GLA_DK), F32)
    wgu = wgu.at[0, :GLA_RANK].set(w_gu[0]).at[1, GLA_RANK:2 * GLA_RANK].set(w_gu[1]).astype(BF16)
    bg = b_g.reshape(2, 1, GLA_HEADS * GLA_DK)
    o_f, o_b, st_c = _gla_scan(proj, wgu, bg, jnp.swapaxes(state, -1, -2))
    x = _mm(o_f, w_o.astype(BF16), rows=N_TOK, pro="gla", x2=o_b, gin=proj, gin_col=2, g=g_norm,
            epi="resid", res=x, mod=mod, layer=layer, jmod=1, tm=256, tn=2048)
    return x, jnp.swapaxes(st_c, -1, -2)


def kernel(x_prompt, x_sample, cache_na_k, cache_na_v, cache_mla_ckv, cache_mla_krope, state_gla, c, c_ctx, norm_g, w_ada, b_ada, w_ffn_in, w_ffn_out, w_na_qkv, w_na_o, na_rpb, w_mla_down, g_mla_q, w_mla_uq, g_mla_kv, w_mla_ukv, w_mla_o, w_gla_in, w_gla_gate_down, w_gla_gate_up, b_gla_gate, g_gla_norm, w_gla_o, final_norm_g):
    x = jnp.concatenate([x_prompt.reshape(N_CTX, D), x_sample.reshape(N_LAT, D)], axis=0)
    cond = jnp.concatenate([c_ctx[None], c, jnp.zeros((N_GROUPS - 1 - DEC_BATCH, D), F32)], axis=0)
    mod = _ada_mod(cond, w_ada, b_ada)
    w_ffn = (w_ffn_in[0, 0].astype(BF16), w_ffn_out[0, 0].astype(BF16))

    def ffn(x, w_ffn, layer, half):
        last = layer == DEPTH - 1 and half == 1
        nxt = None if last else (w_ffn_in, w_ffn_out, layer + half, 1 - half)
        out = _ffn(x, mod, layer, half, w_ffn[0], w_ffn[1], norm_g[layer, 2 * half], nxt)
        return (out, None) if last else out

    na_k, na_v, ckv, krope, gla_st = [], [], [], [], []
    for i in range(DEPTH):
        kind, slot = i % 3, i // 3
        x, w_ffn = ffn(x, w_ffn, i, 0)
        if kind == 0:
            x, k_c, v_c = _mixer_na(x, mod, i, norm_g[i, 1], w_na_qkv[slot], w_na_o[slot], na_rpb[slot],
                                    cache_na_k[:, slot], cache_na_v[:, slot])
            na_k.append(k_c)
            na_v.append(v_c)
        elif kind == 1:
            x, ckv_c, kr_c = _mixer_mla(x, mod, i, norm_g[i, 1], w_mla_down[slot], g_mla_q[slot], w_mla_uq[slot],
                                        g_mla_kv[slot], w_mla_ukv[slot], w_mla_o[slot],
                                        cache_mla_ckv[:, slot], cache_mla_krope[:, slot])
            ckv.append(ckv_c)
            krope.append(kr_c)
        else:
            x, st = _mixer_gla(x, mod, i, norm_g[i, 1], w_gla_in[slot], w_gla_gate_down[slot],
                               w_gla_gate_up[slot], b_gla_gate[slot], g_gla_norm[slot], w_gla_o[slot],
                               state_gla[:, slot])
            gla_st.append(st)
        x, w_ffn = ffn(x, w_ffn, i, 1)

    y_prompt = _rmsnorm(x, final_norm_g, rows=N_CTX).reshape(BATCH, SEQ, D)
    y_sample = _rmsnorm(x, final_norm_g, rows=N_LAT, row_off=N_CTX // 512).reshape(DEC_BATCH, DEC_SEQ, D)
    return (y_prompt, y_sample, jnp.stack(na_k, axis=1), jnp.stack(na_v, axis=1), jnp.stack(ckv, axis=1),
            jnp.stack(krope, axis=1), jnp.stack(gla_st, axis=1))
```

```python
import functools

import jax
import jax.numpy as jnp
from jax import lax
from jax.experimental import pallas as pl
from jax.experimental.pallas import tpu as pltpu

F32 = jnp.float32
BF16 = jnp.bfloat16

D = 2048
BATCH, SEQ = 16, 256
DEC_BATCH, DEC_SEQ = 4, 2048
PAST = 512
DEPTH = 4
N_MOD = 9
EPS = 1e-6
D_FF = 5632
GRID_W = 64
LOG2_GRID_W = 6
NEG_INF = -1e30
LOG2E = 1.4426950408889634

N_CTX = BATCH * SEQ
N_LAT = DEC_BATCH * DEC_SEQ
N_TOK = N_CTX + N_LAT
N_GROUPS = 8

NA_HEADS, NA_DH = 16, 128
NA_WIN_ROWS, NA_WIN_COLS = 8, 16
LAT_ROWS = DEC_SEQ // GRID_W
NA_QR = 8
NA_KR = NA_QR + NA_WIN_ROWS
NA_HPS = 4

MLA_HEADS = 16
MLA_Q_LORA, MLA_KV_LORA = 768, 512
MLA_NOPE, MLA_ROPE, MLA_V = 128, 64, 128
MLA_SCALE = (MLA_NOPE + MLA_ROPE) ** -0.5
ROPE_THETA = 10000.0
MLA_DOWN_N = 1536

GLA_HEADS, GLA_DK, GLA_DV = 4, 256, 512
GLA_RANK = 16
GLA_TAU = 16.0
GLA_CHUNK = 128
GLA_QKVG_N = 2 * GLA_HEADS * GLA_DK + 2 * GLA_HEADS * GLA_DV
GLA_PROJ_N = GLA_QKVG_N + 256
GLA_CTX_CHUNKS = SEQ // GLA_CHUNK
GLA_LAT_CHUNKS = DEC_SEQ // GLA_CHUNK
GLA_CTX_STEPS = BATCH * GLA_CTX_CHUNKS
GLA_STEPS = GLA_CTX_STEPS + DEC_BATCH * GLA_LAT_CHUNKS

VMEM_LIMIT = 56 * 1024 * 1024
ROW_CHUNK = 16
CTX_PPS = 8
FFN_TF = 512
FFN_SUB = 512
FFN_CVT_IN_ROWS = 16
FFN_CVT_OUT_ROWS = 64


def _cparams(sem):
    return pltpu.CompilerParams(dimension_semantics=sem, vmem_limit_bytes=VMEM_LIMIT)


def _group_of_row(r0):
    return jnp.where(r0 < N_CTX, 0, 1 + (r0 - N_CTX) // DEC_SEQ)


def _silu(x):
    return x / (1.0 + jnp.exp(-x))


def _rms(x):
    return x * lax.rsqrt(jnp.mean(x * x, axis=-1, keepdims=True) + EPS)


def _for_row_chunks(n_rows, body):
    def step(c, carry):
        body(pl.ds(pl.multiple_of(c * ROW_CHUNK, ROW_CHUNK), ROW_CHUNK))
        return carry
    lax.fori_loop(0, n_rows // ROW_CHUNK, step, 0, unroll=8)


def _dot(a, b):
    return jnp.dot(a, b, preferred_element_type=F32)


def _dot_nt(a, b):
    return lax.dot_general(a, b, (((1,), (1,)), ((), ())), preferred_element_type=F32)


def _dot_tn(a, b):
    return lax.dot_general(a, b, (((0,), (0,)), ((), ())), preferred_element_type=F32)


def _ada_kernel(c_ref, w_ref, b_ref, o_ref):
    s = _silu(c_ref[...]).astype(BF16)
    o_ref[...] = _dot(s, w_ref[...].astype(BF16)) + b_ref[...]


def _ada_mod(cond, w_ada, b_ada):
    tn = 1024
    n = N_MOD * D
    out = pl.pallas_call(
        _ada_kernel,
        out_shape=jax.ShapeDtypeStruct((DEPTH, N_GROUPS, n), F32),
        grid=(DEPTH, n // tn),
        in_specs=[
            pl.BlockSpec((N_GROUPS, D), lambda l, j: (0, 0)),
            pl.BlockSpec((None, D, tn), lambda l, j: (l, 0, j)),
            pl.BlockSpec((None, 1, tn), lambda l, j: (l, 0, j)),
        ],
        out_specs=pl.BlockSpec((None, N_GROUPS, tn), lambda l, j: (l, 0, j)),
        compiler_params=_cparams(("parallel", "parallel")),
        name="ada_mod",
    )(cond, w_ada, b_ada.reshape(DEPTH, 1, n))
    return out.reshape(DEPTH, N_GROUPS, N_MOD, 1, D)


def _mod_spec(layer, j, tm, row_off):
    return pl.BlockSpec(
        (None, None, None, 1, D),
        lambda i, n: (layer, _group_of_row((i + row_off) * tm), j, 0, 0))


def _ffn_kernel(*refs, convert_next):
    if convert_next:
        (x_ref, g_ref, sh_ref, sc_ref, gt_ref, wg_ref, wu_ref, wo_ref, nin_ref, nout_ref,
         o_ref, cin_ref, cout_ref, h_ref, gm_ref) = refs
    else:
        x_ref, g_ref, sh_ref, sc_ref, gt_ref, wg_ref, wu_ref, wo_ref, o_ref, h_ref, gm_ref = refs
    f = pl.program_id(1)
    tm = x_ref.shape[0]

    @pl.when(f == 0)
    def _():
        gm_ref[...] = g_ref[...] * (1.0 + sc_ref[...])

        def rows_fn(rows):
            h_ref[rows, :] = (_rms(x_ref[rows, :]) * gm_ref[...] + sh_ref[...]).astype(BF16)
            o_ref[rows, :] = jnp.zeros((ROW_CHUNK, D), F32)

        _for_row_chunks(tm, rows_fn)

    for r in range(0, tm, FFN_SUB):
        h = h_ref[r:r + FFN_SUB, :]
        a = _silu(_dot(h, wg_ref[...])) * _dot(h, wu_ref[...])
        o_ref[r:r + FFN_SUB, :] += _dot(a.astype(BF16), wo_ref[...])

    @pl.when(f == pl.num_programs(1) - 1)
    def _():
        o_ref[...] = x_ref[...] + 0.5 * gt_ref[...] * o_ref[...]

    if convert_next:
        cin_ref[...] = nin_ref[...].astype(BF16)
        cout_ref[...] = nout_ref[...].astype(BF16)


def _ffn(x, mod, layer, half, w_in, w_out, g, nxt=None, *, tm=1024):
    tf = FFN_TF
    nf = D_FF // tf
    j = 2 * half
    in_specs = [
        pl.BlockSpec((tm, D), lambda i, f: (i, 0)),
        pl.BlockSpec((1, D), lambda i, f: (0, 0)),
        _mod_spec(layer, 3 * j, tm, 0),
        _mod_spec(layer, 3 * j + 1, tm, 0),
        _mod_spec(layer, 3 * j + 2, tm, 0),
        pl.BlockSpec((D, tf), lambda i, f: (0, f)),
        pl.BlockSpec((D, tf), lambda i, f: (0, nf + f)),
        pl.BlockSpec((tf, D), lambda i, f: (f, 0)),
    ]
    args = [x, g.reshape(1, D), mod, mod, mod, w_in, w_in, w_out]
    out_shape = [jax.ShapeDtypeStruct((N_TOK, D), F32)]
    out_specs = [pl.BlockSpec((tm, D), lambda i, f: (i, 0))]
    if nxt is not None:
        nw_in, nw_out, nl, nh = nxt
        steps = (N_TOK // tm) * nf
        n_in, n_out = D // FFN_CVT_IN_ROWS, D_FF // FFN_CVT_OUT_ROWS
        assert n_in <= steps and n_out <= steps

        def slab(last):
            return lambda i, f: jnp.minimum(i * nf + f, last)

        s_in, s_out = slab(n_in - 1), slab(n_out - 1)
        in_specs += [pl.BlockSpec((None, None, FFN_CVT_IN_ROWS, 2 * D_FF), lambda i, f: (nl, nh, s_in(i, f), 0)),
                     pl.BlockSpec((None, None, FFN_CVT_OUT_ROWS, D), lambda i, f: (nl, nh, s_out(i, f), 0))]
        args += [nw_in, nw_out]
        out_shape += [jax.ShapeDtypeStruct((D, 2 * D_FF), BF16), jax.ShapeDtypeStruct((D_FF, D), BF16)]
        out_specs += [pl.BlockSpec((FFN_CVT_IN_ROWS, 2 * D_FF), lambda i, f: (s_in(i, f), 0)),
                      pl.BlockSpec((FFN_CVT_OUT_ROWS, D), lambda i, f: (s_out(i, f), 0))]
    outs = pl.pallas_call(
        functools.partial(_ffn_kernel, convert_next=nxt is not None),
        out_shape=out_shape,
        grid=(N_TOK // tm, nf),
        in_specs=in_specs,
        out_specs=out_specs,
        scratch_shapes=[pltpu.VMEM((tm, D), BF16), pltpu.VMEM((1, D), F32)],
        compiler_params=_cparams(("arbitrary", "arbitrary")),
        name="ffn_half",
    )(*args)
    return outs[0] if nxt is None else (outs[0], (outs[1], outs[2]))


def _mm_kernel(*refs, pro, epi, split, n_outs, tiles_per_out):
    it = iter(refs)
    x_ref = next(it)
    if pro == "cast" and split is not None:
        x2_ref = next(it)
    if pro == "gla":
        x2_ref, gin_ref = next(it), next(it)
    if pro in ("ada", "rms", "gla"):
        g_ref = next(it)
    if pro == "ada":
        sh_ref, sc_ref = next(it), next(it)
    w_ref = next(it)
    if epi == "resid":
        res_ref, gt_ref = next(it), next(it)
    o_refs = [next(it) for _ in range(n_outs)]
    if pro != "cast":
        h_ref = next(it)
    if pro == "ada":
        gm_ref = next(it)

    if pro != "cast":
        @pl.when(pl.program_id(1) == 0)
        def _():
            if pro == "ada":
                gm_ref[...] = g_ref[...] * (1.0 + sc_ref[...])

            def rows_fn(rows):
                if pro == "ada":
                    h = _rms(x_ref[rows, :]) * gm_ref[...] + sh_ref[...]
                elif pro == "rms":
                    h = _rms(x_ref[rows, :]) * g_ref[...]
                else:
                    o = x_ref[rows, :] + x2_ref[rows, :]
                    parts = [_rms(o[:, k * GLA_DV:(k + 1) * GLA_DV]) * g_ref[...] for k in range(GLA_HEADS)]
                    h = jnp.concatenate(parts, axis=-1) * _silu(gin_ref[rows, :])
                h_ref[rows, :] = h.astype(BF16)

            _for_row_chunks(x_ref.shape[0], rows_fn)

    def finish(lhs_ref):
        y = _dot(lhs_ref[...], w_ref[...])
        if epi == "resid":
            y = res_ref[...] + gt_ref[...] * y
        if n_outs == 1:
            o_refs[0][...] = y.astype(o_refs[0].dtype)
        else:
            for k, o_ref in enumerate(o_refs):
                @pl.when(pl.program_id(1) // tiles_per_out == k)
                def _(o_ref=o_ref):
                    o_ref[...] = y.astype(o_ref.dtype)

    if pro != "cast":
        finish(h_ref)
    elif split is None:
        finish(x_ref)
    else:
        pl.when(pl.program_id(0) < split)(lambda: finish(x_ref))
        pl.when(pl.program_id(0) >= split)(lambda: finish(x2_ref))


def _mm(x, w, *, rows, row_off=0, xcol=0, n_outs=1, pro, epi="plain", out_dtype=F32, tm=1024, tn=1024,
        g=None, mod=None, layer=None, jmod=None, x2=None, gin=None, gin_col=0, res=None):
    kdim, n = w.shape
    tn = min(tn, n // n_outs)
    tiles_per_out = n // n_outs // tn
    assert rows % tm == 0 and n % (n_outs * tn) == 0 and (n_outs == 1 or epi == "plain")
    split = None
    if pro == "cast" and x2 is not None:
        split = x.shape[0] // tm
        xspec = pl.BlockSpec((tm, kdim), lambda i, j: (jnp.minimum(i, split - 1), 0))
        args, specs = [x, x2], [xspec, pl.BlockSpec((tm, kdim), lambda i, j: (jnp.maximum(i - split, 0), 0))]
    else:
        xspec = pl.BlockSpec((tm, kdim), lambda i, j: (i + row_off, xcol))
        args, specs = [x], [xspec]
    if pro == "gla":
        args += [x2, gin]
        specs += [xspec, pl.BlockSpec((tm, kdim), lambda i, j: (i + row_off, gin_col))]
    if pro in ("ada", "rms", "gla"):
        args.append(g.reshape(1, -1))
        specs.append(pl.BlockSpec((1, g.shape[-1]), lambda i, j: (0, 0)))
    if pro == "ada":
        args += [mod, mod]
        specs += [_mod_spec(layer, 3 * jmod, tm, row_off), _mod_spec(layer, 3 * jmod + 1, tm, row_off)]
    args.append(w)
    specs.append(pl.BlockSpec((kdim, tn), lambda i, j: (0, j)))
    if epi == "resid":
        args += [res, mod]
        specs += [
            pl.BlockSpec((tm, tn), lambda i, j: (i + row_off, j)),
            pl.BlockSpec((None, None, None, 1, tn),
                         lambda i, j: (layer, _group_of_row((i + row_off) * tm), 3 * jmod + 2, 0, j)),
        ]
    scratch = []
    if pro != "cast":
        scratch.append(pltpu.VMEM((tm, kdim), BF16))
    if pro == "ada":
        scratch.append(pltpu.VMEM((1, kdim), F32))
    out_specs = [pl.BlockSpec((tm, tn), functools.partial(
        lambda i, j, k: (i, jnp.clip(j - k * tiles_per_out, 0, tiles_per_out - 1)), k=k)) for k in range(n_outs)]
    outs = pl.pallas_call(
        functools.partial(_mm_kernel, pro=pro, epi=epi, split=split, n_outs=n_outs, tiles_per_out=tiles_per_out),
        out_shape=[jax.ShapeDtypeStruct((rows, n // n_outs), out_dtype)] * n_outs,
        grid=(rows // tm, n // tn),
        in_specs=specs,
        out_specs=out_specs,
        scratch_shapes=scratch,
        compiler_params=_cparams(("parallel", "arbitrary")),
        name="proj_" + pro + "_" + epi,
    )(*args)
    return outs[0] if n_outs == 1 else outs


def _rmsnorm_kernel(x_ref, g_ref, o_ref):
    o_ref[...] = _rms(x_ref[...]) * g_ref[...]


def _rmsnorm(x, g, *, rows, row_off=0, xcol=0, tm=512):
    width = g.shape[-1]
    return pl.pallas_call(
        _rmsnorm_kernel,
        out_shape=jax.ShapeDtypeStruct((rows, width), F32),
        grid=(rows // tm,),
        in_specs=[pl.BlockSpec((tm, width), lambda i: (i + row_off, xcol)),
                  pl.BlockSpec((1, width), lambda i: (0, 0))],
        out_specs=pl.BlockSpec((tm, width), lambda i: (i, 0)),
        compiler_params=_cparams(("parallel",)),
        name="rmsnorm",
    )(x, g.reshape(1, width))


def _rope(x, cos, sin):
    width = x.shape[-1]
    lane = lax.broadcasted_iota(jnp.int32, x.shape, 1)
    up = pltpu.roll(x, width - 16, 1)
    down = pltpu.roll(x, 16, 1)
    swapped = jnp.where((lane & 31) < 16, up, down)
    return x * cos + swapped * sin


def _rope_keys_kernel(x_ref, cos_ref, sin_ref, o_ref):
    o_ref[...] = _rope(x_ref[...], cos_ref[...], sin_ref[...]).astype(o_ref.dtype)


def _rope_keys(down, rope_tabs, *, tm=1024):
    cos, sin = rope_tabs
    per_seq = DEC_SEQ // tm
    return pl.pallas_call(
        _rope_keys_kernel,
        out_shape=jax.ShapeDtypeStruct((N_LAT, 128), BF16),
        grid=(N_LAT // tm,),
        in_specs=[pl.BlockSpec((tm, 128), lambda i: (N_CTX // tm + i, MLA_Q_LORA // 128)),
                  pl.BlockSpec((tm, 128), lambda i: (i % per_seq, 0)),
                  pl.BlockSpec((tm, 128), lambda i: (i % per_seq, 0))],
        out_specs=pl.BlockSpec((tm, 128), lambda i: (i, 0)),
        compiler_params=_cparams(("parallel",)),
        name="rope_keys",
    )(down, cos, sin)


def _attn_kernel(*refs, nseg, has_r, rope, scale, dh, dv, pps):
    it = iter(refs)
    q_ref = next(it)
    qr_ref = next(it) if has_r else None
    segs = []
    for _ in range(nseg):
        k_ref = next(it)
        kr_ref = next(it) if has_r else None
        v_ref = next(it)
        segs.append((k_ref, kr_ref, v_ref))
    if rope:
        cq_ref, sq_ref = next(it), next(it)
    o_ref = next(it)

    if has_r:
        qr = qr_ref[...].astype(F32)
        if rope:
            qr = _rope(qr, cq_ref[...], sq_ref[...])
        qrs = [qr[:, h * MLA_ROPE:(h + 1) * MLA_ROPE].astype(BF16) for h in range(2 * pps)]
        krs = [kr_ref[:, :MLA_ROPE].astype(BF16) for _, kr_ref, _ in segs]

    nh = 2 * pps

    def qk(h):
        q = q_ref[:, h * dh:(h + 1) * dh].astype(BF16)
        ss = []
        for si, (k_ref, _, _) in enumerate(segs):
            s = _dot_nt(q, k_ref[:, h * dh:(h + 1) * dh].astype(BF16))
            if has_r:
                s = s + _dot_nt(qrs[h], krs[si])
            ss.append(s * (scale * LOG2E))
        return ss

    def softmax(ss):
        m = ss[0].max(axis=-1, keepdims=True)
        for s in ss[1:]:
            m = jnp.maximum(m, s.max(axis=-1, keepdims=True))
        es = [jnp.exp2(s - m) for s in ss]
        den = es[0].sum(axis=-1, keepdims=True)
        for e in es[1:]:
            den = den + e.sum(axis=-1, keepdims=True)
        return [e.astype(BF16) for e in es], den

    def pv(h, es, den):
        acc = None
        for e, (_, _, v_ref) in zip(es, segs):
            part = _dot(e, v_ref[:, h * dv:(h + 1) * dv].astype(BF16))
            acc = part if acc is None else acc + part
        o_ref[:, h * dv:(h + 1) * dv] = (acc / den).astype(o_ref.dtype)

    scores, probs = {}, {}
    for t in range(nh + 2):
        if t < nh:
            scores[t] = qk(t)
        if 0 <= t - 1 < nh:
            probs[t - 1] = softmax(scores.pop(t - 1))
        if 0 <= t - 2 < nh:
            pv(t - 2, *probs.pop(t - 2))


def _attention(q, segs, *, nb, lq, tq, q_row0, q_col, scale, qr=None, rope_tabs=None, pps=1, dh=128, dv=128):
    npairs = 8
    ng = npairs // pps
    nq = lq // tq
    has_r = qr is not None
    q_blk0 = q_row0 // tq
    assert q_col % pps == 0
    args = [q]
    specs = [pl.BlockSpec((tq, 2 * dh * pps), lambda b, p, t: (q_blk0 + b * nq + t, q_col // pps + p))]
    if has_r:
        qr_arr, qr_col = qr
        assert qr_col % pps == 0
        args.append(qr_arr)
        specs.append(pl.BlockSpec((tq, 2 * MLA_ROPE * pps),
                                  lambda b, p, t: (q_blk0 + b * nq + t, qr_col // pps + p)))
    for sg in segs:
        lk = sg["lk"]
        k_arr, k_row0, k_col = sg["k"]
        assert k_col % pps == 0
        args.append(k_arr)
        specs.append(pl.BlockSpec((lk, 2 * dh * pps), functools.partial(
            lambda b, p, t, r0, c0: (r0 + b, c0 + p), r0=k_row0 // lk, c0=k_col // pps)))
        if has_r:
            kr_arr, kr_row0, kr_col, kr_w = sg["kr"]
            args.append(kr_arr)
            specs.append(pl.BlockSpec((lk, kr_w), functools.partial(
                lambda b, p, t, r0, c0: (r0 + b, c0), r0=kr_row0 // lk, c0=kr_col)))
        v_arr, v_row0, v_col = sg["v"]
        assert v_col % pps == 0
        args.append(v_arr)
        specs.append(pl.BlockSpec((lk, 2 * dv * pps), functools.partial(
            lambda b, p, t, r0, c0: (r0 + b, c0 + p), r0=v_row0 // lk, c0=v_col // pps)))
    if rope_tabs is not None:
        cos, sin = (jnp.concatenate([tab] * pps, axis=-1) for tab in rope_tabs)
        args += [cos, sin]
        specs += [pl.BlockSpec((tq, 128 * pps), lambda b, p, t: (t, 0)),
                  pl.BlockSpec((tq, 128 * pps), lambda b, p, t: (t, 0))]
    return pl.pallas_call(
        functools.partial(_attn_kernel, nseg=len(segs), has_r=has_r, rope=rope_tabs is not None,
                          scale=scale, dh=dh, dv=dv, pps=pps),
        out_shape=jax.ShapeDtypeStruct((nb * lq, npairs * 2 * dv), BF16),
        grid=(nb, ng, nq),
        in_specs=specs,
        out_specs=pl.BlockSpec((tq, 2 * dv * pps), lambda b, p, t: (b * nq + t, p)),
        compiler_params=_cparams(("parallel", "parallel", "arbitrary")),
        name="attention",
    )(*args)


def _na_kernel(q_ref, k_ref, v_ref, kc_ref, vc_ref, t2_ref, o_ref, bias_ref, cap_ref):
    r0 = pl.program_id(1) * NA_QR
    ks = jnp.clip(r0 - NA_WIN_ROWS // 2, 0, LAT_ROWS - NA_KR)
    nq, nk = NA_QR * GRID_W, NA_KR * GRID_W
    scale = NA_DH ** -0.5 * LOG2E

    @pl.when(pl.program_id(2) == 0)
    def _():
        row = lax.broadcasted_iota(jnp.int32, (nq, nk), 0)
        lane = lax.broadcasted_iota(jnp.int32, (nq, nk), 1)
        qc = row & (GRID_W - 1)
        kc = lane & (GRID_W - 1)
        rs = jnp.clip(r0 + (row >> LOG2_GRID_W) - NA_WIN_ROWS // 2, 0, LAT_ROWS - NA_WIN_ROWS)
        kr = ks + (lane >> LOG2_GRID_W)
        cs = jnp.clip(qc - NA_WIN_COLS // 2, 0, GRID_W - NA_WIN_COLS)
        ok = (kr >= rs) & (kr < rs + NA_WIN_ROWS) & (kc >= cs) & (kc < cs + NA_WIN_COLS)
        cap_ref[...] = jnp.where(ok, jnp.inf, NEG_INF)
        for hh in range(NA_HPS):
            for i in range(NA_QR):
                for jp in range(NA_KR // 2):
                    e = jnp.clip(ks - r0 + 2 * jp - i + NA_WIN_ROWS, 0, 2 * NA_WIN_ROWS - 1)
                    bias_ref[hh, i * GRID_W:(i + 1) * GRID_W, jp * 128:(jp + 1) * 128] = t2_ref[hh, e] * LOG2E

    start = pl.multiple_of(ks * GRID_W, GRID_W)
    kw = k_ref[pl.ds(start, nk), :]
    vw = v_ref[pl.ds(start, nk), :]
    cap = cap_ref[...]
    def head_cols(hh):
        return slice(hh * NA_DH, (hh + 1) * NA_DH)

    def qk(hh):
        q = q_ref[:, head_cols(hh)]
        return (jnp.minimum(_dot_nt(q, kw[:, head_cols(hh)]) * scale + bias_ref[hh], cap),
                _dot_nt(q, kc_ref[:, head_cols(hh)]) * scale)

    def softmax(s_loc, s_ctx):
        m = jnp.maximum(s_loc.max(axis=-1, keepdims=True), s_ctx.max(axis=-1, keepdims=True))
        e_loc = jnp.exp2(s_loc - m)
        e_ctx = jnp.exp2(s_ctx - m)
        den = e_loc.sum(axis=-1, keepdims=True) + e_ctx.sum(axis=-1, keepdims=True)
        return e_loc.astype(BF16), e_ctx.astype(BF16), den

    def pv(hh, e_loc, e_ctx, den):
        acc = _dot(e_loc, vw[:, head_cols(hh)]) + _dot(e_ctx, vc_ref[:, head_cols(hh)])
        o_ref[:, head_cols(hh)] = (acc / den).astype(o_ref.dtype)

    scores, probs = {}, {}
    for t in range(NA_HPS + 2):
        if t < NA_HPS:
            scores[t] = qk(t)
        if 0 <= t - 1 < NA_HPS:
            probs[t - 1] = softmax(*scores.pop(t - 1))
        if 0 <= t - 2 < NA_HPS:
            pv(t - 2, *probs.pop(t - 2))


def _na_bias_table(rpb):
    col = jnp.arange(GRID_W)
    col_off = jnp.clip(col[None, :] - col[:, None] + NA_WIN_COLS - 1, 0, 2 * NA_WIN_COLS - 2)
    e = jnp.arange(2 * NA_WIN_ROWS)
    dr = jnp.clip(jnp.stack([e - 1, e], axis=1), 0, 2 * NA_WIN_ROWS - 2)
    tb = rpb[:, dr][:, :, :, col_off]
    return jnp.transpose(tb, (0, 1, 3, 2, 4)).reshape(NA_HEADS, 2 * NA_WIN_ROWS, GRID_W, 2 * GRID_W)


def _na_latent(qkv_l, kc, vc, t2):
    ngroups = NA_HEADS // NA_HPS
    wg = NA_HPS * NA_DH
    nrb = LAT_ROWS // NA_QR
    nq = NA_QR * GRID_W
    return pl.pallas_call(
        _na_kernel,
        out_shape=jax.ShapeDtypeStruct((N_LAT, D), BF16),
        grid=(ngroups, nrb, DEC_BATCH),
        in_specs=[
            pl.BlockSpec((nq, wg), lambda g, r, b: (b * nrb + r, g)),
            pl.BlockSpec((DEC_SEQ, wg), lambda g, r, b: (b, ngroups + g)),
            pl.BlockSpec((DEC_SEQ, wg), lambda g, r, b: (b, 2 * ngroups + g)),
            pl.BlockSpec((PAST, wg), lambda g, r, b: (b, g)),
            pl.BlockSpec((PAST, wg), lambda g, r, b: (b, g)),
            pl.BlockSpec((NA_HPS, 2 * NA_WIN_ROWS, GRID_W, 2 * GRID_W), lambda g, r, b: (g, 0, 0, 0)),
        ],
        out_specs=pl.BlockSpec((nq, wg), lambda g, r, b: (b * nrb + r, g)),
        scratch_shapes=[pltpu.VMEM((NA_HPS, nq, NA_KR * GRID_W), F32), pltpu.VMEM((nq, NA_KR * GRID_W), F32)],
        compiler_params=_cparams(("parallel", "parallel", "arbitrary")),
        name="na_latent",
    )(qkv_l, qkv_l, qkv_l, kc, vc, t2)


def _gla_pos(t):
    is_ctx = t < GLA_CTX_STEPS
    u = jnp.maximum(t - GLA_CTX_STEPS, 0)
    seq = jnp.where(is_ctx, t // GLA_CTX_CHUNKS, u // GLA_LAT_CHUNKS)
    n = jnp.where(is_ctx, t % GLA_CTX_CHUNKS, u % GLA_LAT_CHUNKS)
    return is_ctx, seq, n


def _gla_bwd_chunk(t):
    is_ctx, _, n = _gla_pos(t)
    return t + jnp.where(is_ctx, GLA_CTX_CHUNKS, GLA_LAT_CHUNKS) - 1 - 2 * n


def _gla_kernel(qf_ref, kf_ref, vf_ref, gdf_ref, qb_ref, kb_ref, vb_ref, gdb_ref, wgu_ref, bg_ref, s0_ref,
                of_ref, ob_ref, fin_ref, st_ref):
    is_ctx, _, n = _gla_pos(pl.program_id(0))
    is_lat = jnp.logical_not(is_ctx)

    @pl.when(jnp.logical_and(n == 0, is_ctx))
    def _():
        st_ref[...] = jnp.zeros_like(st_ref)

    @pl.when(jnp.logical_and(n == 0, is_lat))
    def _():
        st_ref[...] = s0_ref[...]

    ri = lax.broadcasted_iota(jnp.int32, (GLA_CHUNK, GLA_CHUNK), 0)
    ci = lax.broadcasted_iota(jnp.int32, (GLA_CHUNK, GLA_CHUNK), 1)
    streams = ((qf_ref, kf_ref, vf_ref, gdf_ref, of_ref), (qb_ref, kb_ref, vb_ref, gdb_ref, ob_ref))
    for d, (q_ref, k_ref, v_ref, gd_ref, o_ref) in enumerate(streams):
        keep = (ci <= ri) if d == 0 else (ci >= ri)
        tri = jnp.where(keep, 1.0, 0.0).astype(BF16)
        pre = _dot(gd_ref[...].astype(BF16), wgu_ref[d]) + bg_ref[d]
        la = (jnp.minimum(pre, 0.0) - jnp.log1p(jnp.exp(-jnp.abs(pre)))) / GLA_TAU
        hi = la.astype(BF16)
        r1 = la - hi.astype(F32)
        mid = r1.astype(BF16)
        lo = (r1 - mid.astype(F32)).astype(BF16)
        cum = _dot(tri, hi) + _dot(tri, mid) + _dot(tri, lo)
        tot = jnp.sum(la, axis=0, keepdims=True)
        k = k_ref[...]
        q_dec = (q_ref[...] * (GLA_DK ** -0.5) * jnp.exp(cum)).astype(BF16)
        k_inv = (k * jnp.exp(-cum)).astype(BF16)
        k_end = (k * jnp.exp(tot - cum)).astype(BF16)
        dec = jnp.exp(tot)
        v = v_ref[...].astype(BF16)
        for h in range(GLA_HEADS):
            ks = slice(h * GLA_DK, (h + 1) * GLA_DK)
            vs = slice(h * GLA_DV, (h + 1) * GLA_DV)
            att = jnp.where(keep, _dot_nt(q_dec[:, ks], k_inv[:, ks]), 0.0)
            st = st_ref[d, h]
            o_ref[:, vs] = _dot(att.astype(BF16), v[:, vs]) + _dot_nt(q_dec[:, ks], st.astype(BF16))
            st_ref[d, h] = st * dec[:, ks] + _dot_tn(v[:, vs], k_end[:, ks])

    @pl.when(jnp.logical_and(is_ctx, n == GLA_CTX_CHUNKS - 1))
    def _():
        fin_ref[...] = st_ref[...]


def _gla_scan(proj, wgu, bg, s0t):
    hk = GLA_HEADS * GLA_DK
    hv = GLA_HEADS * GLA_DV
    gd_col = GLA_QKVG_N // 128

    def fwd(c):
        return lambda t: (t, c)

    def bwd(c):
        return lambda t: (_gla_bwd_chunk(t), c)

    in_specs = []
    for ix in (fwd, bwd):
        in_specs += [pl.BlockSpec((GLA_CHUNK, hk), ix(0)), pl.BlockSpec((GLA_CHUNK, hk), ix(1)),
                     pl.BlockSpec((GLA_CHUNK, hv), ix(1)), pl.BlockSpec((GLA_CHUNK, 128), ix(gd_col))]
    st_block = (None, 2, GLA_HEADS, GLA_DV, GLA_DK)
    in_specs += [
        pl.BlockSpec((2, 128, hk), lambda t: (0, 0, 0)),
        pl.BlockSpec((2, 1, hk), lambda t: (0, 0, 0)),
        pl.BlockSpec(st_block, lambda t: (jnp.where(_gla_pos(t)[0], 0, _gla_pos(t)[1]), 0, 0, 0, 0)),
    ]
    o_shape = jax.ShapeDtypeStruct((N_TOK, hv), F32)
    return pl.pallas_call(
        _gla_kernel,
        out_shape=(o_shape, o_shape, jax.ShapeDtypeStruct((BATCH, 2, GLA_HEADS, GLA_DV, GLA_DK), F32)),
        grid=(GLA_STEPS,),
        in_specs=in_specs,
        out_specs=(
            pl.BlockSpec((GLA_CHUNK, hv), lambda t: (t, 0)),
            pl.BlockSpec((GLA_CHUNK, hv), lambda t: (_gla_bwd_chunk(t), 0)),
            pl.BlockSpec(st_block, lambda t: (jnp.where(_gla_pos(t)[0], _gla_pos(t)[1], BATCH - 1), 0, 0, 0, 0)),
        ),
        scratch_shapes=[pltpu.VMEM((2, GLA_HEADS, GLA_DV, GLA_DK), F32)],
        compiler_params=_cparams(("arbitrary",)),
        name="gla_scan",
    )(*([proj] * 8), wgu, bg, s0t)


def _rope_tables():
    t = jnp.arange(DEC_SEQ)
    d = MLA_ROPE // 2
    inv = ROPE_THETA ** (-jnp.arange(0, d, 2, dtype=F32) / d)
    ang_r = (t // GRID_W).astype(F32)[:, None] * inv[None]
    ang_c = (t % GRID_W).astype(F32)[:, None] * inv[None]
    cos = jnp.concatenate([jnp.cos(ang_r)] * 2 + [jnp.cos(ang_c)] * 2, axis=-1)
    sin = jnp.concatenate([-jnp.sin(ang_r), jnp.sin(ang_r), -jnp.sin(ang_c), jnp.sin(ang_c)], axis=-1)
    return jnp.concatenate([cos, cos], axis=-1), jnp.concatenate([sin, sin], axis=-1)


def _mixer_na(x, mod, layer, g, w_qkv, w_o, rpb, cache_k, cache_v):
    w_qkv = w_qkv.astype(BF16)
    ada = dict(pro="ada", g=g, mod=mod, layer=layer, jmod=1)
    q_c, k_c, v_c = _mm(x, w_qkv, rows=N_CTX, n_outs=3, out_dtype=F32, tm=512, **ada)
    qkv_l = _mm(x, w_qkv, rows=N_LAT, row_off=N_CTX // 1024, out_dtype=BF16, tn=1536, **ada)
    o_c = _attention(q_c, [dict(k=(k_c, 0, 0), v=(v_c, 0, 0), lk=SEQ)],
                     nb=BATCH, lq=SEQ, tq=SEQ, q_row0=0, q_col=0, scale=NA_DH ** -0.5, pps=CTX_PPS)
    kc = cache_k.reshape(DEC_BATCH * PAST, D).astype(BF16)
    vc = cache_v.reshape(DEC_BATCH * PAST, D).astype(BF16)
    o_l = _na_latent(qkv_l, kc, vc, _na_bias_table(rpb))
    x = _mm(o_c, w_o.astype(BF16), rows=N_TOK, pro="cast", x2=o_l, epi="resid", res=x, mod=mod, layer=layer,
            jmod=1, tm=512, tn=2048)
    return x, k_c.reshape(BATCH, SEQ, NA_HEADS, NA_DH), v_c.reshape(BATCH, SEQ, NA_HEADS, NA_DH)


def _mixer_mla(x, mod, layer, g, w_down, g_q, w_uq, g_kv, w_ukv, w_o, cache_ckv, cache_kr):
    hq = MLA_NOPE + MLA_ROPE
    wd = jnp.concatenate([
        w_down[:, :MLA_Q_LORA],
        w_down[:, MLA_Q_LORA + MLA_KV_LORA:],
        jnp.zeros((D, 256 - MLA_ROPE), F32),
        w_down[:, MLA_Q_LORA:MLA_Q_LORA + MLA_KV_LORA]], axis=1).astype(BF16)
    wq = w_uq.reshape(MLA_Q_LORA, MLA_HEADS, hq)
    wq = jnp.concatenate([wq[:, :, :MLA_NOPE].reshape(MLA_Q_LORA, -1),
                          wq[:, :, MLA_NOPE:].reshape(MLA_Q_LORA, -1)], axis=1).astype(BF16)
    wkv = w_ukv.reshape(MLA_KV_LORA, MLA_HEADS, MLA_NOPE + MLA_V)
    wkv = jnp.concatenate([wkv[:, :, :MLA_NOPE].reshape(MLA_KV_LORA, -1),
                           wkv[:, :, MLA_NOPE:].reshape(MLA_KV_LORA, -1)], axis=1).astype(BF16)

    down = _mm(x, wd, rows=N_TOK, pro="ada", g=g, mod=mod, layer=layer, jmod=1, out_dtype=F32, tn=MLA_DOWN_N)
    q = _mm(down, wq, rows=N_TOK, pro="rms", g=g_q, out_dtype=F32, tn=1536)
    kv = _mm(down, wkv, rows=N_TOK, xcol=2, pro="rms", g=g_kv, out_dtype=BF16, tn=2048)
    kv_cache = _mm(cache_ckv.reshape(DEC_BATCH * PAST, MLA_KV_LORA).astype(BF16), wkv,
                   rows=DEC_BATCH * PAST, pro="cast", out_dtype=BF16, tn=2048)
    ckv_c = _rmsnorm(down, g_kv, rows=N_CTX, xcol=2)
    kr_c = down[:N_CTX, MLA_Q_LORA:MLA_Q_LORA + MLA_ROPE]

    npairs = MLA_HEADS // 2
    kr_col = MLA_Q_LORA // 128
    o_c = _attention(q, [dict(k=(kv, 0, 0), v=(kv, 0, npairs), kr=(down, 0, kr_col, 128), lk=SEQ)],
                     nb=BATCH, lq=SEQ, tq=SEQ, q_row0=0, q_col=0, qr=(q, 2 * npairs), scale=MLA_SCALE, pps=CTX_PPS)
    tabs = _rope_tables()
    kr_l = _rope_keys(down, tabs)
    o_l = _attention(
        q,
        [dict(k=(kv, N_CTX, 0), v=(kv, N_CTX, npairs), kr=(kr_l, 0, 0, 128), lk=DEC_SEQ),
         dict(k=(kv_cache, 0, 0), v=(kv_cache, 0, npairs),
              kr=(cache_kr.reshape(DEC_BATCH * PAST, MLA_ROPE), 0, 0, MLA_ROPE), lk=PAST)],
        nb=DEC_BATCH, lq=DEC_SEQ, tq=512, q_row0=N_CTX, q_col=0, qr=(q, 2 * npairs), scale=MLA_SCALE,
        rope_tabs=tabs, pps=2)
    x = _mm(o_c, w_o.astype(BF16), rows=N_TOK, pro="cast", x2=o_l, epi="resid", res=x, mod=mod, layer=layer,
            jmod=1, tm=512, tn=2048)
    return x, ckv_c.reshape(BATCH, SEQ, MLA_KV_LORA), kr_c.reshape(BATCH, SEQ, MLA_ROPE)


def _mixer_gla(x, mod, layer, g, w_in, w_gd, w_gu, b_g, g_norm, w_o, state):
    pad = jnp.zeros((D, GLA_PROJ_N - GLA_QKVG_N - 2 * GLA_RANK), F32)
    w_cat = jnp.concatenate([w_in, w_gd[0], w_gd[1], pad], axis=1).astype(BF16)
    proj = _mm(x, w_cat, rows=N_TOK, pro="ada", g=g, mod=mod, layer=layer, jmod=1, out_dtype=F32, tn=1280)
    wgu = jnp.zeros((2, 128, GLA_HEADS * GLA_DK), F32)
    wgu = wgu.at[0, :GLA_RANK].set(w_gu[0]).at[1, GLA_RANK:2 * GLA_RANK].set(w_gu[1]).astype(BF16)
    bg = b_g.reshape(2, 1, GLA_HEADS * GLA_DK)
    o_f, o_b, st_c = _gla_scan(proj, wgu, bg, jnp.swapaxes(state, -1, -2))
    x = _mm(o_f, w_o.astype(BF16), rows=N_TOK, pro="gla", x2=o_b, gin=proj, gin_col=2, g=g_norm,
            epi="resid", res=x, mod=mod, layer=layer, jmod=1, tm=256, tn=2048)
    return x, jnp.swapaxes(st_c, -1, -2)


def kernel(x_prompt, x_sample, cache_na_k, cache_na_v, cache_mla_ckv, cache_mla_krope, state_gla, c, c_ctx, norm_g, w_ada, b_ada, w_ffn_in, w_ffn_out, w_na_qkv, w_na_o, na_rpb, w_mla_down, g_mla_q, w_mla_uq, g_mla_kv, w_mla_ukv, w_mla_o, w_gla_in, w_gla_gate_down, w_gla_gate_up, b_gla_gate, g_gla_norm, w_gla_o, final_norm_g):
    x = jnp.concatenate([x_prompt.reshape(N_CTX, D), x_sample.reshape(N_LAT, D)], axis=0)
    cond = jnp.concatenate([c_ctx[None], c, jnp.zeros((N_GROUPS - 1 - DEC_BATCH, D), F32)], axis=0)
    mod = _ada_mod(cond, w_ada, b_ada)
    w_ffn = (w_ffn_in[0, 0].astype(BF16), w_ffn_out[0, 0].astype(BF16))

    def ffn(x, w_ffn, layer, half):
        last = layer == DEPTH - 1 and half == 1
        nxt = None if last else (w_ffn_in, w_ffn_out, layer + half, 1 - half)
        out = _ffn(x, mod, layer, half, w_ffn[0], w_ffn[1], norm_g[layer, 2 * half], nxt)
        return (out, None) if last else out

    na_k, na_v, ckv, krope, gla_st = [], [], [], [], []
    for i in range(DEPTH):
        kind, slot = i % 3, i // 3
        x, w_ffn = ffn(x, w_ffn, i, 0)
        if kind == 0:
            x, k_c, v_c = _mixer_na(x, mod, i, norm_g[i, 1], w_na_qkv[slot], w_na_o[slot], na_rpb[slot],
                                    cache_na_k[:, slot], cache_na_v[:, slot])
            na_k.append(k_c)
            na_v.append(v_c)
        elif kind == 1:
            x, ckv_c, kr_c = _mixer_mla(x, mod, i, norm_g[i, 1], w_mla_down[slot], g_mla_q[slot], w_mla_uq[slot],
                                        g_mla_kv[slot], w_mla_ukv[slot], w_mla_o[slot],
                                        cache_mla_ckv[:, slot], cache_mla_krope[:, slot])
            ckv.append(ckv_c)
            krope.append(kr_c)
        else:
            x, st = _mixer_gla(x, mod, i, norm_g[i, 1], w_gla_in[slot], w_gla_gate_down[slot],
                               w_gla_gate_up[slot], b_gla_gate[slot], g_gla_norm[slot], w_gla_o[slot],
                               state_gla[:, slot])
            gla_st.append(st)
        x, w_ffn = ffn(x, w_ffn, i, 1)

    y_prompt = _rmsnorm(x, final_norm_g, rows=N_CTX).reshape(BATCH, SEQ, D)
    y_sample = _rmsnorm(x, final_norm_g, rows=N_LAT, row_off=N_CTX // 512).reshape(DEC_BATCH, DEC_SEQ, D)
    return (y_prompt, y_sample, jnp.stack(na_k, axis=1), jnp.stack(na_v, axis=1), jnp.stack(ckv, axis=1),
            jnp.stack(krope, axis=1), jnp.stack(gla_st, axis=1))
```

```python
import functools

import jax
import jax.numpy as jnp
from jax import lax
from jax.experimental import pallas as pl
from jax.experimental.pallas import tpu as pltpu

F32 = jnp.float32
BF16 = jnp.bfloat16

D = 2048
BATCH, SEQ = 16, 256
DEC_BATCH, DEC_SEQ = 4, 2048
PAST = 512
DEPTH = 4
N_MOD = 9
EPS = 1e-6
D_FF = 5632
GRID_W = 64
LOG2_GRID_W = 6
NEG_INF = -1e30
LOG2E = 1.4426950408889634

N_CTX = BATCH * SEQ
N_LAT = DEC_BATCH * DEC_SEQ
N_TOK = N_CTX + N_LAT
N_GROUPS = 8

NA_HEADS, NA_DH = 16, 128
NA_WIN_ROWS, NA_WIN_COLS = 8, 16
LAT_ROWS = DEC_SEQ // GRID_W
NA_QR = 8
NA_KR = NA_QR + NA_WIN_ROWS
NA_HPS = 4

MLA_HEADS = 16
MLA_Q_LORA, MLA_KV_LORA = 768, 512
MLA_NOPE, MLA_ROPE, MLA_V = 128, 64, 128
MLA_SCALE = (MLA_NOPE + MLA_ROPE) ** -0.5
ROPE_THETA = 10000.0
MLA_DOWN_N = 1536

GLA_HEADS, GLA_DK, GLA_DV = 4, 256, 512
GLA_RANK = 16
GLA_TAU = 16.0
GLA_CHUNK = 128
GLA_QKVG_N = 2 * GLA_HEADS * GLA_DK + 2 * GLA_HEADS * GLA_DV
GLA_PROJ_N = GLA_QKVG_N + 256
GLA_CTX_CHUNKS = SEQ // GLA_CHUNK
GLA_LAT_CHUNKS = DEC_SEQ // GLA_CHUNK
GLA_CTX_STEPS = BATCH * GLA_CTX_CHUNKS
GLA_STEPS = GLA_CTX_STEPS + DEC_BATCH * GLA_LAT_CHUNKS

VMEM_LIMIT = 56 * 1024 * 1024
ROW_CHUNK = 16
CTX_PPS = 8
FFN_TF = 512
FFN_SUB = 512
FFN_CVT_IN_ROWS = 16
FFN_CVT_OUT_ROWS = 64


def _cparams(sem):
    return pltpu.CompilerParams(dimension_semantics=sem, vmem_limit_bytes=VMEM_LIMIT)


def _group_of_row(r0):
    return jnp.where(r0 < N_CTX, 0, 1 + (r0 - N_CTX) // DEC_SEQ)


def _silu(x):
    return x / (1.0 + jnp.exp(-x))


def _rms(x):
    return x * lax.rsqrt(jnp.mean(x * x, axis=-1, keepdims=True) + EPS)


def _for_row_chunks(n_rows, body):
    def step(c, carry):
        body(pl.ds(pl.multiple_of(c * ROW_CHUNK, ROW_CHUNK), ROW_CHUNK))
        return carry
    lax.fori_loop(0, n_rows // ROW_CHUNK, step, 0, unroll=8)


def _dot(a, b):
    return jnp.dot(a, b, preferred_element_type=F32)


def _dot_nt(a, b):
    return lax.dot_general(a, b, (((1,), (1,)), ((), ())), preferred_element_type=F32)


def _dot_tn(a, b):
    return lax.dot_general(a, b, (((0,), (0,)), ((), ())), preferred_element_type=F32)


def _ada_kernel(c_ref, w_ref, b_ref, o_ref):
    s = _silu(c_ref[...]).astype(BF16)
    o_ref[...] = _dot(s, w_ref[...].astype(BF16)) + b_ref[...]


def _ada_mod(cond, w_ada, b_ada):
    tn = 1024
    n = N_MOD * D
    out = pl.pallas_call(
        _ada_kernel,
        out_shape=jax.ShapeDtypeStruct((DEPTH, N_GROUPS, n), F32),
        grid=(DEPTH, n // tn),
        in_specs=[
            pl.BlockSpec((N_GROUPS, D), lambda l, j: (0, 0)),
            pl.BlockSpec((None, D, tn), lambda l, j: (l, 0, j)),
            pl.BlockSpec((None, 1, tn), lambda l, j: (l, 0, j)),
        ],
        out_specs=pl.BlockSpec((None, N_GROUPS, tn), lambda l, j: (l, 0, j)),
        compiler_params=_cparams(("parallel", "parallel")),
        name="ada_mod",
    )(cond, w_ada, b_ada.reshape(DEPTH, 1, n))
    return out.reshape(DEPTH, N_GROUPS, N_MOD, 1, D)


def _mod_spec(layer, j, tm, row_off):
    return pl.BlockSpec(
        (None, None, None, 1, D),
        lambda i, n: (layer, _group_of_row((i + row_off) * tm), j, 0, 0))


def _ffn_kernel(*refs, convert_next):
    if convert_next:
        (x_ref, g_ref, sh_ref, sc_ref, gt_ref, wg_ref, wu_ref, wo_ref, nin_ref, nout_ref,
         o_ref, cin_ref, cout_ref, h_ref, gm_ref) = refs
    else:
        x_ref, g_ref, sh_ref, sc_ref, gt_ref, wg_ref, wu_ref, wo_ref, o_ref, h_ref, gm_ref = refs
    f = pl.program_id(1)
    tm = x_ref.shape[0]

    @pl.when(f == 0)
    def _():
        gm_ref[...] = g_ref[...] * (1.0 + sc_ref[...])

        def rows_fn(rows):
            h_ref[rows, :] = (_rms(x_ref[rows, :]) * gm_ref[...] + sh_ref[...]).astype(BF16)
            o_ref[rows, :] = jnp.zeros((ROW_CHUNK, D), F32)

        _for_row_chunks(tm, rows_fn)

    for r in range(0, tm, FFN_SUB):
        h = h_ref[r:r + FFN_SUB, :]
        a = _silu(_dot(h, wg_ref[...])) * _dot(h, wu_ref[...])
        o_ref[r:r + FFN_SUB, :] += _dot(a.astype(BF16), wo_ref[...])

    @pl.when(f == pl.num_programs(1) - 1)
    def _():
        o_ref[...] = x_ref[...] + 0.5 * gt_ref[...] * o_ref[...]

    if convert_next:
        cin_ref[...] = nin_ref[...].astype(BF16)
        cout_ref[...] = nout_ref[...].astype(BF16)


def _ffn(x, mod, layer, half, w_in, w_out, g, nxt=None, *, tm=1024):
    tf = FFN_TF
    nf = D_FF // tf
    j = 2 * half
    in_specs = [
        pl.BlockSpec((tm, D), lambda i, f: (i, 0)),
        pl.BlockSpec((1, D), lambda i, f: (0, 0)),
        _mod_spec(layer, 3 * j, tm, 0),
        _mod_spec(layer, 3 * j + 1, tm, 0),
        _mod_spec(layer, 3 * j + 2, tm, 0),
        pl.BlockSpec((D, tf), lambda i, f: (0, f)),
        pl.BlockSpec((D, tf), lambda i, f: (0, nf + f)),
        pl.BlockSpec((tf, D), lambda i, f: (f, 0)),
    ]
    args = [x, g.reshape(1, D), mod, mod, mod, w_in, w_in, w_out]
    out_shape = [jax.ShapeDtypeStruct((N_TOK, D), F32)]
    out_specs = [pl.BlockSpec((tm, D), lambda i, f: (i, 0))]
    if nxt is not None:
        nw_in, nw_out, nl, nh = nxt
        steps = (N_TOK // tm) * nf
        n_in, n_out = D // FFN_CVT_IN_ROWS, D_FF // FFN_CVT_OUT_ROWS
        assert n_in <= steps and n_out <= steps

        def slab(last):
            return lambda i, f: jnp.minimum(i * nf + f, last)

        s_in, s_out = slab(n_in - 1), slab(n_out - 1)
        in_specs += [pl.BlockSpec((None, None, FFN_CVT_IN_ROWS, 2 * D_FF), lambda i, f: (nl, nh, s_in(i, f), 0)),
                     pl.BlockSpec((None, None, FFN_CVT_OUT_ROWS, D), lambda i, f: (nl, nh, s_out(i, f), 0))]
        args += [nw_in, nw_out]
        out_shape += [jax.ShapeDtypeStruct((D, 2 * D_FF), BF16), jax.ShapeDtypeStruct((D_FF, D), BF16)]
        out_specs += [pl.BlockSpec((FFN_CVT_IN_ROWS, 2 * D_FF), lambda i, f: (s_in(i, f), 0)),
                      pl.BlockSpec((FFN_CVT_OUT_ROWS, D), lambda i, f: (s_out(i, f), 0))]
    outs = pl.pallas_call(
        functools.partial(_ffn_kernel, convert_next=nxt is not None),
        out_shape=out_shape,
        grid=(N_TOK // tm, nf),
        in_specs=in_specs,
        out_specs=out_specs,
        scratch_shapes=[pltpu.VMEM((tm, D), BF16), pltpu.VMEM((1, D), F32)],
        compiler_params=_cparams(("arbitrary", "arbitrary")),
        name="ffn_half",
    )(*args)
    return outs[0] if nxt is None else (outs[0], (outs[1], outs[2]))


def _mm_kernel(*refs, pro, epi, split, n_outs, tiles_per_out):
    it = iter(refs)
    x_ref = next(it)
    if pro == "cast" and split is not None:
        x2_ref = next(it)
    if pro == "gla":
        x2_ref, gin_ref = next(it), next(it)
    if pro in ("ada", "rms", "gla"):
        g_ref = next(it)
    if pro == "ada":
        sh_ref, sc_ref = next(it), next(it)
    w_ref = next(it)
    if epi == "resid":
        res_ref, gt_ref = next(it), next(it)
    o_refs = [next(it) for _ in range(n_outs)]
    if pro != "cast":
        h_ref = next(it)
    if pro == "ada":
        gm_ref = next(it)

    if pro != "cast":
        @pl.when(pl.program_id(1) == 0)
        def _():
            if pro == "ada":
                gm_ref[...] = g_ref[...] * (1.0 + sc_ref[...])

            def rows_fn(rows):
                if pro == "ada":
                    h = _rms(x_ref[rows, :]) * gm_ref[...] + sh_ref[...]
                elif pro == "rms":
                    h = _rms(x_ref[rows, :]) * g_ref[...]
                else:
                    o = x_ref[rows, :] + x2_ref[rows, :]
                    parts = [_rms(o[:, k * GLA_DV:(k + 1) * GLA_DV]) * g_ref[...] for k in range(GLA_HEADS)]
                    h = jnp.concatenate(parts, axis=-1) * _silu(gin_ref[rows, :])
                h_ref[rows, :] = h.astype(BF16)

            _for_row_chunks(x_ref.shape[0], rows_fn)

    def finish(lhs_ref):
        y = _dot(lhs_ref[...], w_ref[...])
        if epi == "resid":
            y = res_ref[...] + gt_ref[...] * y
        if n_outs == 1:
            o_refs[0][...] = y.astype(o_refs[0].dtype)
        else:
            for k, o_ref in enumerate(o_refs):
                @pl.when(pl.program_id(1) // tiles_per_out == k)
                def _(o_ref=o_ref):
                    o_ref[...] = y.astype(o_ref.dtype)

    if pro != "cast":
        finish(h_ref)
    elif split is None:
        finish(x_ref)
    else:
        pl.when(pl.program_id(0) < split)(lambda: finish(x_ref))
        pl.when(pl.program_id(0) >= split)(lambda: finish(x2_ref))


def _mm(x, w, *, rows, row_off=0, xcol=0, n_outs=1, pro, epi="plain", out_dtype=F32, tm=1024, tn=1024,
        g=None, mod=None, layer=None, jmod=None, x2=None, gin=None, gin_col=0, res=None):
    kdim, n = w.shape
    tn = min(tn, n // n_outs)
    tiles_per_out = n // n_outs // tn
    assert rows % tm == 0 and n % (n_outs * tn) == 0 and (n_outs == 1 or epi == "plain")
    split = None
    if pro == "cast" and x2 is not None:
        split = x.shape[0] // tm
        xspec = pl.BlockSpec((tm, kdim), lambda i, j: (jnp.minimum(i, split - 1), 0))
        args, specs = [x, x2], [xspec, pl.BlockSpec((tm, kdim), lambda i, j: (jnp.maximum(i - split, 0), 0))]
    else:
        xspec = pl.BlockSpec((tm, kdim), lambda i, j: (i + row_off, xcol))
        args, specs = [x], [xspec]
    if pro == "gla":
        args += [x2, gin]
        specs += [xspec, pl.BlockSpec((tm, kdim), lambda i, j: (i + row_off, gin_col))]
    if pro in ("ada", "rms", "gla"):
        args.append(g.reshape(1, -1))
        specs.append(pl.BlockSpec((1, g.shape[-1]), lambda i, j: (0, 0)))
    if pro == "ada":
        args += [mod, mod]
        specs += [_mod_spec(layer, 3 * jmod, tm, row_off), _mod_spec(layer, 3 * jmod + 1, tm, row_off)]
    args.append(w)
    specs.append(pl.BlockSpec((kdim, tn), lambda i, j: (0, j)))
    if epi == "resid":
        args += [res, mod]
        specs += [
            pl.BlockSpec((tm, tn), lambda i, j: (i + row_off, j)),
            pl.BlockSpec((None, None, None, 1, tn),
                         lambda i, j: (layer, _group_of_row((i + row_off) * tm), 3 * jmod + 2, 0, j)),
        ]
    scratch = []
    if pro != "cast":
        scratch.append(pltpu.VMEM((tm, kdim), BF16))
    if pro == "ada":
        scratch.append(pltpu.VMEM((1, kdim), F32))
    out_specs = [pl.BlockSpec((tm, tn), functools.partial(
        lambda i, j, k: (i, jnp.clip(j - k * tiles_per_out, 0, tiles_per_out - 1)), k=k)) for k in range(n_outs)]
    outs = pl.pallas_call(
        functools.partial(_mm_kernel, pro=pro, epi=epi, split=split, n_outs=n_outs, tiles_per_out=tiles_per_out),
        out_shape=[jax.ShapeDtypeStruct((rows, n // n_outs), out_dtype)] * n_outs,
        grid=(rows // tm, n // tn),
        in_specs=specs,
        out_specs=out_specs,
        scratch_shapes=scratch,
        compiler_params=_cparams(("parallel", "arbitrary")),
        name="proj_" + pro + "_" + epi,
    )(*args)
    return outs[0] if n_outs == 1 else outs


def _rmsnorm_kernel(x_ref, g_ref, o_ref):
    o_ref[...] = _rms(x_ref[...]) * g_ref[...]


def _rmsnorm(x, g, *, rows, row_off=0, xcol=0, tm=512):
    width = g.shape[-1]
    return pl.pallas_call(
        _rmsnorm_kernel,
        out_shape=jax.ShapeDtypeStruct((rows, width), F32),
        grid=(rows // tm,),
        in_specs=[pl.BlockSpec((tm, width), lambda i: (i + row_off, xcol)),
                  pl.BlockSpec((1, width), lambda i: (0, 0))],
        out_specs=pl.BlockSpec((tm, width), lambda i: (i, 0)),
        compiler_params=_cparams(("parallel",)),
        name="rmsnorm",
    )(x, g.reshape(1, width))


def _rope(x, cos, sin):
    width = x.shape[-1]
    lane = lax.broadcasted_iota(jnp.int32, x.shape, 1)
    up = pltpu.roll(x, width - 16, 1)
    down = pltpu.roll(x, 16, 1)
    swapped = jnp.where((lane & 31) < 16, up, down)
    return x * cos + swapped * sin


def _rope_keys_kernel(x_ref, cos_ref, sin_ref, o_ref):
    o_ref[...] = _rope(x_ref[...], cos_ref[...], sin_ref[...]).astype(o_ref.dtype)


def _rope_keys(down, rope_tabs, *, tm=1024):
    cos, sin = rope_tabs
    per_seq = DEC_SEQ // tm
    return pl.pallas_call(
        _rope_keys_kernel,
        out_shape=jax.ShapeDtypeStruct((N_LAT, 128), BF16),
        grid=(N_LAT // tm,),
        in_specs=[pl.BlockSpec((tm, 128), lambda i: (N_CTX // tm + i, MLA_Q_LORA // 128)),
                  pl.BlockSpec((tm, 128), lambda i: (i % per_seq, 0)),
                  pl.BlockSpec((tm, 128), lambda i: (i % per_seq, 0))],
        out_specs=pl.BlockSpec((tm, 128), lambda i: (i, 0)),
        compiler_params=_cparams(("parallel",)),
        name="rope_keys",
    )(down, cos, sin)


def _attn_kernel(*refs, nseg, has_r, rope, scale, dh, dv, pps):
    it = iter(refs)
    q_ref = next(it)
    qr_ref = next(it) if has_r else None
    segs = []
    for _ in range(nseg):
        k_ref = next(it)
        kr_ref = next(it) if has_r else None
        v_ref = next(it)
        segs.append((k_ref, kr_ref, v_ref))
    if rope:
        cq_ref, sq_ref = next(it), next(it)
    o_ref = next(it)

    if has_r:
        qr = qr_ref[...].astype(F32)
        if rope:
            qr = _rope(qr, cq_ref[...], sq_ref[...])
        qrs = [qr[:, h * MLA_ROPE:(h + 1) * MLA_ROPE].astype(BF16) for h in range(2 * pps)]
        krs = [kr_ref[:, :MLA_ROPE].astype(BF16) for _, kr_ref, _ in segs]

    nh = 2 * pps

    def qk(h):
        q = q_ref[:, h * dh:(h + 1) * dh].astype(BF16)
        ss = []
        for si, (k_ref, _, _) in enumerate(segs):
            s = _dot_nt(q, k_ref[:, h * dh:(h + 1) * dh].astype(BF16))
            if has_r:
                s = s + _dot_nt(qrs[h], krs[si])
            ss.append(s * (scale * LOG2E))
        return ss

    def softmax(ss):
        m = ss[0].max(axis=-1, keepdims=True)
        for s in ss[1:]:
            m = jnp.maximum(m, s.max(axis=-1, keepdims=True))
        es = [jnp.exp2(s - m) for s in ss]
        den = es[0].sum(axis=-1, keepdims=True)
        for e in es[1:]:
            den = den + e.sum(axis=-1, keepdims=True)
        return [e.astype(BF16) for e in es], den

    def pv(h, es, den):
        acc = None
        for e, (_, _, v_ref) in zip(es, segs):
            part = _dot(e, v_ref[:, h * dv:(h + 1) * dv].astype(BF16))
            acc = part if acc is None else acc + part
        o_ref[:, h * dv:(h + 1) * dv] = (acc / den).astype(o_ref.dtype)

    scores, probs = {}, {}
    for t in range(nh + 2):
        if t < nh:
            scores[t] = qk(t)
        if 0 <= t - 1 < nh:
            probs[t - 1] = softmax(scores.pop(t - 1))
        if 0 <= t - 2 < nh:
            pv(t - 2, *probs.pop(t - 2))


def _attention(q, segs, *, nb, lq, tq, q_row0, q_col, scale, qr=None, rope_tabs=None, pps=1, dh=128, dv=128):
    npairs = 8
    ng = npairs // pps
    nq = lq // tq
    has_r = qr is not None
    q_blk0 = q_row0 // tq
    assert q_col % pps == 0
    args = [q]
    specs = [pl.BlockSpec((tq, 2 * dh * pps), lambda b, p, t: (q_blk0 + b * nq + t, q_col // pps + p))]
    if has_r:
        qr_arr, qr_col = qr
        assert qr_col % pps == 0
        args.append(qr_arr)
        specs.append(pl.BlockSpec((tq, 2 * MLA_ROPE * pps),
                                  lambda b, p, t: (q_blk0 + b * nq + t, qr_col // pps + p)))
    for sg in segs:
        lk = sg["lk"]
        k_arr, k_row0, k_col = sg["k"]
        assert k_col % pps == 0
        args.append(k_arr)
        specs.append(pl.BlockSpec((lk, 2 * dh * pps), functools.partial(
            lambda b, p, t, r0, c0: (r0 + b, c0 + p), r0=k_row0 // lk, c0=k_col // pps)))
        if has_r:
            kr_arr, kr_row0, kr_col, kr_w = sg["kr"]
            args.append(kr_arr)
            specs.append(pl.BlockSpec((lk, kr_w), functools.partial(
                lambda b, p, t, r0, c0: (r0 + b, c0), r0=kr_row0 // lk, c0=kr_col)))
        v_arr, v_row0, v_col = sg["v"]
        assert v_col % pps == 0
        args.append(v_arr)
        specs.append(pl.BlockSpec((lk, 2 * dv * pps), functools.partial(
            lambda b, p, t, r0, c0: (r0 + b, c0 + p), r0=v_row0 // lk, c0=v_col // pps)))
    if rope_tabs is not None:
        cos, sin = (jnp.concatenate([tab] * pps, axis=-1) for tab in rope_tabs)
        args += [cos, sin]
        specs += [pl.BlockSpec((tq, 128 * pps), lambda b, p, t: (t, 0)),
                  pl.BlockSpec((tq, 128 * pps), lambda b, p, t: (t, 0))]
    return pl.pallas_call(
        functools.partial(_attn_kernel, nseg=len(segs), has_r=has_r, rope=rope_tabs is not None,
                          scale=scale, dh=dh, dv=dv, pps=pps),
        out_shape=jax.ShapeDtypeStruct((nb * lq, npairs * 2 * dv), BF16),
        grid=(nb, ng, nq),
        in_specs=specs,
        out_specs=pl.BlockSpec((tq, 2 * dv * pps), lambda b, p, t: (b * nq + t, p)),
        compiler_params=_cparams(("parallel", "parallel", "arbitrary")),
        name="attention",
    )(*args)


def _na_kernel(q_ref, k_ref, v_ref, kc_ref, vc_ref, t2_ref, o_ref, bias_ref, cap_ref):
    r0 = pl.program_id(1) * NA_QR
    ks = jnp.clip(r0 - NA_WIN_ROWS // 2, 0, LAT_ROWS - NA_KR)
    nq, nk = NA_QR * GRID_W, NA_KR * GRID_W
    scale = NA_DH ** -0.5 * LOG2E

    @pl.when(pl.program_id(2) == 0)
    def _():
        row = lax.broadcasted_iota(jnp.int32, (nq, nk), 0)
        lane = lax.broadcasted_iota(jnp.int32, (nq, nk), 1)
        qc = row & (GRID_W - 1)
        kc = lane & (GRID_W - 1)
        rs = jnp.clip(r0 + (row >> LOG2_GRID_W) - NA_WIN_ROWS // 2, 0, LAT_ROWS - NA_WIN_ROWS)
        kr = ks + (lane >> LOG2_GRID_W)
        cs = jnp.clip(qc - NA_WIN_COLS // 2, 0, GRID_W - NA_WIN_COLS)
        ok = (kr >= rs) & (kr < rs + NA_WIN_ROWS) & (kc >= cs) & (kc < cs + NA_WIN_COLS)
        cap_ref[...] = jnp.where(ok, jnp.inf, NEG_INF)
        for hh in range(NA_HPS):
            for i in range(NA_QR):
                for jp in range(NA_KR // 2):
                    e = jnp.clip(ks - r0 + 2 * jp - i + NA_WIN_ROWS, 0, 2 * NA_WIN_ROWS - 1)
                    bias_ref[hh, i * GRID_W:(i + 1) * GRID_W, jp * 128:(jp + 1) * 128] = t2_ref[hh, e] * LOG2E

    start = pl.multiple_of(ks * GRID_W, GRID_W)
    kw = k_ref[pl.ds(start, nk), :]
    vw = v_ref[pl.ds(start, nk), :]
    cap = cap_ref[...]
    def head_cols(hh):
        return slice(hh * NA_DH, (hh + 1) * NA_DH)

    def qk(hh):
        q = q_ref[:, head_cols(hh)]
        return (jnp.minimum(_dot_nt(q, kw[:, head_cols(hh)]) * scale + bias_ref[hh], cap),
                _dot_nt(q, kc_ref[:, head_cols(hh)]) * scale)

    def softmax(s_loc, s_ctx):
        m = jnp.maximum(s_loc.max(axis=-1, keepdims=True), s_ctx.max(axis=-1, keepdims=True))
        e_loc = jnp.exp2(s_loc - m)
        e_ctx = jnp.exp2(s_ctx - m)
        den = e_loc.sum(axis=-1, keepdims=True) + e_ctx.sum(axis=-1, keepdims=True)
        return e_loc.astype(BF16), e_ctx.astype(BF16), den

    def pv(hh, e_loc, e_ctx, den):
        acc = _dot(e_loc, vw[:, head_cols(hh)]) + _dot(e_ctx, vc_ref[:, head_cols(hh)])
        o_ref[:, head_cols(hh)] = (acc / den).astype(o_ref.dtype)

    scores, probs = {}, {}
    for t in range(NA_HPS + 2):
        if t < NA_HPS:
            scores[t] = qk(t)
        if 0 <= t - 1 < NA_HPS:
            probs[t - 1] = softmax(*scores.pop(t - 1))
        if 0 <= t - 2 < NA_HPS:
            pv(t - 2, *probs.pop(t - 2))


def _na_bias_table(rpb):
    shift = jnp.arange(2 * GRID_W - 1)
    ext = rpb[:, :, jnp.clip(shift - (GRID_W - NA_WIN_COLS), 0, 2 * NA_WIN_COLS - 2)]
    toe = jnp.stack([ext[:, :, GRID_W - 1 - q:2 * GRID_W - 1 - q] for q in range(GRID_W)], axis=2)
    e = jnp.arange(2 * NA_WIN_ROWS)
    dr = jnp.clip(jnp.stack([e - 1, e], axis=1), 0, 2 * NA_WIN_ROWS - 2)
    tb = toe[:, dr]
    return jnp.transpose(tb, (0, 1, 3, 2, 4)).reshape(NA_HEADS, 2 * NA_WIN_ROWS, GRID_W, 2 * GRID_W)


def _na_latent(qkv_l, kc, vc, t2):
    ngroups = NA_HEADS // NA_HPS
    wg = NA_HPS * NA_DH
    nrb = LAT_ROWS // NA_QR
    nq = NA_QR * GRID_W
    return pl.pallas_call(
        _na_kernel,
        out_shape=jax.ShapeDtypeStruct((N_LAT, D), BF16),
        grid=(ngroups, nrb, DEC_BATCH),
        in_specs=[
            pl.BlockSpec((nq, wg), lambda g, r, b: (b * nrb + r, g)),
            pl.BlockSpec((DEC_SEQ, wg), lambda g, r, b: (b, ngroups + g)),
            pl.BlockSpec((DEC_SEQ, wg), lambda g, r, b: (b, 2 * ngroups + g)),
            pl.BlockSpec((PAST, wg), lambda g, r, b: (b, g)),
            pl.BlockSpec((PAST, wg), lambda g, r, b: (b, g)),
            pl.BlockSpec((NA_HPS, 2 * NA_WIN_ROWS, GRID_W, 2 * GRID_W), lambda g, r, b: (g, 0, 0, 0)),
        ],
        out_specs=pl.BlockSpec((nq, wg), lambda g, r, b: (b * nrb + r, g)),
        scratch_shapes=[pltpu.VMEM((NA_HPS, nq, NA_KR * GRID_W), F32), pltpu.VMEM((nq, NA_KR * GRID_W), F32)],
        compiler_params=_cparams(("parallel", "parallel", "arbitrary")),
        name="na_latent",
    )(qkv_l, qkv_l, qkv_l, kc, vc, t2)


def _gla_pos(t):
    is_ctx = t < GLA_CTX_STEPS
    u = jnp.maximum(t - GLA_CTX_STEPS, 0)
    seq = jnp.where(is_ctx, t // GLA_CTX_CHUNKS, u // GLA_LAT_CHUNKS)
    n = jnp.where(is_ctx, t % GLA_CTX_CHUNKS, u % GLA_LAT_CHUNKS)
    return is_ctx, seq, n


def _gla_bwd_chunk(t):
    is_ctx, _, n = _gla_pos(t)
    return t + jnp.where(is_ctx, GLA_CTX_CHUNKS, GLA_LAT_CHUNKS) - 1 - 2 * n


def _gla_kernel(qf_ref, kf_ref, vf_ref, gdf_ref, qb_ref, kb_ref, vb_ref, gdb_ref, wgu_ref, bg_ref, s0_ref,
                of_ref, ob_ref, fin_ref, st_ref):
    is_ctx, _, n = _gla_pos(pl.program_id(0))
    is_lat = jnp.logical_not(is_ctx)

    @pl.when(jnp.logical_and(n == 0, is_ctx))
    def _():
        st_ref[...] = jnp.zeros_like(st_ref)

    @pl.when(jnp.logical_and(n == 0, is_lat))
    def _():
        st_ref[...] = s0_ref[...]

    ri = lax.broadcasted_iota(jnp.int32, (GLA_CHUNK, GLA_CHUNK), 0)
    ci = lax.broadcasted_iota(jnp.int32, (GLA_CHUNK, GLA_CHUNK), 1)
    streams = ((qf_ref, kf_ref, vf_ref, gdf_ref, of_ref), (qb_ref, kb_ref, vb_ref, gdb_ref, ob_ref))
    for d, (q_ref, k_ref, v_ref, gd_ref, o_ref) in enumerate(streams):
        keep = (ci <= ri) if d == 0 else (ci >= ri)
        tri = jnp.where(keep, 1.0, 0.0).astype(BF16)
        pre = _dot(gd_ref[...].astype(BF16), wgu_ref[d]) + bg_ref[d]
        la = (jnp.minimum(pre, 0.0) - jnp.log1p(jnp.exp(-jnp.abs(pre)))) / GLA_TAU
        hi = la.astype(BF16)
        r1 = la - hi.astype(F32)
        mid = r1.astype(BF16)
        lo = (r1 - mid.astype(F32)).astype(BF16)
        cum = _dot(tri, hi) + _dot(tri, mid) + _dot(tri, lo)
        tot = jnp.sum(la, axis=0, keepdims=True)
        k = k_ref[...]
        q_dec = (q_ref[...] * (GLA_DK ** -0.5) * jnp.exp(cum)).astype(BF16)
        k_inv = (k * jnp.exp(-cum)).astype(BF16)
        k_end = (k * jnp.exp(tot - cum)).astype(BF16)
        dec = jnp.exp(tot)
        v = v_ref[...].astype(BF16)
        for h in range(GLA_HEADS):
            ks = slice(h * GLA_DK, (h + 1) * GLA_DK)
            vs = slice(h * GLA_DV, (h + 1) * GLA_DV)
            att = jnp.where(keep, _dot_nt(q_dec[:, ks], k_inv[:, ks]), 0.0)
            st = st_ref[d, h]
            o_ref[:, vs] = _dot(att.astype(BF16), v[:, vs]) + _dot_nt(q_dec[:, ks], st.astype(BF16))
            st_ref[d, h] = st * dec[:, ks] + _dot_tn(v[:, vs], k_end[:, ks])

    @pl.when(jnp.logical_and(is_ctx, n == GLA_CTX_CHUNKS - 1))
    def _():
        fin_ref[...] = st_ref[...]


def _gla_scan(proj, wgu, bg, s0t):
    hk = GLA_HEADS * GLA_DK
    hv = GLA_HEADS * GLA_DV
    gd_col = GLA_QKVG_N // 128

    def fwd(c):
        return lambda t: (t, c)

    def bwd(c):
        return lambda t: (_gla_bwd_chunk(t), c)

    in_specs = []
    for ix in (fwd, bwd):
        in_specs += [pl.BlockSpec((GLA_CHUNK, hk), ix(0)), pl.BlockSpec((GLA_CHUNK, hk), ix(1)),
                     pl.BlockSpec((GLA_CHUNK, hv), ix(1)), pl.BlockSpec((GLA_CHUNK, 128), ix(gd_col))]
    st_block = (None, 2, GLA_HEADS, GLA_DV, GLA_DK)
    in_specs += [
        pl.BlockSpec((2, 128, hk), lambda t: (0, 0, 0)),
        pl.BlockSpec((2, 1, hk), lambda t: (0, 0, 0)),
        pl.BlockSpec(st_block, lambda t: (jnp.where(_gla_pos(t)[0], 0, _gla_pos(t)[1]), 0, 0, 0, 0)),
    ]
    o_shape = jax.ShapeDtypeStruct((N_TOK, hv), F32)
    return pl.pallas_call(
        _gla_kernel,
        out_shape=(o_shape, o_shape, jax.ShapeDtypeStruct((BATCH, 2, GLA_HEADS, GLA_DV, GLA_DK), F32)),
        grid=(GLA_STEPS,),
        in_specs=in_specs,
        out_specs=(
            pl.BlockSpec((GLA_CHUNK, hv), lambda t: (t, 0)),
            pl.BlockSpec((GLA_CHUNK, hv), lambda t: (_gla_bwd_chunk(t), 0)),
            pl.BlockSpec(st_block, lambda t: (jnp.where(_gla_pos(t)[0], _gla_pos(t)[1], BATCH - 1), 0, 0, 0, 0)),
        ),
        scratch_shapes=[pltpu.VMEM((2, GLA_HEADS, GLA_DV, GLA_DK), F32)],
        compiler_params=_cparams(("arbitrary",)),
        name="gla_scan",
    )(*([proj] * 8), wgu, bg, s0t)


def _rope_tables():
    t = jnp.arange(DEC_SEQ)
    d = MLA_ROPE // 2
    inv = ROPE_THETA ** (-jnp.arange(0, d, 2, dtype=F32) / d)
    ang_r = (t // GRID_W).astype(F32)[:, None] * inv[None]
    ang_c = (t % GRID_W).astype(F32)[:, None] * inv[None]
    cos = jnp.concatenate([jnp.cos(ang_r)] * 2 + [jnp.cos(ang_c)] * 2, axis=-1)
    sin = jnp.concatenate([-jnp.sin(ang_r), jnp.sin(ang_r), -jnp.sin(ang_c), jnp.sin(ang_c)], axis=-1)
    return jnp.concatenate([cos, cos], axis=-1), jnp.concatenate([sin, sin], axis=-1)


def _mixer_na(x, mod, layer, g, w_qkv, w_o, rpb, cache_k, cache_v):
    w_qkv = w_qkv.astype(BF16)
    ada = dict(pro="ada", g=g, mod=mod, layer=layer, jmod=1)
    q_c, k_c, v_c = _mm(x, w_qkv, rows=N_CTX, n_outs=3, out_dtype=F32, tm=512, **ada)
    qkv_l = _mm(x, w_qkv, rows=N_LAT, row_off=N_CTX // 1024, out_dtype=BF16, tn=1536, **ada)
    o_c = _attention(q_c, [dict(k=(k_c, 0, 0), v=(v_c, 0, 0), lk=SEQ)],
                     nb=BATCH, lq=SEQ, tq=SEQ, q_row0=0, q_col=0, scale=NA_DH ** -0.5, pps=CTX_PPS)
    kc = cache_k.reshape(DEC_BATCH * PAST, D).astype(BF16)
    vc = cache_v.reshape(DEC_BATCH * PAST, D).astype(BF16)
    o_l = _na_latent(qkv_l, kc, vc, _na_bias_table(rpb))
    x = _mm(o_c, w_o.astype(BF16), rows=N_TOK, pro="cast", x2=o_l, epi="resid", res=x, mod=mod, layer=layer,
            jmod=1, tm=512, tn=2048)
    return x, k_c.reshape(BATCH, SEQ, NA_HEADS, NA_DH), v_c.reshape(BATCH, SEQ, NA_HEADS, NA_DH)


def _mixer_mla(x, mod, layer, g, w_down, g_q, w_uq, g_kv, w_ukv, w_o, cache_ckv, cache_kr):
    hq = MLA_NOPE + MLA_ROPE
    wd = jnp.concatenate([
        w_down[:, :MLA_Q_LORA],
        w_down[:, MLA_Q_LORA + MLA_KV_LORA:],
        jnp.zeros((D, 256 - MLA_ROPE), F32),
        w_down[:, MLA_Q_LORA:MLA_Q_LORA + MLA_KV_LORA]], axis=1).astype(BF16)
    wq = w_uq.reshape(MLA_Q_LORA, MLA_HEADS, hq)
    wq = jnp.concatenate([wq[:, :, :MLA_NOPE].reshape(MLA_Q_LORA, -1),
                          wq[:, :, MLA_NOPE:].reshape(MLA_Q_LORA, -1)], axis=1).astype(BF16)
    wkv = w_ukv.reshape(MLA_KV_LORA, MLA_HEADS, MLA_NOPE + MLA_V)
    wkv = jnp.concatenate([wkv[:, :, :MLA_NOPE].reshape(MLA_KV_LORA, -1),
                           wkv[:, :, MLA_NOPE:].reshape(MLA_KV_LORA, -1)], axis=1).astype(BF16)

    down = _mm(x, wd, rows=N_TOK, pro="ada", g=g, mod=mod, layer=layer, jmod=1, out_dtype=F32, tn=MLA_DOWN_N)
    q = _mm(down, wq, rows=N_TOK, pro="rms", g=g_q, out_dtype=F32, tn=1536)
    kv = _mm(down, wkv, rows=N_TOK, xcol=2, pro="rms", g=g_kv, out_dtype=BF16, tn=2048)
    kv_cache = _mm(cache_ckv.reshape(DEC_BATCH * PAST, MLA_KV_LORA).astype(BF16), wkv,
                   rows=DEC_BATCH * PAST, pro="cast", out_dtype=BF16, tn=2048)
    ckv_c = _rmsnorm(down, g_kv, rows=N_CTX, xcol=2)
    kr_c = down[:N_CTX, MLA_Q_LORA:MLA_Q_LORA + MLA_ROPE]

    npairs = MLA_HEADS // 2
    kr_col = MLA_Q_LORA // 128
    o_c = _attention(q, [dict(k=(kv, 0, 0), v=(kv, 0, npairs), kr=(down, 0, kr_col, 128), lk=SEQ)],
                     nb=BATCH, lq=SEQ, tq=SEQ, q_row0=0, q_col=0, qr=(q, 2 * npairs), scale=MLA_SCALE, pps=CTX_PPS)
    tabs = _rope_tables()
    kr_l = _rope_keys(down, tabs)
    o_l = _attention(
        q,
        [dict(k=(kv, N_CTX, 0), v=(kv, N_CTX, npairs), kr=(kr_l, 0, 0, 128), lk=DEC_SEQ),
         dict(k=(kv_cache, 0, 0), v=(kv_cache, 0, npairs),
              kr=(cache_kr.reshape(DEC_BATCH * PAST, MLA_ROPE), 0, 0, MLA_ROPE), lk=PAST)],
        nb=DEC_BATCH, lq=DEC_SEQ, tq=512, q_row0=N_CTX, q_col=0, qr=(q, 2 * npairs), scale=MLA_SCALE,
        rope_tabs=tabs, pps=2)
    x = _mm(o_c, w_o.astype(BF16), rows=N_TOK, pro="cast", x2=o_l, epi="resid", res=x, mod=mod, layer=layer,
            jmod=1, tm=512, tn=2048)
    return x, ckv_c.reshape(BATCH, SEQ, MLA_KV_LORA), kr_c.reshape(BATCH, SEQ, MLA_ROPE)


def _mixer_gla(x, mod, layer, g, w_in, w_gd, w_gu, b_g, g_norm, w_o, state):
    pad = jnp.zeros((D, GLA_PROJ_N - GLA_QKVG_N - 2 * GLA_RANK), F32)
    w_cat = jnp.concatenate([w_in, w_gd[0], w_gd[1], pad], axis=1).astype(BF16)
    proj = _mm(x, w_cat, rows=N_TOK, pro="ada", g=g, mod=mod, layer=layer, jmod=1, out_dtype=F32, tn=1280)
    wgu = jnp.zeros((2, 128, GLA_HEADS * GLA_DK), F32)
    wgu = wgu.at[0, :GLA_RANK].set(w_gu[0]).at[1, GLA_RANK:2 * GLA_RANK].set(w_gu[1]).astype(BF16)
    bg = b_g.reshape(2, 1, GLA_HEADS * GLA_DK)
    o_f, o_b, st_c = _gla_scan(proj, wgu, bg, jnp.swapaxes(state, -1, -2))
    x = _mm(o_f, w_o.astype(BF16), rows=N_TOK, pro="gla", x2=o_b, gin=proj, gin_col=2, g=g_norm,
            epi="resid", res=x, mod=mod, layer=layer, jmod=1, tm=256, tn=2048)
    return x, jnp.swapaxes(st_c, -1, -2)


def kernel(x_prompt, x_sample, cache_na_k, cache_na_v, cache_mla_ckv, cache_mla_krope, state_gla, c, c_ctx, norm_g, w_ada, b_ada, w_ffn_in, w_ffn_out, w_na_qkv, w_na_o, na_rpb, w_mla_down, g_mla_q, w_mla_uq, g_mla_kv, w_mla_ukv, w_mla_o, w_gla_in, w_gla_gate_down, w_gla_gate_up, b_gla_gate, g_gla_norm, w_gla_o, final_norm_g):
    x = jnp.concatenate([x_prompt.reshape(N_CTX, D), x_sample.reshape(N_LAT, D)], axis=0)
    cond = jnp.concatenate([c_ctx[None], c, jnp.zeros((N_GROUPS - 1 - DEC_BATCH, D), F32)], axis=0)
    mod = _ada_mod(cond, w_ada, b_ada)
    w_ffn = (w_ffn_in[0, 0].astype(BF16), w_ffn_out[0, 0].astype(BF16))

    def ffn(x, w_ffn, layer, half):
        last = layer == DEPTH - 1 and half == 1
        nxt = None if last else (w_ffn_in, w_ffn_out, layer + half, 1 - half)
        out = _ffn(x, mod, layer, half, w_ffn[0], w_ffn[1], norm_g[layer, 2 * half], nxt)
        return (out, None) if last else out

    na_k, na_v, ckv, krope, gla_st = [], [], [], [], []
    for i in range(DEPTH):
        kind, slot = i % 3, i // 3
        x, w_ffn = ffn(x, w_ffn, i, 0)
        if kind == 0:
            x, k_c, v_c = _mixer_na(x, mod, i, norm_g[i, 1], w_na_qkv[slot], w_na_o[slot], na_rpb[slot],
                                    cache_na_k[:, slot], cache_na_v[:, slot])
            na_k.append(k_c)
            na_v.append(v_c)
        elif kind == 1:
            x, ckv_c, kr_c = _mixer_mla(x, mod, i, norm_g[i, 1], w_mla_down[slot], g_mla_q[slot], w_mla_uq[slot],
                                        g_mla_kv[slot], w_mla_ukv[slot], w_mla_o[slot],
                                        cache_mla_ckv[:, slot], cache_mla_krope[:, slot])
            ckv.append(ckv_c)
            krope.append(kr_c)
        else:
            x, st = _mixer_gla(x, mod, i, norm_g[i, 1], w_gla_in[slot], w_gla_gate_down[slot],
                               w_gla_gate_up[slot], b_gla_gate[slot], g_gla_norm[slot], w_gla_o[slot],
                               state_gla[:, slot])
            gla_st.append(st)
        x, w_ffn = ffn(x, w_ffn, i, 1)

    y_prompt = _rmsnorm(x, final_norm_g, rows=N_CTX).reshape(BATCH, SEQ, D)
    y_sample = _rmsnorm(x, final_norm_g, rows=N_LAT, row_off=N_CTX // 512).reshape(DEC_BATCH, DEC_SEQ, D)
    return (y_prompt, y_sample, jnp.stack(na_k, axis=1), jnp.stack(na_v, axis=1), jnp.stack(ckv, axis=1),
            jnp.stack(krope, axis=1), jnp.stack(gla_st, axis=1))
```

```python
import functools

import jax
import jax.numpy as jnp
from jax import lax
from jax.experimental import pallas as pl
from jax.experimental.pallas import tpu as pltpu

F32 = jnp.float32
BF16 = jnp.bfloat16

D = 2048
BATCH, SEQ = 16, 256
DEC_BATCH, DEC_SEQ = 4, 2048
PAST = 512
DEPTH = 4
N_MOD = 9
EPS = 1e-6
D_FF = 5632
GRID_W = 64
LOG2_GRID_W = 6
NEG_INF = -1e30
LOG2E = 1.4426950408889634

N_CTX = BATCH * SEQ
N_LAT = DEC_BATCH * DEC_SEQ
N_TOK = N_CTX + N_LAT
N_GROUPS = 8

NA_HEADS, NA_DH = 16, 128
NA_WIN_ROWS, NA_WIN_COLS = 8, 16
LAT_ROWS = DEC_SEQ // GRID_W
NA_QR = 8
NA_KR = NA_QR + NA_WIN_ROWS
NA_HPS = 4

MLA_HEADS = 16
MLA_Q_LORA, MLA_KV_LORA = 768, 512
MLA_NOPE, MLA_ROPE, MLA_V = 128, 64, 128
MLA_SCALE = (MLA_NOPE + MLA_ROPE) ** -0.5
ROPE_THETA = 10000.0
MLA_DOWN_N = 1536

GLA_HEADS, GLA_DK, GLA_DV = 4, 256, 512
GLA_RANK = 16
GLA_TAU = 16.0
GLA_CHUNK = 128
GLA_QKVG_N = 2 * GLA_HEADS * GLA_DK + 2 * GLA_HEADS * GLA_DV
GLA_PROJ_N = GLA_QKVG_N + 256
GLA_CTX_CHUNKS = SEQ // GLA_CHUNK
GLA_LAT_CHUNKS = DEC_SEQ // GLA_CHUNK
GLA_CTX_STEPS = BATCH * GLA_CTX_CHUNKS
GLA_STEPS = GLA_CTX_STEPS + DEC_BATCH * GLA_LAT_CHUNKS

VMEM_LIMIT = 56 * 1024 * 1024
ROW_CHUNK = 16
CTX_PPS = 8
FFN_TF = 512
FFN_SUB = 512
FFN_CVT_IN_ROWS = 16
FFN_CVT_OUT_ROWS = 64


def _cparams(sem):
    return pltpu.CompilerParams(dimension_semantics=sem, vmem_limit_bytes=VMEM_LIMIT)


def _group_of_row(r0):
    return jnp.where(r0 < N_CTX, 0, 1 + (r0 - N_CTX) // DEC_SEQ)


def _silu(x):
    return x / (1.0 + jnp.exp(-x))


def _rms(x):
    return x * lax.rsqrt(jnp.mean(x * x, axis=-1, keepdims=True) + EPS)


def _for_row_chunks(n_rows, body):
    def step(c, carry):
        body(pl.ds(pl.multiple_of(c * ROW_CHUNK, ROW_CHUNK), ROW_CHUNK))
        return carry
    lax.fori_loop(0, n_rows // ROW_CHUNK, step, 0, unroll=8)


def _dot(a, b):
    return jnp.dot(a, b, preferred_element_type=F32)


def _dot_nt(a, b):
    return lax.dot_general(a, b, (((1,), (1,)), ((), ())), preferred_element_type=F32)


def _dot_tn(a, b):
    return lax.dot_general(a, b, (((0,), (0,)), ((), ())), preferred_element_type=F32)


def _ada_kernel(c_ref, w_ref, b_ref, o_ref):
    s = _silu(c_ref[...]).astype(BF16)
    o_ref[...] = _dot(s, w_ref[...].astype(BF16)) + b_ref[...]


def _ada_mod(cond, w_ada, b_ada):
    tn = 1024
    n = N_MOD * D
    out = pl.pallas_call(
        _ada_kernel,
        out_shape=jax.ShapeDtypeStruct((DEPTH, N_GROUPS, n), F32),
        grid=(DEPTH, n // tn),
        in_specs=[
            pl.BlockSpec((N_GROUPS, D), lambda l, j: (0, 0)),
            pl.BlockSpec((None, D, tn), lambda l, j: (l, 0, j)),
            pl.BlockSpec((None, 1, tn), lambda l, j: (l, 0, j)),
        ],
        out_specs=pl.BlockSpec((None, N_GROUPS, tn), lambda l, j: (l, 0, j)),
        compiler_params=_cparams(("parallel", "parallel")),
        name="ada_mod",
    )(cond, w_ada, b_ada.reshape(DEPTH, 1, n))
    return out.reshape(DEPTH, N_GROUPS, N_MOD, 1, D)


def _mod_spec(layer, j, tm, row_off):
    return pl.BlockSpec(
        (None, None, None, 1, D),
        lambda i, n: (layer, _group_of_row((i + row_off) * tm), j, 0, 0))


def _ffn_kernel(*refs, convert_next):
    if convert_next:
        (x_ref, g_ref, sh_ref, sc_ref, gt_ref, wg_ref, wu_ref, wo_ref, nin_ref, nout_ref,
         o_ref, cin_ref, cout_ref, h_ref, gm_ref) = refs
    else:
        x_ref, g_ref, sh_ref, sc_ref, gt_ref, wg_ref, wu_ref, wo_ref, o_ref, h_ref, gm_ref = refs
    f = pl.program_id(1)
    tm = x_ref.shape[0]

    @pl.when(f == 0)
    def _():
        gm_ref[...] = g_ref[...] * (1.0 + sc_ref[...])

        def rows_fn(rows):
            h_ref[rows, :] = (_rms(x_ref[rows, :]) * gm_ref[...] + sh_ref[...]).astype(BF16)
            o_ref[rows, :] = jnp.zeros((ROW_CHUNK, D), F32)

        _for_row_chunks(tm, rows_fn)

    for r in range(0, tm, FFN_SUB):
        h = h_ref[r:r + FFN_SUB, :]
        a = _silu(_dot(h, wg_ref[...])) * _dot(h, wu_ref[...])
        o_ref[r:r + FFN_SUB, :] += _dot(a.astype(BF16), wo_ref[...])

    @pl.when(f == pl.num_programs(1) - 1)
    def _():
        o_ref[...] = x_ref[...] + 0.5 * gt_ref[...] * o_ref[...]

    if convert_next:
        cin_ref[...] = nin_ref[...].astype(BF16)
        cout_ref[...] = nout_ref[...].astype(BF16)


def _ffn(x, mod, layer, half, w_in, w_out, g, nxt=None, *, tm=1024):
    tf = FFN_TF
    nf = D_FF // tf
    j = 2 * half
    in_specs = [
        pl.BlockSpec((tm, D), lambda i, f: (i, 0)),
        pl.BlockSpec((1, D), lambda i, f: (0, 0)),
        _mod_spec(layer, 3 * j, tm, 0),
        _mod_spec(layer, 3 * j + 1, tm, 0),
        _mod_spec(layer, 3 * j + 2, tm, 0),
        pl.BlockSpec((D, tf), lambda i, f: (0, f)),
        pl.BlockSpec((D, tf), lambda i, f: (0, nf + f)),
        pl.BlockSpec((tf, D), lambda i, f: (f, 0)),
    ]
    args = [x, g.reshape(1, D), mod, mod, mod, w_in, w_in, w_out]
    out_shape = [jax.ShapeDtypeStruct((N_TOK, D), F32)]
    out_specs = [pl.BlockSpec((tm, D), lambda i, f: (i, 0))]
    if nxt is not None:
        nw_in, nw_out, nl, nh = nxt
        steps = (N_TOK // tm) * nf
        n_in, n_out = D // FFN_CVT_IN_ROWS, D_FF // FFN_CVT_OUT_ROWS
        assert n_in <= steps and n_out <= steps

        def slab(last):
            return lambda i, f: jnp.minimum(i * nf + f, last)

        s_in, s_out = slab(n_in - 1), slab(n_out - 1)
        in_specs += [pl.BlockSpec((None, None, FFN_CVT_IN_ROWS, 2 * D_FF), lambda i, f: (nl, nh, s_in(i, f), 0)),
                     pl.BlockSpec((None, None, FFN_CVT_OUT_ROWS, D), lambda i, f: (nl, nh, s_out(i, f), 0))]
        args += [nw_in, nw_out]
        out_shape += [jax.ShapeDtypeStruct((D, 2 * D_FF), BF16), jax.ShapeDtypeStruct((D_FF, D), BF16)]
        out_specs += [pl.BlockSpec((FFN_CVT_IN_ROWS, 2 * D_FF), lambda i, f: (s_in(i, f), 0)),
                      pl.BlockSpec((FFN_CVT_OUT_ROWS, D), lambda i, f: (s_out(i, f), 0))]
    outs = pl.pallas_call(
        functools.partial(_ffn_kernel, convert_next=nxt is not None),
        out_shape=out_shape,
        grid=(N_TOK // tm, nf),
        in_specs=in_specs,
        out_specs=out_specs,
        scratch_shapes=[pltpu.VMEM((tm, D), BF16), pltpu.VMEM((1, D), F32)],
        compiler_params=_cparams(("arbitrary", "arbitrary")),
        name="ffn_half",
    )(*args)
    return outs[0] if nxt is None else (outs[0], (outs[1], outs[2]))


def _mm_kernel(*refs, pro, epi, split, n_outs, tiles_per_out):
    it = iter(refs)
    x_ref = next(it)
    if pro == "cast" and split is not None:
        x2_ref = next(it)
    if pro == "gla":
        x2_ref, gin_ref = next(it), next(it)
    if pro in ("ada", "rms", "gla"):
        g_ref = next(it)
    if pro == "ada":
        sh_ref, sc_ref = next(it), next(it)
    w_ref = next(it)
    if epi == "resid":
        res_ref, gt_ref = next(it), next(it)
    o_refs = [next(it) for _ in range(n_outs)]
    if pro != "cast":
        h_ref = next(it)
    if pro == "ada":
        gm_ref = next(it)

    if pro != "cast":
        @pl.when(pl.program_id(1) == 0)
        def _():
            if pro == "ada":
                gm_ref[...] = g_ref[...] * (1.0 + sc_ref[...])

            def rows_fn(rows):
                if pro == "ada":
                    h = _rms(x_ref[rows, :]) * gm_ref[...] + sh_ref[...]
                elif pro == "rms":
                    h = _rms(x_ref[rows, :]) * g_ref[...]
                else:
                    o = x_ref[rows, :] + x2_ref[rows, :]
                    parts = [_rms(o[:, k * GLA_DV:(k + 1) * GLA_DV]) * g_ref[...] for k in range(GLA_HEADS)]
                    h = jnp.concatenate(parts, axis=-1) * _silu(gin_ref[rows, :])
                h_ref[rows, :] = h.astype(BF16)

            _for_row_chunks(x_ref.shape[0], rows_fn)

    def finish(lhs_ref):
        y = _dot(lhs_ref[...], w_ref[...])
        if epi == "resid":
            y = res_ref[...] + gt_ref[...] * y
        if n_outs == 1:
            o_refs[0][...] = y.astype(o_refs[0].dtype)
        else:
            for k, o_ref in enumerate(o_refs):
                @pl.when(pl.program_id(1) // tiles_per_out == k)
                def _(o_ref=o_ref):
                    o_ref[...] = y.astype(o_ref.dtype)

    if pro != "cast":
        finish(h_ref)
    elif split is None:
        finish(x_ref)
    else:
        pl.when(pl.program_id(0) < split)(lambda: finish(x_ref))
        pl.when(pl.program_id(0) >= split)(lambda: finish(x2_ref))


def _mm(x, w, *, rows, row_off=0, xcol=0, n_outs=1, pro, epi="plain", out_dtype=F32, tm=1024, tn=1024,
        g=None, mod=None, layer=None, jmod=None, x2=None, gin=None, gin_col=0, res=None):
    kdim, n = w.shape
    tn = min(tn, n // n_outs)
    tiles_per_out = n // n_outs // tn
    assert rows % tm == 0 and n % (n_outs * tn) == 0 and (n_outs == 1 or epi == "plain")
    split = None
    if pro == "cast" and x2 is not None:
        split = x.shape[0] // tm
        xspec = pl.BlockSpec((tm, kdim), lambda i, j: (jnp.minimum(i, split - 1), 0))
        args, specs = [x, x2], [xspec, pl.BlockSpec((tm, kdim), lambda i, j: (jnp.maximum(i - split, 0), 0))]
    else:
        xspec = pl.BlockSpec((tm, kdim), lambda i, j: (i + row_off, xcol))
        args, specs = [x], [xspec]
    if pro == "gla":
        args += [x2, gin]
        specs += [xspec, pl.BlockSpec((tm, kdim), lambda i, j: (i + row_off, gin_col))]
    if pro in ("ada", "rms", "gla"):
        args.append(g.reshape(1, -1))
        specs.append(pl.BlockSpec((1, g.shape[-1]), lambda i, j: (0, 0)))
    if pro == "ada":
        args += [mod, mod]
        specs += [_mod_spec(layer, 3 * jmod, tm, row_off), _mod_spec(layer, 3 * jmod + 1, tm, row_off)]
    args.append(w)
    specs.append(pl.BlockSpec((kdim, tn), lambda i, j: (0, j)))
    if epi == "resid":
        args += [res, mod]
        specs += [
            pl.BlockSpec((tm, tn), lambda i, j: (i + row_off, j)),
            pl.BlockSpec((None, None, None, 1, tn),
                         lambda i, j: (layer, _group_of_row((i + row_off) * tm), 3 * jmod + 2, 0, j)),
        ]
    scratch = []
    if pro != "cast":
        scratch.append(pltpu.VMEM((tm, kdim), BF16))
    if pro == "ada":
        scratch.append(pltpu.VMEM((1, kdim), F32))
    out_specs = [pl.BlockSpec((tm, tn), functools.partial(
        lambda i, j, k: (i, jnp.clip(j - k * tiles_per_out, 0, tiles_per_out - 1)), k=k)) for k in range(n_outs)]
    outs = pl.pallas_call(
        functools.partial(_mm_kernel, pro=pro, epi=epi, split=split, n_outs=n_outs, tiles_per_out=tiles_per_out),
        out_shape=[jax.ShapeDtypeStruct((rows, n // n_outs), out_dtype)] * n_outs,
        grid=(rows // tm, n // tn),
        in_specs=specs,
        out_specs=out_specs,
        scratch_shapes=scratch,
        compiler_params=_cparams(("parallel", "arbitrary")),
        name="proj_" + pro + "_" + epi,
    )(*args)
    return outs[0] if n_outs == 1 else outs


def _rmsnorm_kernel(x_ref, g_ref, o_ref):
    o_ref[...] = _rms(x_ref[...]) * g_ref[...]


def _rmsnorm(x, g, *, rows, row_off=0, xcol=0, tm=512):
    width = g.shape[-1]
    return pl.pallas_call(
        _rmsnorm_kernel,
        out_shape=jax.ShapeDtypeStruct((rows, width), F32),
        grid=(rows // tm,),
        in_specs=[pl.BlockSpec((tm, width), lambda i: (i + row_off, xcol)),
                  pl.BlockSpec((1, width), lambda i: (0, 0))],
        out_specs=pl.BlockSpec((tm, width), lambda i: (i, 0)),
        compiler_params=_cparams(("parallel",)),
        name="rmsnorm",
    )(x, g.reshape(1, width))


def _rope(x, cos, sin):
    width = x.shape[-1]
    lane = lax.broadcasted_iota(jnp.int32, x.shape, 1)
    up = pltpu.roll(x, width - 16, 1)
    down = pltpu.roll(x, 16, 1)
    swapped = jnp.where((lane & 31) < 16, up, down)
    return x * cos + swapped * sin


def _rope_keys_kernel(x_ref, cos_ref, sin_ref, o_ref):
    o_ref[...] = _rope(x_ref[...], cos_ref[...], sin_ref[...]).astype(o_ref.dtype)


def _rope_keys(down, rope_tabs, *, tm=1024):
    cos, sin = rope_tabs
    per_seq = DEC_SEQ // tm
    return pl.pallas_call(
        _rope_keys_kernel,
        out_shape=jax.ShapeDtypeStruct((N_LAT, 128), BF16),
        grid=(N_LAT // tm,),
        in_specs=[pl.BlockSpec((tm, 128), lambda i: (N_CTX // tm + i, MLA_Q_LORA // 128)),
                  pl.BlockSpec((tm, 128), lambda i: (i % per_seq, 0)),
                  pl.BlockSpec((tm, 128), lambda i: (i % per_seq, 0))],
        out_specs=pl.BlockSpec((tm, 128), lambda i: (i, 0)),
        compiler_params=_cparams(("parallel",)),
        name="rope_keys",
    )(down, cos, sin)


def _attn_kernel(*refs, nseg, has_r, rope, scale, dh, dv, pps):
    it = iter(refs)
    q_ref = next(it)
    qr_ref = next(it) if has_r else None
    segs = []
    for _ in range(nseg):
        k_ref = next(it)
        kr_ref = next(it) if has_r else None
        v_ref = next(it)
        segs.append((k_ref, kr_ref, v_ref))
    if rope:
        cq_ref, sq_ref = next(it), next(it)
    o_ref = next(it)

    if has_r:
        qr = qr_ref[...].astype(F32)
        if rope:
            qr = _rope(qr, cq_ref[...], sq_ref[...])
        qrs = [qr[:, h * MLA_ROPE:(h + 1) * MLA_ROPE].astype(BF16) for h in range(2 * pps)]
        krs = [kr_ref[:, :MLA_ROPE].astype(BF16) for _, kr_ref, _ in segs]

    nh = 2 * pps

    def qk(h):
        q = q_ref[:, h * dh:(h + 1) * dh].astype(BF16)
        ss = []
        for si, (k_ref, _, _) in enumerate(segs):
            s = _dot_nt(q, k_ref[:, h * dh:(h + 1) * dh].astype(BF16))
            if has_r:
                s = s + _dot_nt(qrs[h], krs[si])
            ss.append(s * (scale * LOG2E))
        return ss

    def softmax(ss):
        m = ss[0].max(axis=-1, keepdims=True)
        for s in ss[1:]:
            m = jnp.maximum(m, s.max(axis=-1, keepdims=True))
        es = [jnp.exp2(s - m) for s in ss]
        den = es[0].sum(axis=-1, keepdims=True)
        for e in es[1:]:
            den = den + e.sum(axis=-1, keepdims=True)
        return [e.astype(BF16) for e in es], den

    def pv(h, es, den):
        acc = None
        for e, (_, _, v_ref) in zip(es, segs):
            part = _dot(e, v_ref[:, h * dv:(h + 1) * dv].astype(BF16))
            acc = part if acc is None else acc + part
        o_ref[:, h * dv:(h + 1) * dv] = (acc / den).astype(o_ref.dtype)

    scores, probs = {}, {}
    for t in range(nh + 2):
        if t < nh:
            scores[t] = qk(t)
        if 0 <= t - 1 < nh:
            probs[t - 1] = softmax(scores.pop(t - 1))
        if 0 <= t - 2 < nh:
            pv(t - 2, *probs.pop(t - 2))


def _attention(q, segs, *, nb, lq, tq, q_row0, q_col, scale, qr=None, rope_tabs=None, pps=1, dh=128, dv=128):
    npairs = 8
    ng = npairs // pps
    nq = lq // tq
    has_r = qr is not None
    q_blk0 = q_row0 // tq
    assert q_col % pps == 0
    args = [q]
    specs = [pl.BlockSpec((tq, 2 * dh * pps), lambda b, p, t: (q_blk0 + b * nq + t, q_col // pps + p))]
    if has_r:
        qr_arr, qr_col = qr
        assert qr_col % pps == 0
        args.append(qr_arr)
        specs.append(pl.BlockSpec((tq, 2 * MLA_ROPE * pps),
                                  lambda b, p, t: (q_blk0 + b * nq + t, qr_col // pps + p)))
    for sg in segs:
        lk = sg["lk"]
        k_arr, k_row0, k_col = sg["k"]
        assert k_col % pps == 0
        args.append(k_arr)
        specs.append(pl.BlockSpec((lk, 2 * dh * pps), functools.partial(
            lambda b, p, t, r0, c0: (r0 + b, c0 + p), r0=k_row0 // lk, c0=k_col // pps)))
        if has_r:
            kr_arr, kr_row0, kr_col, kr_w = sg["kr"]
            args.append(kr_arr)
            specs.append(pl.BlockSpec((lk, kr_w), functools.partial(
                lambda b, p, t, r0, c0: (r0 + b, c0), r0=kr_row0 // lk, c0=kr_col)))
        v_arr, v_row0, v_col = sg["v"]
        assert v_col % pps == 0
        args.append(v_arr)
        specs.append(pl.BlockSpec((lk, 2 * dv * pps), functools.partial(
            lambda b, p, t, r0, c0: (r0 + b, c0 + p), r0=v_row0 // lk, c0=v_col // pps)))
    if rope_tabs is not None:
        cos, sin = (jnp.concatenate([tab] * pps, axis=-1) for tab in rope_tabs)
        args += [cos, sin]
        specs += [pl.BlockSpec((tq, 128 * pps), lambda b, p, t: (t, 0)),
                  pl.BlockSpec((tq, 128 * pps), lambda b, p, t: (t, 0))]
    return pl.pallas_call(
        functools.partial(_attn_kernel, nseg=len(segs), has_r=has_r, rope=rope_tabs is not None,
                          scale=scale, dh=dh, dv=dv, pps=pps),
        out_shape=jax.ShapeDtypeStruct((nb * lq, npairs * 2 * dv), BF16),
        grid=(nb, ng, nq),
        in_specs=specs,
        out_specs=pl.BlockSpec((tq, 2 * dv * pps), lambda b, p, t: (b * nq + t, p)),
        compiler_params=_cparams(("parallel", "parallel", "arbitrary")),
        name="attention",
    )(*args)


def _na_kernel(q_ref, k_ref, v_ref, kc_ref, vc_ref, t2_ref, o_ref, bias_ref, cap_ref):
    r0 = pl.program_id(1) * NA_QR
    ks = jnp.clip(r0 - NA_WIN_ROWS // 2, 0, LAT_ROWS - NA_KR)
    nq, nk = NA_QR * GRID_W, NA_KR * GRID_W
    scale = NA_DH ** -0.5 * LOG2E

    @pl.when(pl.program_id(2) == 0)
    def _():
        row = lax.broadcasted_iota(jnp.int32, (nq, nk), 0)
        lane = lax.broadcasted_iota(jnp.int32, (nq, nk), 1)
        qc = row & (GRID_W - 1)
        kc = lane & (GRID_W - 1)
        rs = jnp.clip(r0 + (row >> LOG2_GRID_W) - NA_WIN_ROWS // 2, 0, LAT_ROWS - NA_WIN_ROWS)
        kr = ks + (lane >> LOG2_GRID_W)
        cs = jnp.clip(qc - NA_WIN_COLS // 2, 0, GRID_W - NA_WIN_COLS)
        ok = (kr >= rs) & (kr < rs + NA_WIN_ROWS) & (kc >= cs) & (kc < cs + NA_WIN_COLS)
        cap_ref[...] = jnp.where(ok, jnp.inf, NEG_INF)
        for hh in range(NA_HPS):
            for i in range(NA_QR):
                for jp in range(NA_KR // 2):
                    e = jnp.clip(ks - r0 + 2 * jp - i + NA_WIN_ROWS, 0, 2 * NA_WIN_ROWS - 1)
                    bias_ref[hh, i * GRID_W:(i + 1) * GRID_W, jp * 128:(jp + 1) * 128] = t2_ref[hh, e] * LOG2E

    start = pl.multiple_of(ks * GRID_W, GRID_W)
    kw = k_ref[pl.ds(start, nk), :]
    vw = v_ref[pl.ds(start, nk), :]
    cap = cap_ref[...]
    def head_cols(hh):
        return slice(hh * NA_DH, (hh + 1) * NA_DH)

    def qk(hh):
        q = q_ref[:, head_cols(hh)]
        return (jnp.minimum(_dot_nt(q, kw[:, head_cols(hh)]) * scale + bias_ref[hh], cap),
                _dot_nt(q, kc_ref[:, head_cols(hh)]) * scale)

    def softmax(s_loc, s_ctx):
        m = jnp.maximum(s_loc.max(axis=-1, keepdims=True), s_ctx.max(axis=-1, keepdims=True))
        e_loc = jnp.exp2(s_loc - m)
        e_ctx = jnp.exp2(s_ctx - m)
        den = e_loc.sum(axis=-1, keepdims=True) + e_ctx.sum(axis=-1, keepdims=True)
        return e_loc.astype(BF16), e_ctx.astype(BF16), den

    def pv(hh, e_loc, e_ctx, den):
        acc = _dot(e_loc, vw[:, head_cols(hh)]) + _dot(e_ctx, vc_ref[:, head_cols(hh)])
        o_ref[:, head_cols(hh)] = (acc / den).astype(o_ref.dtype)

    scores, probs = {}, {}
    for t in range(NA_HPS + 2):
        if t < NA_HPS:
            scores[t] = qk(t)
        if 0 <= t - 1 < NA_HPS:
            probs[t - 1] = softmax(*scores.pop(t - 1))
        if 0 <= t - 2 < NA_HPS:
            pv(t - 2, *probs.pop(t - 2))


def _na_bias_table(rpb):
    col = jnp.arange(GRID_W)
    col_off = jnp.clip(col[None, :] - col[:, None] + NA_WIN_COLS - 1, 0, 2 * NA_WIN_COLS - 2)
    e = jnp.arange(2 * NA_WIN_ROWS)
    dr = jnp.clip(jnp.stack([e - 1, e], axis=1), 0, 2 * NA_WIN_ROWS - 2)
    tb = rpb[:, dr][:, :, :, col_off]
    return jnp.transpose(tb, (0, 1, 3, 2, 4)).reshape(NA_HEADS, 2 * NA_WIN_ROWS, GRID_W, 2 * GRID_W)


def _na_latent(qkv_l, kc, vc, t2):
    ngroups = NA_HEADS // NA_HPS
    wg = NA_HPS * NA_DH
    nrb = LAT_ROWS // NA_QR
    nq = NA_QR * GRID_W
    return pl.pallas_call(
        _na_kernel,
        out_shape=jax.ShapeDtypeStruct((N_LAT, D), BF16),
        grid=(ngroups, nrb, DEC_BATCH),
        in_specs=[
            pl.BlockSpec((nq, wg), lambda g, r, b: (b * nrb + r, g)),
            pl.BlockSpec((DEC_SEQ, wg), lambda g, r, b: (b, ngroups + g)),
            pl.BlockSpec((DEC_SEQ, wg), lambda g, r, b: (b, 2 * ngroups + g)),
            pl.BlockSpec((PAST, wg), lambda g, r, b: (b, g)),
            pl.BlockSpec((PAST, wg), lambda g, r, b: (b, g)),
            pl.BlockSpec((NA_HPS, 2 * NA_WIN_ROWS, GRID_W, 2 * GRID_W), lambda g, r, b: (g, 0, 0, 0)),
        ],
        out_specs=pl.BlockSpec((nq, wg), lambda g, r, b: (b * nrb + r, g)),
        scratch_shapes=[pltpu.VMEM((NA_HPS, nq, NA_KR * GRID_W), F32), pltpu.VMEM((nq, NA_KR * GRID_W), F32)],
        compiler_params=_cparams(("parallel", "parallel", "arbitrary")),
        name="na_latent",
    )(qkv_l, qkv_l, qkv_l, kc, vc, t2)


def _gla_pos(t):
    is_ctx = t < GLA_CTX_STEPS
    u = jnp.maximum(t - GLA_CTX_STEPS, 0)
    seq = jnp.where(is_ctx, t // GLA_CTX_CHUNKS, u // GLA_LAT_CHUNKS)
    n = jnp.where(is_ctx, t % GLA_CTX_CHUNKS, u % GLA_LAT_CHUNKS)
    return is_ctx, seq, n


def _gla_bwd_chunk(t):
    is_ctx, _, n = _gla_pos(t)
    return t + jnp.where(is_ctx, GLA_CTX_CHUNKS, GLA_LAT_CHUNKS) - 1 - 2 * n


def _gla_kernel(qf_ref, kf_ref, vf_ref, gdf_ref, qb_ref, kb_ref, vb_ref, gdb_ref, wgu_ref, bg_ref, s0_ref,
                of_ref, ob_ref, fin_ref, st_ref):
    is_ctx, _, n = _gla_pos(pl.program_id(0))
    is_lat = jnp.logical_not(is_ctx)

    @pl.when(jnp.logical_and(n == 0, is_ctx))
    def _():
        st_ref[...] = jnp.zeros_like(st_ref)

    @pl.when(jnp.logical_and(n == 0, is_lat))
    def _():
        st_ref[...] = s0_ref[...]

    ri = lax.broadcasted_iota(jnp.int32, (GLA_CHUNK, GLA_CHUNK), 0)
    ci = lax.broadcasted_iota(jnp.int32, (GLA_CHUNK, GLA_CHUNK), 1)
    streams = ((qf_ref, kf_ref, vf_ref, gdf_ref, of_ref), (qb_ref, kb_ref, vb_ref, gdb_ref, ob_ref))
    for d, (q_ref, k_ref, v_ref, gd_ref, o_ref) in enumerate(streams):
        keep = (ci <= ri) if d == 0 else (ci >= ri)
        tri = jnp.where(keep, 1.0, 0.0).astype(BF16)
        pre = _dot(gd_ref[...].astype(BF16), wgu_ref[d]) + bg_ref[d]
        la = (jnp.minimum(pre, 0.0) - jnp.log1p(jnp.exp(-jnp.abs(pre)))) / GLA_TAU
        hi = la.astype(BF16)
        r1 = la - hi.astype(F32)
        mid = r1.astype(BF16)
        lo = (r1 - mid.astype(F32)).astype(BF16)
        cum = _dot(tri, hi) + _dot(tri, mid) + _dot(tri, lo)
        tot = jnp.sum(la, axis=0, keepdims=True)
        k = k_ref[...]
        qs = q_ref[...] * (GLA_DK ** -0.5)
        q_dec = (qs * jnp.exp(cum)).astype(BF16)
        c_mid = cum[GLA_CHUNK // 2:GLA_CHUNK // 2 + 1, :]
        q_att = (qs * jnp.exp(cum - c_mid)).astype(BF16)
        k_inv = (k * jnp.exp(c_mid - cum)).astype(BF16)
        k_end = (k * jnp.exp(tot - cum)).astype(BF16)
        dec = jnp.exp(tot)
        v = v_ref[...].astype(BF16)
        for h in range(GLA_HEADS):
            ks = slice(h * GLA_DK, (h + 1) * GLA_DK)
            vs = slice(h * GLA_DV, (h + 1) * GLA_DV)
            att = jnp.where(keep, _dot_nt(q_att[:, ks], k_inv[:, ks]), 0.0)
            st = st_ref[d, h]
            o_ref[:, vs] = _dot(att.astype(BF16), v[:, vs]) + _dot_nt(q_dec[:, ks], st.astype(BF16))
            st_ref[d, h] = st * dec[:, ks] + _dot_tn(v[:, vs], k_end[:, ks])

    @pl.when(jnp.logical_and(is_ctx, n == GLA_CTX_CHUNKS - 1))
    def _():
        fin_ref[...] = st_ref[...]


def _gla_scan(proj, wgu, bg, s0t):
    hk = GLA_HEADS * GLA_DK
    hv = GLA_HEADS * GLA_DV
    gd_col = GLA_QKVG_N // 128

    def fwd(c):
        return lambda t: (t, c)

    def bwd(c):
        return lambda t: (_gla_bwd_chunk(t), c)

    in_specs = []
    for ix in (fwd, bwd):
        in_specs += [pl.BlockSpec((GLA_CHUNK, hk), ix(0)), pl.BlockSpec((GLA_CHUNK, hk), ix(1)),
                     pl.BlockSpec((GLA_CHUNK, hv), ix(1)), pl.BlockSpec((GLA_CHUNK, 128), ix(gd_col))]
    st_block = (None, 2, GLA_HEADS, GLA_DV, GLA_DK)
    in_specs += [
        pl.BlockSpec((2, 128, hk), lambda t: (0, 0, 0)),
        pl.BlockSpec((2, 1, hk), lambda t: (0, 0, 0)),
        pl.BlockSpec(st_block, lambda t: (jnp.where(_gla_pos(t)[0], 0, _gla_pos(t)[1]), 0, 0, 0, 0)),
    ]
    o_shape = jax.ShapeDtypeStruct((N_TOK, hv), F32)
    return pl.pallas_call(
        _gla_kernel,
        out_shape=(o_shape, o_shape, jax.ShapeDtypeStruct((BATCH, 2, GLA_HEADS, GLA_DV, GLA_DK), F32)),
        grid=(GLA_STEPS,),
        in_specs=in_specs,
        out_specs=(
            pl.BlockSpec((GLA_CHUNK, hv), lambda t: (t, 0)),
            pl.BlockSpec((GLA_CHUNK, hv), lambda t: (_gla_bwd_chunk(t), 0)),
            pl.BlockSpec(st_block, lambda t: (jnp.where(_gla_pos(t)[0], _gla_pos(t)[1], BATCH - 1), 0, 0, 0, 0)),
        ),
        scratch_shapes=[pltpu.VMEM((2, GLA_HEADS, GLA_DV, GLA_DK), F32)],
        compiler_params=_cparams(("arbitrary",)),
        name="gla_scan",
    )(*([proj] * 8), wgu, bg, s0t)


def _rope_tables():
    t = jnp.arange(DEC_SEQ)
    d = MLA_ROPE // 2
    inv = ROPE_THETA ** (-jnp.arange(0, d, 2, dtype=F32) / d)
    ang_r = (t // GRID_W).astype(F32)[:, None] * inv[None]
    ang_c = (t % GRID_W).astype(F32)[:, None] * inv[None]
    cos = jnp.concatenate([jnp.cos(ang_r)] * 2 + [jnp.cos(ang_c)] * 2, axis=-1)
    sin = jnp.concatenate([-jnp.sin(ang_r), jnp.sin(ang_r), -jnp.sin(ang_c), jnp.sin(ang_c)], axis=-1)
    return jnp.concatenate([cos, cos], axis=-1), jnp.concatenate([sin, sin], axis=-1)


def _mixer_na(x, mod, layer, g, w_qkv, w_o, rpb, cache_k, cache_v):
    w_qkv = w_qkv.astype(BF16)
    ada = dict(pro="ada", g=g, mod=mod, layer=layer, jmod=1)
    q_c, k_c, v_c = _mm(x, w_qkv, rows=N_CTX, n_outs=3, out_dtype=F32, tm=512, **ada)
    qkv_l = _mm(x, w_qkv, rows=N_LAT, row_off=N_CTX // 1024, out_dtype=BF16, tn=1536, **ada)
    o_c = _attention(q_c, [dict(k=(k_c, 0, 0), v=(v_c, 0, 0), lk=SEQ)],
                     nb=BATCH, lq=SEQ, tq=SEQ, q_row0=0, q_col=0, scale=NA_DH ** -0.5, pps=CTX_PPS)
    kc = cache_k.reshape(DEC_BATCH * PAST, D).astype(BF16)
    vc = cache_v.reshape(DEC_BATCH * PAST, D).astype(BF16)
    o_l = _na_latent(qkv_l, kc, vc, _na_bias_table(rpb))
    x = _mm(o_c, w_o.astype(BF16), rows=N_TOK, pro="cast", x2=o_l, epi="resid", res=x, mod=mod, layer=layer,
            jmod=1, tm=512, tn=2048)
    return x, k_c.reshape(BATCH, SEQ, NA_HEADS, NA_DH), v_c.reshape(BATCH, SEQ, NA_HEADS, NA_DH)


def _mixer_mla(x, mod, layer, g, w_down, g_q, w_uq, g_kv, w_ukv, w_o, cache_ckv, cache_kr):
    hq = MLA_NOPE + MLA_ROPE
    wd = jnp.concatenate([
        w_down[:, :MLA_Q_LORA],
        w_down[:, MLA_Q_LORA + MLA_KV_LORA:],
        jnp.zeros((D, 256 - MLA_ROPE), F32),
        w_down[:, MLA_Q_LORA:MLA_Q_LORA + MLA_KV_LORA]], axis=1).astype(BF16)
    wq = w_uq.reshape(MLA_Q_LORA, MLA_HEADS, hq)
    wq = jnp.concatenate([wq[:, :, :MLA_NOPE].reshape(MLA_Q_LORA, -1),
                          wq[:, :, MLA_NOPE:].reshape(MLA_Q_LORA, -1)], axis=1).astype(BF16)
    wkv = w_ukv.reshape(MLA_KV_LORA, MLA_HEADS, MLA_NOPE + MLA_V)
    wkv = jnp.concatenate([wkv[:, :, :MLA_NOPE].reshape(MLA_KV_LORA, -1),
                           wkv[:, :, MLA_NOPE:].reshape(MLA_KV_LORA, -1)], axis=1).astype(BF16)

    down = _mm(x, wd, rows=N_TOK, pro="ada", g=g, mod=mod, layer=layer, jmod=1, out_dtype=F32, tn=MLA_DOWN_N)
    q = _mm(down, wq, rows=N_TOK, pro="rms", g=g_q, out_dtype=F32, tn=1536)
    kv = _mm(down, wkv, rows=N_TOK, xcol=2, pro="rms", g=g_kv, out_dtype=BF16, tn=2048)
    kv_cache = _mm(cache_ckv.reshape(DEC_BATCH * PAST, MLA_KV_LORA).astype(BF16), wkv,
                   rows=DEC_BATCH * PAST, pro="cast", out_dtype=BF16, tn=2048)
    ckv_c = _rmsnorm(down, g_kv, rows=N_CTX, xcol=2)
    kr_c = down[:N_CTX, MLA_Q_LORA:MLA_Q_LORA + MLA_ROPE]

    npairs = MLA_HEADS // 2
    kr_col = MLA_Q_LORA // 128
    o_c = _attention(q, [dict(k=(kv, 0, 0), v=(kv, 0, npairs), kr=(down, 0, kr_col, 128), lk=SEQ)],
                     nb=BATCH, lq=SEQ, tq=SEQ, q_row0=0, q_col=0, qr=(q, 2 * npairs), scale=MLA_SCALE, pps=CTX_PPS)
    tabs = _rope_tables()
    kr_l = _rope_keys(down, tabs)
    o_l = _attention(
        q,
        [dict(k=(kv, N_CTX, 0), v=(kv, N_CTX, npairs), kr=(kr_l, 0, 0, 128), lk=DEC_SEQ),
         dict(k=(kv_cache, 0, 0), v=(kv_cache, 0, npairs),
              kr=(cache_kr.reshape(DEC_BATCH * PAST, MLA_ROPE), 0, 0, MLA_ROPE), lk=PAST)],
        nb=DEC_BATCH, lq=DEC_SEQ, tq=512, q_row0=N_CTX, q_col=0, qr=(q, 2 * npairs), scale=MLA_SCALE,
        rope_tabs=tabs, pps=2)
    x = _mm(o_c, w_o.astype(BF16), rows=N_TOK, pro="cast", x2=o_l, epi="resid", res=x, mod=mod, layer=layer,
            jmod=1, tm=512, tn=2048)
    return x, ckv_c.reshape(BATCH, SEQ, MLA_KV_LORA), kr_c.reshape(BATCH, SEQ, MLA_ROPE)


def _mixer_gla(x, mod, layer, g, w_in, w_gd, w_gu, b_g, g_norm, w_o, state):
    pad = jnp.zeros((D, GLA_PROJ_N - GLA_QKVG_N - 2 * GLA_RANK), F32)
    w_cat = jnp.concatenate([w_in, w_gd[0], w_gd[1], pad], axis=1).astype(BF16)
    proj = _mm(x, w_cat, rows=N_TOK, pro="ada", g=g, mod=mod, layer=layer, jmod=1, out_dtype=F32, tn=1280)
    wgu = jnp.zeros((2, 128, GLA_HEADS * GLA_DK), F32)
    wgu = wgu.at[0, :GLA_RANK].set(w_gu[0]).at[1, GLA_RANK:2 * GLA_RANK].set(w_gu[1]).astype(BF16)
    bg = b_g.reshape(2, 1, GLA_HEADS * GLA_DK)
    o_f, o_b, st_c = _gla_scan(proj, wgu, bg, jnp.swapaxes(state, -1, -2))
    x = _mm(o_f, w_o.astype(BF16), rows=N_TOK, pro="gla", x2=o_b, gin=proj, gin_col=2, g=g_norm,
            epi="resid", res=x, mod=mod, layer=layer, jmod=1, tm=256, tn=2048)
    return x, jnp.swapaxes(st_c, -1, -2)


def kernel(x_prompt, x_sample, cache_na_k, cache_na_v, cache_mla_ckv, cache_mla_krope, state_gla, c, c_ctx, norm_g, w_ada, b_ada, w_ffn_in, w_ffn_out, w_na_qkv, w_na_o, na_rpb, w_mla_down, g_mla_q, w_mla_uq, g_mla_kv, w_mla_ukv, w_mla_o, w_gla_in, w_gla_gate_down, w_gla_gate_up, b_gla_gate, g_gla_norm, w_gla_o, final_norm_g):
    x = jnp.concatenate([x_prompt.reshape(N_CTX, D), x_sample.reshape(N_LAT, D)], axis=0)
    cond = jnp.concatenate([c_ctx[None], c, jnp.zeros((N_GROUPS - 1 - DEC_BATCH, D), F32)], axis=0)
    mod = _ada_mod(cond, w_ada, b_ada)
    w_ffn = (w_ffn_in[0, 0].astype(BF16), w_ffn_out[0, 0].astype(BF16))

    def ffn(x, w_ffn, layer, half):
        last = layer == DEPTH - 1 and half == 1
        nxt = None if last else (w_ffn_in, w_ffn_out, layer + half, 1 - half)
        out = _ffn(x, mod, layer, half, w_ffn[0], w_ffn[1], norm_g[layer, 2 * half], nxt)
        return (out, None) if last else out

    na_k, na_v, ckv, krope, gla_st = [], [], [], [], []
    for i in range(DEPTH):
        kind, slot = i % 3, i // 3
        x, w_ffn = ffn(x, w_ffn, i, 0)
        if kind == 0:
            x, k_c, v_c = _mixer_na(x, mod, i, norm_g[i, 1], w_na_qkv[slot], w_na_o[slot], na_rpb[slot],
                                    cache_na_k[:, slot], cache_na_v[:, slot])
            na_k.append(k_c)
            na_v.append(v_c)
        elif kind == 1:
            x, ckv_c, kr_c = _mixer_mla(x, mod, i, norm_g[i, 1], w_mla_down[slot], g_mla_q[slot], w_mla_uq[slot],
                                        g_mla_kv[slot], w_mla_ukv[slot], w_mla_o[slot],
                                        cache_mla_ckv[:, slot], cache_mla_krope[:, slot])
            ckv.append(ckv_c)
            krope.append(kr_c)
        else:
            x, st = _mixer_gla(x, mod, i, norm_g[i, 1], w_gla_in[slot], w_gla_gate_down[slot],
                               w_gla_gate_up[slot], b_gla_gate[slot], g_gla_norm[slot], w_gla_o[slot],
                               state_gla[:, slot])
            gla_st.append(st)
        x, w_ffn = ffn(x, w_ffn, i, 1)

    y_prompt = _rmsnorm(x, final_norm_g, rows=N_CTX).reshape(BATCH, SEQ, D)
    y_sample = _rmsnorm(x, final_norm_g, rows=N_LAT, row_off=N_CTX // 512).reshape(DEC_BATCH, DEC_SEQ, D)
    return (y_prompt, y_sample, jnp.stack(na_k, axis=1), jnp.stack(na_v, axis=1), jnp.stack(ckv, axis=1),
            jnp.stack(krope, axis=1), jnp.stack(gla_st, axis=1))
```
